```python
import math
import jax, jax.numpy as jnp
from jax import lax
import numpy as np

D_MODEL = 1024
BATCH = 4
SEQ = 8192
DEPTH = 1

CHUNK = 64
Q_BLOCK = 128
EPS = 1e-6

FOX_WIDTH = D_MODEL // 2
FOX_HEAD_DIM = 64
FOX_HEADS = FOX_WIDTH // FOX_HEAD_DIM

S5_WIDTH = D_MODEL // 2
S5_GROUP = 16
S5_GROUPS = S5_WIDTH // S5_GROUP
S5_STATE = 64
DT_MIN = 1e-3
DT_MAX = 1e-1

PROJ_SIZES = (FOX_WIDTH, FOX_WIDTH, FOX_WIDTH, FOX_HEADS, FOX_WIDTH,
              S5_WIDTH, S5_WIDTH,
              D_MODEL, D_MODEL)
PROJ_WIDTH = sum(PROJ_SIZES)

kernel_name = "fox_s5_gated_hybrid_block"


def rmsnorm(x, g):
    xf = x.astype(jnp.float32)
    y = xf * lax.rsqrt(jnp.mean(xf * xf, axis=-1, keepdims=True) + EPS)
    return (y * g.astype(jnp.float32)).astype(x.dtype)


def forgetting_attention(q, k, v, log_f):
    S = q.shape[1]
    Dh = q.shape[-1]
    F = jnp.cumsum(log_f.astype(jnp.float32), axis=1).transpose(0, 2, 1)
    scale = Dh ** -0.5
    neg = jnp.finfo(jnp.float32).min
    outs = []
    for i in range(S // Q_BLOCK):
        q0 = i * Q_BLOCK
        kend = q0 + Q_BLOCK
        qb = q[:, q0:kend]
        kb = k[:, :kend]
        vb = v[:, :kend]
        s = jnp.einsum('bqhd,bkhd->bhqk', qb, kb).astype(jnp.float32) * scale
        s = s + F[:, :, q0:kend, None] - F[:, :, None, :kend]
        t_idx = q0 + jnp.arange(Q_BLOCK)
        s_idx = jnp.arange(kend)
        s = jnp.where(s_idx[None, :] <= t_idx[:, None], s, neg)
        p = jax.nn.softmax(s, axis=-1).astype(v.dtype)
        outs.append(jnp.einsum('bhqk,bkhd->bqhd', p, vb))
    return jnp.concatenate(outs, axis=1)


def s5_ssm(u, a_re, a_im, log_dt, b_re, b_im, c_re, c_im, d_skip):
    Bsz, S, W = u.shape
    uf = u.astype(jnp.float32)
    ug = uf.reshape(Bsz, S, S5_GROUPS, S5_GROUP).astype(jnp.complex64)
    lam = lax.complex(a_re.astype(jnp.float32), a_im.astype(jnp.float32))
    dt = jnp.exp(log_dt.astype(jnp.float32))[:, None]
    a_bar = jnp.exp(lam * dt)
    bmat = lax.complex(b_re.astype(jnp.float32), b_im.astype(jnp.float32))
    b_bar = ((a_bar - 1.0) / lam)[:, :, None] * bmat
    bu = jnp.einsum('bsgc,gpc->sbgp', ug, b_bar)
    a_seq = jnp.broadcast_to(a_bar, bu.shape)

    def combine(left, right):
        return (right[0] * left[0], right[0] * left[1] + right[1])

    _, states = lax.associative_scan(combine, (a_seq, bu), axis=0)
    cmat = lax.complex(c_re.astype(jnp.float32), c_im.astype(jnp.float32))
    y = jnp.einsum('sbgp,gcp->bsgc', states, cmat).real.reshape(Bsz, S, W)
    y = y + d_skip.astype(jnp.float32) * uf
    return y.astype(u.dtype)


def setup_inputs(seed: int = 0) -> dict:
    key = jax.random.key(seed)
    ks = jax.random.split(key, 24)
    D, L, G, P, Cg = D_MODEL, DEPTH, S5_GROUPS, S5_STATE, S5_GROUP
    f32 = jnp.float32

    def nrm(k, shape, fan_in, s=1.0):
        return s * jax.random.normal(k, shape, f32) * fan_in ** -0.5

    x = jax.random.normal(ks[0], (BATCH, SEQ, D), f32)
    c = jax.random.normal(ks[1], (BATCH, D), f32)
    w_ada = nrm(ks[2], (L, D, 3 * D), D, 0.3)
    b_ada = 0.02 * jax.random.normal(ks[3], (L, 3 * D), f32)
    g_norm = 1.0 + 0.05 * jax.random.normal(ks[4], (L, D), f32)
    w_in = nrm(ks[5], (L, D, PROJ_WIDTH), D)
    b_f = 3.0 + 0.5 * jax.random.normal(ks[6], (L, FOX_HEADS), f32)
    n_idx = jnp.arange(P, dtype=f32)
    a_re = -0.5 + 0.01 * jax.random.normal(ks[7], (L, G, P), f32)
    a_im = jnp.broadcast_to(math.pi * n_idx, (L, G, P)) + 0.01 * jax.random.normal(ks[8], (L, G, P), f32)
    log_dt = jax.random.uniform(ks[9], (L, G), f32, math.log(DT_MIN), math.log(DT_MAX))
    b_re = nrm(ks[10], (L, G, P, Cg), 2 * Cg)
    b_im = nrm(ks[11], (L, G, P, Cg), 2 * Cg)
    c_re = nrm(ks[12], (L, G, Cg, P), 2 * P)
    c_im = nrm(ks[13], (L, G, Cg, P), 2 * P)
    d_skip = jax.random.normal(ks[14], (L, S5_WIDTH), f32)
    w_glu = nrm(ks[15], (L, S5_WIDTH, S5_WIDTH), S5_WIDTH)
    b_glu = 0.02 * jax.random.normal(ks[16], (L, S5_WIDTH), f32)
    w_up_a = nrm(ks[17], (L, FOX_WIDTH, D), FOX_WIDTH)
    w_up_b = nrm(ks[18], (L, S5_WIDTH, D), S5_WIDTH)
    w_out = nrm(ks[19], (L, D, D), D)
    g_final = 1.0 + 0.05 * jax.random.normal(ks[20], (D,), f32)
    return {"x": x, "c": c, "w_ada": w_ada, "b_ada": b_ada, "g_norm": g_norm,
            "w_in": w_in, "b_f": b_f, "a_re": a_re, "a_im": a_im, "log_dt": log_dt,
            "b_re": b_re, "b_im": b_im, "c_re": c_re, "c_im": c_im, "d_skip": d_skip,
            "w_glu": w_glu, "b_glu": b_glu, "w_up_a": w_up_a, "w_up_b": w_up_b,
            "w_out": w_out, "g_final": g_final}


def reference(x, c, w_ada, b_ada, g_norm, w_in, b_f, a_re, a_im, log_dt, b_re, b_im,
              c_re, c_im, d_skip, w_glu, b_glu, w_up_a, w_up_b, w_out, g_final):
    Bsz, S, D = x.shape
    offsets = []
    acc = 0
    for sz in PROJ_SIZES[:-1]:
        acc += sz
        offsets.append(acc)
    for l in range(DEPTH):
        mod = c @ w_ada[l] + b_ada[l]
        shift, scale, gate = jnp.split(mod, 3, axis=-1)
        h = rmsnorm(x, g_norm[l]) * (1.0 + scale[:, None, :]) + shift[:, None, :]

        proj = h @ w_in[l]
        q, k, v, f_logit, z_a, u, z_b, g_a, g_b = jnp.split(proj, offsets, axis=-1)

        hs = (Bsz, S, FOX_HEADS, FOX_HEAD_DIM)
        log_f = jax.nn.log_sigmoid((f_logit + b_f[l]).astype(jnp.float32))
        y_a = forgetting_attention(q.reshape(hs), k.reshape(hs), v.reshape(hs), log_f)
        y_a = y_a.reshape(Bsz, S, FOX_WIDTH) * jax.nn.silu(z_a)

        y_b = s5_ssm(u, a_re[l], a_im[l], log_dt[l], b_re[l], b_im[l], c_re[l], c_im[l], d_skip[l])
        y_b = jax.nn.gelu(y_b)
        y_b = y_b * jax.nn.sigmoid(y_b @ w_glu[l] + b_glu[l])
        y_b = y_b * jax.nn.silu(z_b)

        merged = jax.nn.sigmoid(g_a) * (y_a @ w_up_a[l]) + jax.nn.sigmoid(g_b) * (y_b @ w_up_b[l])
        x = x + gate[:, None, :] * (merged @ w_out[l])
    return rmsnorm(x, g_final)
```

```python
import functools
import math

import jax
import jax.numpy as jnp
from jax import lax
from jax.experimental import pallas as pl
from jax.experimental.pallas import tpu as pltpu

D_MODEL = 1024
EPS = 1e-6
FOX_WIDTH = 512
HEAD_DIM = 64
HEADS = 8
HEAD_PAIRS = HEADS // 2
S5_WIDTH = 512
S5_GROUP = 16
S5_GROUPS = 32
S5_STATE = 64

LANES = 128
SUBLANES = 8
VMEM_LIMIT = 56 * 1024 * 1024

S5_HALVES = 2
S5_HALF_CH = S5_WIDTH // S5_HALVES
S5_HALF_STATE = S5_GROUPS * S5_STATE // S5_HALVES
S5_SLABS = S5_HALF_STATE // LANES

NEG_BIG = -1e30

_F32 = jnp.float32
_BF16 = jnp.bfloat16


def _dot(a, b):
    return jnp.dot(a, b, preferred_element_type=_F32)


def _sigmoid(x):
    return 1.0 / (1.0 + jnp.exp(-x))


def _silu(x):
    return x * _sigmoid(x)


def _adaln_kernel(c_ref, w_ref, b_ref, o_ref):
    o_ref[...] = jnp.dot(c_ref[...], w_ref[...], precision=lax.Precision.HIGHEST,
                         preferred_element_type=_F32) + b_ref[...]


def _adaln_mod(c, w, b):
    bsz, d = c.shape
    n = w.shape[1]
    bn = 1024
    return pl.pallas_call(
        _adaln_kernel,
        grid=(n // bn,),
        in_specs=[pl.BlockSpec((bsz, d), lambda j: (0, 0)),
                  pl.BlockSpec((d, bn), lambda j: (0, j)),
                  pl.BlockSpec((1, bn), lambda j: (0, j))],
        out_specs=pl.BlockSpec((bsz, bn), lambda j: (0, j)),
        out_shape=jax.ShapeDtypeStruct((bsz, n), _F32),
        name="adaln_mod",
    )(c, w, b.reshape(1, n))


def _s5_param_kernel(are_ref, aim_ref, ldt_ref, bre_ref, bim_ref,
                     abr_ref, abi_ref, bbr_ref, bbi_ref):
    lr = are_ref[...]
    li = aim_ref[...]
    dt = jnp.exp(ldt_ref[...])
    mag = jnp.exp(lr * dt)
    ang = li * dt
    abr = mag * jnp.cos(ang)
    abi = mag * jnp.sin(ang)
    abr_ref[...] = abr
    abi_ref[...] = abi
    nr = abr - 1.0
    ni = abi
    den = lr * lr + li * li
    cr = (nr * lr + ni * li) / den
    ci = (ni * lr - nr * li) / den
    cr3 = cr[:, None, :]
    ci3 = ci[:, None, :]
    br = bre_ref[...]
    bi = bim_ref[...]
    bbr_ref[...] = cr3 * br - ci3 * bi
    bbi_ref[...] = cr3 * bi + ci3 * br


def _s5_params(a_re, a_im, log_dt, b_re_t, b_im_t):
    g, p = a_re.shape
    cg = b_re_t.shape[1]
    return pl.pallas_call(
        _s5_param_kernel,
        out_shape=(jax.ShapeDtypeStruct((g, p), _F32), jax.ShapeDtypeStruct((g, p), _F32),
                   jax.ShapeDtypeStruct((g, cg, p), _F32), jax.ShapeDtypeStruct((g, cg, p), _F32)),
        name="s5_params",
    )(a_re, a_im, log_dt.reshape(g, 1), b_re_t, b_im_t)


def _in_proj_kernel(x_ref, gn_ref, sc_ref, sh_ref, wqkv_ref, wf_ref, wz_ref, wg_ref, bf_ref,
                    q_ref, k_ref, v_ref, fcol_ref, frow_ref, za_ref, u_ref, zb_ref, ga_ref, gb_ref,
                    carry_ref, *, tm):
    i = pl.program_id(1)

    @pl.when(i == 0)
    def _():
        carry_ref[...] = jnp.zeros_like(carry_ref)

    x = x_ref[0]
    ms = jnp.mean(x * x, axis=-1, keepdims=True)
    h = x * lax.rsqrt(ms + EPS) * gn_ref[...]
    h = (h * (1.0 + sc_ref[0]) + sh_ref[0]).astype(_BF16)

    w = FOX_WIDTH
    q_ref[0] = (_dot(h, wqkv_ref[:, 0:w]) * (HEAD_DIM ** -0.5)).astype(_BF16)
    k_ref[0] = _dot(h, wqkv_ref[:, w:2 * w]).astype(_BF16)
    v_ref[0] = _dot(h, wqkv_ref[:, 2 * w:3 * w]).astype(_BF16)
    za_ref[0] = _silu(_dot(h, wz_ref[:, 0:w])).astype(_BF16)
    u_ref[0] = _dot(h, wz_ref[:, w:2 * w]).astype(_BF16)
    zb_ref[0] = _silu(_dot(h, wz_ref[:, 2 * w:3 * w])).astype(_BF16)
    ga_ref[0] = _sigmoid(_dot(h, wg_ref[:, 0:D_MODEL])).astype(_BF16)
    gb_ref[0] = _sigmoid(_dot(h, wg_ref[:, D_MODEL:2 * D_MODEL])).astype(_BF16)

    fl = _dot(h, wf_ref[...]) + bf_ref[...]
    logf = -(jnp.maximum(-fl, 0.0) + jnp.log1p(jnp.exp(-jnp.abs(fl))))
    hi = logf.astype(_BF16)
    r1 = logf - hi.astype(_F32)
    mid = r1.astype(_BF16)
    lo = (r1 - mid.astype(_F32)).astype(_BF16)
    row = lax.broadcasted_iota(jnp.int32, (tm, tm), 0)
    col = lax.broadcasted_iota(jnp.int32, (tm, tm), 1)
    tri = (col <= row).astype(_BF16)
    parts = _dot(tri, jnp.concatenate([hi, mid, lo], axis=1))
    cum = (parts[:, 0:LANES] + parts[:, LANES:2 * LANES]) + parts[:, 2 * LANES:3 * LANES]
    cum = cum + carry_ref[...]
    carry_ref[...] = cum[tm - 1:tm, :]
    fcol_ref[0] = cum
    frow_ref[0] = cum.T[0:HEADS, :]


def _in_proj(x, g_norm, scale, shift, wqkv, wf, wz, wg, bf_pad, tm=512):
    bsz, s, d = x.shape
    ni = s // tm
    tok = lambda width, dt: jax.ShapeDtypeStruct((bsz, s, width), dt)
    tok_spec = lambda width: pl.BlockSpec((1, tm, width), lambda b, i: (b, i, 0))
    const = lambda shape: pl.BlockSpec(shape, lambda b, i: (0,) * len(shape))
    per_b = pl.BlockSpec((1, 1, d), lambda b, i: (b, 0, 0))
    return pl.pallas_call(
        functools.partial(_in_proj_kernel, tm=tm),
        grid=(bsz, ni),
        in_specs=[tok_spec(d), const((1, d)), per_b, per_b,
                  const(wqkv.shape), const(wf.shape), const(wz.shape), const(wg.shape),
                  const((1, LANES))],
        out_specs=[tok_spec(FOX_WIDTH), tok_spec(FOX_WIDTH), tok_spec(FOX_WIDTH),
                   tok_spec(LANES),
                   pl.BlockSpec((1, HEADS, tm), lambda b, i: (b, 0, i)),
                   tok_spec(FOX_WIDTH), tok_spec(S5_WIDTH), tok_spec(S5_WIDTH),
                   tok_spec(d), tok_spec(d)],
        out_shape=[tok(FOX_WIDTH, _BF16), tok(FOX_WIDTH, _BF16), tok(FOX_WIDTH, _BF16),
                   tok(LANES, _F32),
                   jax.ShapeDtypeStruct((bsz, HEADS, s), _F32),
                   tok(FOX_WIDTH, _BF16), tok(S5_WIDTH, _BF16), tok(S5_WIDTH, _BF16),
                   tok(d, _BF16), tok(d, _BF16)],
        scratch_shapes=[pltpu.VMEM((1, LANES), _F32)],
        compiler_params=pltpu.CompilerParams(
            dimension_semantics=("arbitrary", "arbitrary"), vmem_limit_bytes=VMEM_LIMIT),
        name="in_proj",
    )(x, g_norm.reshape(1, d), scale, shift, wqkv, wf, wz, wg, bf_pad)


def _attn_kernel(q_ref, k_ref, v_ref, fcol_ref, frow_ref, za_ref, o_ref, *, tq, tk):
    hp = pl.program_id(1)
    i = pl.program_id(2)
    lane = lax.broadcasted_iota(jnp.int32, (1, LANES), 1)
    first = lane < HEAD_DIM

    q = q_ref[0]
    zero = jnp.zeros_like(q)
    qs = (jnp.where(first, q, zero), jnp.where(first, zero, q))
    fc = fcol_ref[0]
    fqs = tuple(jnp.sum(jnp.where(lane == 2 * hp + hh, fc, 0.0), axis=1, keepdims=True)
                for hh in range(2))

    def step(j, carry, masked):
        (m0, l0, m1, l1, acc) = carry
        ms, ls = [m0, m1], [l0, l1]
        k0 = pl.multiple_of(j * tk, tk)
        kb = k_ref[0, pl.ds(k0, tk), :]
        vb = v_ref[0, pl.ds(k0, tk), :]
        outs, alphas = [], []
        for hh in range(2):
            s = lax.dot_general(qs[hh], kb, (((1,), (1,)), ((), ())),
                                preferred_element_type=_F32)
            fk = frow_ref[0, pl.ds(2 * hp + hh, 1), pl.ds(k0, tk)]
            s = s + (fqs[hh] - fk)
            if masked:
                r = lax.broadcasted_iota(jnp.int32, (tq, tk), 0)
                c = lax.broadcasted_iota(jnp.int32, (tq, tk), 1)
                s = jnp.where(c <= r, s, NEG_BIG)
            m_new = jnp.maximum(ms[hh], jnp.max(s, axis=1, keepdims=True))
            alpha = jnp.exp(ms[hh] - m_new)
            p = jnp.exp(s - m_new)
            ls[hh] = alpha * ls[hh] + jnp.sum(p, axis=1, keepdims=True)
            ms[hh] = m_new
            outs.append(_dot(p.astype(_BF16), vb))
            alphas.append(alpha)
        acc = jnp.where(first, alphas[0], alphas[1]) * acc + jnp.where(first, outs[0], outs[1])
        return (ms[0], ls[0], ms[1], ls[1], acc)

    init_m = jnp.full((tq, 1), NEG_BIG, _F32)
    init_l = jnp.zeros((tq, 1), _F32)
    carry = (init_m, init_l, init_m, init_l, jnp.zeros((tq, LANES), _F32))
    carry = lax.fori_loop(0, i, lambda j, c: step(j, c, False), carry)
    (m0, l0, m1, l1, acc) = step(i, carry, True)
    y = acc / jnp.where(first, l0, l1)
    o_ref[0] = (y * za_ref[0].astype(_F32)).astype(_BF16)


def _fox_attention(q, k, v, fcol, frow, za, tq=256):
    bsz, s, _ = q.shape
    tk = tq
    nq = s // tq
    qspec = pl.BlockSpec((1, tq, LANES), lambda b, hp, i: (b, i, hp))
    kvspec = pl.BlockSpec((1, s, LANES), lambda b, hp, i: (b, 0, hp))
    return pl.pallas_call(
        functools.partial(_attn_kernel, tq=tq, tk=tk),
        grid=(bsz, HEAD_PAIRS, nq),
        in_specs=[qspec, kvspec, kvspec,
                  pl.BlockSpec((1, tq, LANES), lambda b, hp, i: (b, i, 0)),
                  pl.BlockSpec((1, HEADS, s), lambda b, hp, i: (b, 0, 0)),
                  qspec],
        out_specs=qspec,
        out_shape=jax.ShapeDtypeStruct((bsz, s, FOX_WIDTH), _BF16),
        compiler_params=pltpu.CompilerParams(
            dimension_semantics=("arbitrary", "arbitrary", "arbitrary"), vmem_limit_bytes=VMEM_LIMIT),
        name="fox_attn",
    )(q, k, v, fcol, frow, za)


def _gelu_tanh(x):
    return 0.5 * x * (1.0 + jnp.tanh(math.sqrt(2.0 / math.pi) * (x + 0.044715 * (x * x * x))))


def _s5_kernel(u_ref, zb_ref, ar_ref, ai_ref, bblk_ref, cblk_ref, dskip_ref, wglu_ref, bglu_ref,
               o_ref, xr_ref, xi_ref, sr_ref, si_ref, *, tt, nbatch, slabs_per_pass):
    t = pl.program_id(0)
    nchain = nbatch * S5_HALVES

    @pl.when(t == 0)
    def _():
        sr_ref[...] = jnp.zeros_like(sr_ref)
        si_ref[...] = jnp.zeros_like(si_ref)

    for kc in range(nchain):
        b, hf = divmod(kc, S5_HALVES)
        ub = u_ref[b, :, hf * S5_HALF_CH:(hf + 1) * S5_HALF_CH]
        bu = _dot(ub, bblk_ref[hf])
        for j in range(S5_SLABS):
            xr_ref[j, pl.ds(kc, tt, stride=nchain), :] = bu[:, j * LANES:(j + 1) * LANES]
            xi_ref[j, pl.ds(kc, tt, stride=nchain), :] = bu[:, (S5_SLABS + j) * LANES:(S5_SLABS + j + 1) * LANES]

    for j0 in range(0, S5_SLABS, slabs_per_pass):
        js = list(range(j0, j0 + slabs_per_pass))
        ars = [ar_ref[j] for j in js]
        ais = [ai_ref[j] for j in js]

        def body(s, carry, js=js, ars=ars, ais=ais):
            r0 = pl.multiple_of(s * nchain, nchain)
            new = []
            for n, j in enumerate(js):
                sr, si = carry[2 * n], carry[2 * n + 1]
                br = xr_ref[j, pl.ds(r0, nchain), :]
                bi = xi_ref[j, pl.ds(r0, nchain), :]
                nr = (ars[n] * sr - ais[n] * si) + br
                ni = (ars[n] * si + ais[n] * sr) + bi
                xr_ref[j, pl.ds(r0, nchain), :] = nr
                xi_ref[j, pl.ds(r0, nchain), :] = ni
                new += [nr, ni]
            return tuple(new)

        init = []
        for j in js:
            init += [sr_ref[j], si_ref[j]]
        fin = lax.fori_loop(0, tt, body, tuple(init), unroll=4)
        for n, j in enumerate(js):
            sr_ref[j] = fin[2 * n]
            si_ref[j] = fin[2 * n + 1]

    for b in range(nbatch):
        halves = []
        for hf in range(S5_HALVES):
            kc = b * S5_HALVES + hf
            xs = [xr_ref[j, pl.ds(kc, tt, stride=nchain), :] for j in range(S5_SLABS)]
            xs += [xi_ref[j, pl.ds(kc, tt, stride=nchain), :] for j in range(S5_SLABS)]
            xcat = jnp.concatenate(xs, axis=1).astype(_BF16)
            halves.append(_dot(xcat, cblk_ref[hf]))
        y = jnp.concatenate(halves, axis=1)
        y = y + dskip_ref[...] * u_ref[b].astype(_F32)
        y = _gelu_tanh(y)
        y = y * _sigmoid(_dot(y.astype(_BF16), wglu_ref[...]) + bglu_ref[...])
        o_ref[b] = (y * zb_ref[b].astype(_F32)).astype(_BF16)


def _s5_scan(u, zb, ar_t, ai_t, bblk, cblk, d_skip, w_glu, b_glu, tt=256, slabs_per_pass=4):
    bsz, s, w = u.shape
    nchain = bsz * S5_HALVES
    assert nchain == SUBLANES, "the scan packs batch x state-half chains into the 8 sublanes"
    tok_spec = pl.BlockSpec((bsz, tt, w), lambda t: (0, t, 0))
    const = lambda shape: pl.BlockSpec(shape, lambda t: (0,) * len(shape))
    return pl.pallas_call(
        functools.partial(_s5_kernel, tt=tt, nbatch=bsz, slabs_per_pass=slabs_per_pass),
        grid=(s // tt,),
        in_specs=[tok_spec, tok_spec, const(ar_t.shape), const(ai_t.shape),
                  const(bblk.shape), const(cblk.shape), const((1, w)), const(w_glu.shape), const((1, w))],
        out_specs=tok_spec,
        out_shape=jax.ShapeDtypeStruct((bsz, s, w), _BF16),
        scratch_shapes=[pltpu.VMEM((S5_SLABS, tt * nchain, LANES), _F32),
                        pltpu.VMEM((S5_SLABS, tt * nchain, LANES), _F32),
                        pltpu.VMEM((S5_SLABS, nchain, LANES), _F32),
                        pltpu.VMEM((S5_SLABS, nchain, LANES), _F32)],
        compiler_params=pltpu.CompilerParams(
            dimension_semantics=("arbitrary",), vmem_limit_bytes=VMEM_LIMIT),
        name="s5_scan",
    )(u, zb, ar_t, ai_t, bblk, cblk, d_skip.reshape(1, w), w_glu, b_glu.reshape(1, w))


def _out_kernel(x_ref, ya_ref, yb_ref, ga_ref, gb_ref, gate_ref, wua_ref, wub_ref, wo_ref, gf_ref, o_ref):
    ma = ga_ref[0].astype(_F32) * _dot(ya_ref[0], wua_ref[...])
    mb = gb_ref[0].astype(_F32) * _dot(yb_ref[0], wub_ref[...])
    merged = (ma + mb).astype(_BF16)
    xn = x_ref[0] + gate_ref[0] * _dot(merged, wo_ref[...])
    ms = jnp.mean(xn * xn, axis=-1, keepdims=True)
    o_ref[0] = xn * lax.rsqrt(ms + EPS) * gf_ref[...]


def _out_proj(x, ya, yb, ga, gb, gate, w_up_a, w_up_b, w_out, g_final, tm=512):
    bsz, s, d = x.shape
    tok_spec = lambda width: pl.BlockSpec((1, tm, width), lambda b, i: (b, i, 0))
    const = lambda shape: pl.BlockSpec(shape, lambda b, i: (0,) * len(shape))
    return pl.pallas_call(
        _out_kernel,
        grid=(bsz, s // tm),
        in_specs=[tok_spec(d), tok_spec(FOX_WIDTH), tok_spec(S5_WIDTH), tok_spec(d), tok_spec(d),
                  pl.BlockSpec((1, 1, d), lambda b, i: (b, 0, 0)),
                  const(w_up_a.shape), const(w_up_b.shape), const(w_out.shape), const((1, d))],
        out_specs=tok_spec(d),
        out_shape=jax.ShapeDtypeStruct((bsz, s, d), x.dtype),
        compiler_params=pltpu.CompilerParams(
            dimension_semantics=("arbitrary", "arbitrary"), vmem_limit_bytes=VMEM_LIMIT),
        name="out_proj",
    )(x, ya, yb, ga, gb, gate, w_up_a, w_up_b, w_out, g_final.reshape(1, d))


def _block_diag_in(bb):
    gh = S5_GROUPS // S5_HALVES
    bb = bb.reshape(S5_HALVES, gh, S5_GROUP, S5_STATE)
    eye = jnp.eye(gh, dtype=bb.dtype)
    out = bb[:, :, :, None, :] * eye[None, :, None, :, None]
    return out.reshape(S5_HALVES, gh * S5_GROUP, gh * S5_STATE)


def _block_diag_out(cc):
    gh = S5_GROUPS // S5_HALVES
    cc = cc.reshape(S5_HALVES, gh, S5_GROUP, S5_STATE)
    eye = jnp.eye(gh, dtype=cc.dtype)
    out = jnp.swapaxes(cc, 2, 3)[:, :, :, None, :] * eye[None, :, None, :, None]
    return out.reshape(S5_HALVES, gh * S5_STATE, gh * S5_GROUP)


def _chain_tiles(a, nbatch):
    a = a.reshape(S5_HALVES, S5_SLABS, LANES)
    a = jnp.swapaxes(a, 0, 1)
    return jnp.tile(a, (1, nbatch, 1))


def kernel(x, c, w_ada, b_ada, g_norm, w_in, b_f, a_re, a_im, log_dt, b_re, b_im, c_re, c_im,
           d_skip, w_glu, b_glu, w_up_a, w_up_b, w_out, g_final):
    bsz, s, d = x.shape
    depth = w_ada.shape[0]
    assert depth == 1 and d == D_MODEL

    mod = _adaln_mod(c, w_ada[0], b_ada[0])
    shift = mod[:, 0:d].reshape(bsz, 1, d)
    scale = mod[:, d:2 * d].reshape(bsz, 1, d)
    gate = mod[:, 2 * d:3 * d].reshape(bsz, 1, d)

    w = w_in[0]
    o_f = 3 * FOX_WIDTH
    o_z = o_f + HEADS
    o_g = o_z + FOX_WIDTH + 2 * S5_WIDTH
    wqkv = w[:, 0:o_f].astype(_BF16)
    wf = jnp.pad(w[:, o_f:o_z], ((0, 0), (0, LANES - HEADS))).astype(_BF16)
    wz = w[:, o_z:o_g].astype(_BF16)
    wg = w[:, o_g:].astype(_BF16)
    bf_pad = jnp.pad(b_f[0], (0, LANES - HEADS)).reshape(1, LANES)

    q, k, v, fcol, frow, za, u, zb, ga, gb = _in_proj(x, g_norm[0], scale, shift, wqkv, wf, wz, wg, bf_pad)

    ya = _fox_attention(q, k, v, fcol, frow, za)

    abr, abi, bbr, bbi = _s5_params(a_re[0], a_im[0], log_dt[0],
                                    jnp.swapaxes(b_re[0], 1, 2), jnp.swapaxes(b_im[0], 1, 2))
    bblk = jnp.concatenate([_block_diag_in(bbr), _block_diag_in(bbi)], axis=2).astype(_BF16)
    cblk = jnp.concatenate([_block_diag_out(c_re[0]), -_block_diag_out(c_im[0])], axis=1).astype(_BF16)
    yb = _s5_scan(u, zb, _chain_tiles(abr, bsz), _chain_tiles(abi, bsz), bblk, cblk,
                  d_skip[0], w_glu[0].astype(_BF16), b_glu[0])

    return _out_proj(x, ya, yb, ga, gb, gate, w_up_a[0].astype(_BF16), w_up_b[0].astype(_BF16),
                     w_out[0].astype(_BF16), g_final)
```

```python
import functools
import math

import jax
import jax.numpy as jnp
from jax import lax
from jax.experimental import pallas as pl
from jax.experimental.pallas import tpu as pltpu

D_MODEL = 1024
EPS = 1e-6
FOX_WIDTH = 512
HEAD_DIM = 64
HEADS = 8
HEAD_PAIRS = HEADS // 2
S5_WIDTH = 512
S5_GROUP = 16
S5_GROUPS = 32
S5_STATE = 64

LANES = 128
SUBLANES = 8
VMEM_LIMIT = 56 * 1024 * 1024

S5_HALVES = 2
S5_HALF_CH = S5_WIDTH // S5_HALVES
S5_HALF_STATE = S5_GROUPS * S5_STATE // S5_HALVES
S5_SLABS = S5_HALF_STATE // LANES

NEG_BIG = -1e30
LOG2E = math.log2(math.e)

_F32 = jnp.float32
_BF16 = jnp.bfloat16


def _dot(a, b):
    return jnp.dot(a, b, preferred_element_type=_F32)


def _split3(x):
    hi = x.astype(_BF16).astype(_F32)
    r1 = x - hi
    mid = r1.astype(_BF16).astype(_F32)
    lo = (r1 - mid).astype(_BF16).astype(_F32)
    return hi, mid, lo


def _sigmoid(x):
    return 1.0 / (1.0 + jnp.exp(-x))


def _silu(x):
    return x * _sigmoid(x)


def _adaln_kernel(c_ref, w_ref, b_ref, o_ref):
    o_ref[...] = jnp.dot(c_ref[...], w_ref[...], precision=lax.Precision.HIGHEST,
                         preferred_element_type=_F32) + b_ref[...]


def _adaln_mod(c, w, b):
    bsz, d = c.shape
    n = w.shape[1]
    bn = 1024
    return pl.pallas_call(
        _adaln_kernel,
        grid=(n // bn,),
        in_specs=[pl.BlockSpec((bsz, d), lambda j: (0, 0)),
                  pl.BlockSpec((d, bn), lambda j: (0, j)),
                  pl.BlockSpec((1, bn), lambda j: (0, j))],
        out_specs=pl.BlockSpec((bsz, bn), lambda j: (0, j)),
        out_shape=jax.ShapeDtypeStruct((bsz, n), _F32),
        name="adaln_mod",
    )(c, w, b.reshape(1, n))


def _s5_param_kernel(are_ref, aim_ref, ldt_ref, bre_ref, bim_ref,
                     abr_ref, abi_ref, bbr_ref, bbi_ref):
    lr = are_ref[...]
    li = aim_ref[...]
    dt = jnp.exp(ldt_ref[...])
    mag = jnp.exp(lr * dt)
    ang = li * dt
    abr = mag * jnp.cos(ang)
    abi = mag * jnp.sin(ang)
    abr_ref[...] = abr
    abi_ref[...] = abi
    nr = abr - 1.0
    ni = abi
    den = lr * lr + li * li
    cr = (nr * lr + ni * li) / den
    ci = (ni * lr - nr * li) / den
    cr3 = cr[:, None, :]
    ci3 = ci[:, None, :]
    br = bre_ref[...]
    bi = bim_ref[...]
    bbr_ref[...] = cr3 * br - ci3 * bi
    bbi_ref[...] = cr3 * bi + ci3 * br


def _s5_params(a_re, a_im, log_dt, b_re_t, b_im_t):
    g, p = a_re.shape
    cg = b_re_t.shape[1]
    return pl.pallas_call(
        _s5_param_kernel,
        out_shape=(jax.ShapeDtypeStruct((g, p), _F32), jax.ShapeDtypeStruct((g, p), _F32),
                   jax.ShapeDtypeStruct((g, cg, p), _F32), jax.ShapeDtypeStruct((g, cg, p), _F32)),
        name="s5_params",
    )(a_re, a_im, log_dt.reshape(g, 1), b_re_t, b_im_t)


def _in_proj_kernel(x_ref, gn_ref, sc_ref, sh_ref, wq_ref, wkt_ref, wv_ref, wf_ref, wz_ref, wg_ref, bf_ref,
                    q_ref, kt_ref, v_ref, qaug_ref, kaugt_ref, za_ref, u_ref, zb_ref, ga_ref, gb_ref,
                    carry_ref, *, tm):
    i = pl.program_id(1)

    @pl.when(i == 0)
    def _():
        carry_ref[...] = jnp.zeros_like(carry_ref)

    x = x_ref[0]
    ms = jnp.mean(x * x, axis=-1, keepdims=True)
    h = x * lax.rsqrt(ms + EPS) * gn_ref[...]
    h = (h * (1.0 + sc_ref[0]) + sh_ref[0]).astype(_BF16)

    w = FOX_WIDTH
    q_ref[0] = (_dot(h, wq_ref[...]) * (HEAD_DIM ** -0.5 * LOG2E)).astype(_BF16)
    kt_ref[0] = lax.dot_general(wkt_ref[...], h, (((1,), (1,)), ((), ())),
                                preferred_element_type=_F32).astype(_BF16)
    v_ref[0] = _dot(h, wv_ref[...]).astype(_BF16)
    za_ref[0] = _silu(_dot(h, wz_ref[:, 0:w])).astype(_BF16)
    u_ref[0] = _dot(h, wz_ref[:, w:2 * w]).astype(_BF16)
    zb_ref[0] = _silu(_dot(h, wz_ref[:, 2 * w:3 * w])).astype(_BF16)
    ga_ref[0] = _sigmoid(_dot(h, wg_ref[:, 0:D_MODEL])).astype(_BF16)
    gb_ref[0] = _sigmoid(_dot(h, wg_ref[:, D_MODEL:2 * D_MODEL])).astype(_BF16)

    fl = _dot(h, wf_ref[...]) + bf_ref[...]
    logf = -(jnp.maximum(-fl, 0.0) + jnp.log1p(jnp.exp(-jnp.abs(fl)))) * LOG2E
    hi, mid, lo = _split3(logf)
    row = lax.broadcasted_iota(jnp.int32, (tm, tm), 0)
    col = lax.broadcasted_iota(jnp.int32, (tm, tm), 1)
    tri = (col <= row).astype(_BF16)
    parts = _dot(tri, jnp.concatenate([hi, mid, lo], axis=1).astype(_BF16))
    cum = (parts[:, 0:LANES] + parts[:, LANES:2 * LANES]) + parts[:, 2 * LANES:3 * LANES]
    cum = cum + carry_ref[...]
    carry_ref[...] = cum[tm - 1:tm, :]

    lane = lax.broadcasted_iota(jnp.int32, (1, LANES), 1)
    fq = jnp.where(lane < HEADS, cum, 0.0)
    qh, qm, ql = _split3(fq)
    qaug_ref[0] = (qh + pltpu.roll(qm, HEADS, 1) + pltpu.roll(ql, 2 * HEADS, 1)).astype(_BF16)
    kh, km, kl = _split3(cum.T[0:HEADS, :])
    ones = jnp.ones((HEADS, tm), _F32)
    zeros = jnp.zeros((HEADS, tm), _F32)
    kaugt_ref[0] = jnp.concatenate([ones, ones, ones, zeros, kh, km, kl] + [zeros] * 9, axis=0).astype(_BF16)


def _in_proj(x, g_norm, scale, shift, wq, wkt, wv, wf, wz, wg, bf_pad, tm=512):
    bsz, s, d = x.shape
    ni = s // tm
    tok = lambda width, dt: jax.ShapeDtypeStruct((bsz, s, width), dt)
    tok_spec = lambda width: pl.BlockSpec((1, tm, width), lambda b, i: (b, i, 0))
    tr = lambda rows: jax.ShapeDtypeStruct((bsz, rows, s), _BF16)
    tr_spec = lambda rows: pl.BlockSpec((1, rows, tm), lambda b, i: (b, 0, i))
    const = lambda shape: pl.BlockSpec(shape, lambda b, i: (0,) * len(shape))
    per_b = pl.BlockSpec((1, 1, d), lambda b, i: (b, 0, 0))
    return pl.pallas_call(
        functools.partial(_in_proj_kernel, tm=tm),
        grid=(bsz, ni),
        in_specs=[tok_spec(d), const((1, d)), per_b, per_b,
                  const(wq.shape), const(wkt.shape), const(wv.shape), const(wf.shape),
                  const(wz.shape), const(wg.shape), const((1, LANES))],
        out_specs=[tok_spec(FOX_WIDTH), tr_spec(FOX_WIDTH), tok_spec(FOX_WIDTH),
                   tok_spec(LANES), tr_spec(LANES),
                   tok_spec(FOX_WIDTH), tok_spec(S5_WIDTH), tok_spec(S5_WIDTH),
                   tok_spec(d), tok_spec(d)],
        out_shape=[tok(FOX_WIDTH, _BF16), tr(FOX_WIDTH), tok(FOX_WIDTH, _BF16),
                   tok(LANES, _BF16), tr(LANES),
                   tok(FOX_WIDTH, _BF16), tok(S5_WIDTH, _BF16), tok(S5_WIDTH, _BF16),
                   tok(d, _BF16), tok(d, _BF16)],
        scratch_shapes=[pltpu.VMEM((1, LANES), _F32)],
        compiler_params=pltpu.CompilerParams(
            dimension_semantics=("arbitrary", "arbitrary"), vmem_limit_bytes=VMEM_LIMIT),
        name="in_proj",
    )(x, g_norm.reshape(1, d), scale, shift, wq, wkt, wv, wf, wz, wg, bf_pad)


def _attn_kernel(q_ref, kt_ref, kaugt_ref, v_ref, qaug_ref, za_ref, o_ref,
                 lhs_ref, s_ref, p_ref, m_ref, alpha_ref, acc_ref, *, tq, tk, rb):
    hp = pl.program_id(1)
    i = pl.program_id(2)
    lane = lax.broadcasted_iota(jnp.int32, (1, LANES), 1)
    q = q_ref[0]
    qa = qaug_ref[0]

    head_lanes = [(lane >= hh * HEAD_DIM) & (lane < (hh + 1) * HEAD_DIM) for hh in range(2)]
    for hh in range(2):
        h = 2 * hp + hh
        f_lanes = (lane == h) | (lane == HEADS + h) | (lane == 2 * HEADS + h)
        neg_lanes = (lane == 4 * HEADS + h) | (lane == 5 * HEADS + h) | (lane == 6 * HEADS + h)
        qm = jnp.where(head_lanes[hh], q, jnp.zeros_like(q))
        qaug = jnp.where(f_lanes, qa, jnp.where(neg_lanes, -1.0, 0.0).astype(_BF16))
        lhs_ref[hh] = jnp.concatenate([qm, qaug], axis=1)
        m_ref[hh] = jnp.full((tq, LANES), NEG_BIG, _F32)
        acc_ref[hh] = jnp.zeros((tq, LANES), _F32)

    def step(j, masked):
        k0 = pl.multiple_of(j * tk, tk)
        w = jnp.concatenate([kt_ref[0, :, pl.ds(k0, tk)], kaugt_ref[0, :, pl.ds(k0, tk)]], axis=0)
        vb = v_ref[0, pl.ds(k0, tk), :]
        for hh in range(2):
            s = _dot(lhs_ref[hh], w)
            if masked:
                r = lax.broadcasted_iota(jnp.int32, (tq, tk), 0)
                c = lax.broadcasted_iota(jnp.int32, (tq, tk), 1)
                s = jnp.where(c <= r, s, NEG_BIG)
            s_ref[hh] = s
        for hh in range(2):
            for r0 in range(0, tq, rb):
                sr = s_ref[hh, r0:r0 + rb, :]
                m_old = m_ref[hh, r0:r0 + rb, 0:1]
                m_new = jnp.maximum(m_old, jnp.max(sr, axis=1, keepdims=True))
                alpha_ref[hh, r0:r0 + rb, :] = jnp.broadcast_to(jnp.exp2(m_old - m_new), (rb, LANES))
                m_ref[hh, r0:r0 + rb, :] = jnp.broadcast_to(m_new, (rb, LANES))
                p_ref[hh, r0:r0 + rb, :] = jnp.exp2(sr - m_new).astype(_BF16)
        for hh in range(2):
            sum_lane = (1 - hh) * HEAD_DIM
            vaug = jnp.where(head_lanes[hh], vb, jnp.where(lane == sum_lane, 1.0, 0.0).astype(_BF16))
            acc_ref[hh] = alpha_ref[hh] * acc_ref[hh] + _dot(p_ref[hh], vaug)

    def full_step(j, carry):
        step(j, False)
        return carry

    lax.fori_loop(0, i, full_step, 0)
    step(i, True)

    a0 = acc_ref[0]
    a1 = acc_ref[1]
    l0 = a0[:, HEAD_DIM:HEAD_DIM + 1]
    l1 = a1[:, 0:1]
    y = jnp.where(lane < HEAD_DIM, a0 / l0, a1 / l1)
    o_ref[0] = (y * za_ref[0].astype(_F32)).astype(_BF16)


def _fox_attention(q, kt, kaugt, v, qaug, za, tq=512, rb=32):
    bsz, s, _ = q.shape
    tk = tq
    nq = s // tq
    qspec = pl.BlockSpec((1, tq, LANES), lambda b, hp, i: (b, i, hp))
    return pl.pallas_call(
        functools.partial(_attn_kernel, tq=tq, tk=tk, rb=rb),
        grid=(bsz, HEAD_PAIRS, nq),
        in_specs=[qspec,
                  pl.BlockSpec((1, LANES, s), lambda b, hp, i: (b, hp, 0)),
                  pl.BlockSpec((1, LANES, s), lambda b, hp, i: (b, 0, 0)),
                  pl.BlockSpec((1, s, LANES), lambda b, hp, i: (b, 0, hp)),
                  pl.BlockSpec((1, tq, LANES), lambda b, hp, i: (b, i, 0)),
                  qspec],
        out_specs=qspec,
        out_shape=jax.ShapeDtypeStruct((bsz, s, FOX_WIDTH), _BF16),
        scratch_shapes=[pltpu.VMEM((2, tq, 2 * LANES), _BF16),
                        pltpu.VMEM((2, tq, tk), _F32), pltpu.VMEM((2, tq, tk), _BF16),
                        pltpu.VMEM((2, tq, LANES), _F32), pltpu.VMEM((2, tq, LANES), _F32),
                        pltpu.VMEM((2, tq, LANES), _F32)],
        compiler_params=pltpu.CompilerParams(
            dimension_semantics=("arbitrary", "arbitrary", "arbitrary"), vmem_limit_bytes=VMEM_LIMIT),
        name="fox_attn",
    )(q, kt, kaugt, v, qaug, za)


def _gelu_tanh(x):
    return 0.5 * x * (1.0 + jnp.tanh(math.sqrt(2.0 / math.pi) * (x + 0.044715 * (x * x * x))))


def _s5_kernel(u_ref, zb_ref, ar_ref, ai_ref, bblk_ref, cblk_ref, dskip_ref, wglu_ref, bglu_ref,
               o_ref, xr_ref, xi_ref, sr_ref, si_ref, *, tt, nbatch, slabs_per_pass):
    t = pl.program_id(0)
    nchain = nbatch * S5_HALVES

    @pl.when(t == 0)
    def _():
        sr_ref[...] = jnp.zeros_like(sr_ref)
        si_ref[...] = jnp.zeros_like(si_ref)

    for kc in range(nchain):
        b, hf = divmod(kc, S5_HALVES)
        ub = u_ref[b, :, hf * S5_HALF_CH:(hf + 1) * S5_HALF_CH]
        bu = _dot(ub, bblk_ref[hf])
        for j in range(S5_SLABS):
            xr_ref[j, pl.ds(kc, tt, stride=nchain), :] = bu[:, j * LANES:(j + 1) * LANES]
            xi_ref[j, pl.ds(kc, tt, stride=nchain), :] = bu[:, (S5_SLABS + j) * LANES:(S5_SLABS + j + 1) * LANES]

    for j0 in range(0, S5_SLABS, slabs_per_pass):
        js = list(range(j0, j0 + slabs_per_pass))
        ars = [ar_ref[j] for j in js]
        ais = [ai_ref[j] for j in js]

        def body(s, carry, js=js, ars=ars, ais=ais):
            r0 = pl.multiple_of(s * nchain, nchain)
            new = []
            for n, j in enumerate(js):
                sr, si = carry[2 * n], carry[2 * n + 1]
                br = xr_ref[j, pl.ds(r0, nchain), :]
                bi = xi_ref[j, pl.ds(r0, nchain), :]
                nr = (ars[n] * sr - ais[n] * si) + br
                ni = (ars[n] * si + ais[n] * sr) + bi
                xr_ref[j, pl.ds(r0, nchain), :] = nr
                xi_ref[j, pl.ds(r0, nchain), :] = ni
                new += [nr, ni]
            return tuple(new)

        init = []
        for j in js:
            init += [sr_ref[j], si_ref[j]]
        fin = lax.fori_loop(0, tt, body, tuple(init), unroll=4)
        for n, j in enumerate(js):
            sr_ref[j] = fin[2 * n]
            si_ref[j] = fin[2 * n + 1]

    for b in range(nbatch):
        halves = []
        for hf in range(S5_HALVES):
            kc = b * S5_HALVES + hf
            xs = [xr_ref[j, pl.ds(kc, tt, stride=nchain), :] for j in range(S5_SLABS)]
            xs += [xi_ref[j, pl.ds(kc, tt, stride=nchain), :] for j in range(S5_SLABS)]
            xcat = jnp.concatenate(xs, axis=1).astype(_BF16)
            halves.append(_dot(xcat, cblk_ref[hf]))
        y = jnp.concatenate(halves, axis=1)
        y = y + dskip_ref[...] * u_ref[b].astype(_F32)
        y = _gelu_tanh(y)
        y = y * _sigmoid(_dot(y.astype(_BF16), wglu_ref[...]) + bglu_ref[...])
        o_ref[b] = (y * zb_ref[b].astype(_F32)).astype(_BF16)


def _s5_scan(u, zb, ar_t, ai_t, bblk, cblk, d_skip, w_glu, b_glu, tt=256, slabs_per_pass=4):
    bsz, s, w = u.shape
    nchain = bsz * S5_HALVES
    assert nchain == SUBLANES, "the scan packs batch x state-half chains into the 8 sublanes"
    tok_spec = pl.BlockSpec((bsz, tt, w), lambda t: (0, t, 0))
    const = lambda shape: pl.BlockSpec(shape, lambda t: (0,) * len(shape))
    return pl.pallas_call(
        functools.partial(_s5_kernel, tt=tt, nbatch=bsz, slabs_per_pass=slabs_per_pass),
        grid=(s // tt,),
        in_specs=[tok_spec, tok_spec, const(ar_t.shape), const(ai_t.shape),
                  const(bblk.shape), const(cblk.shape), const((1, w)), const(w_glu.shape), const((1, w))],
        out_specs=tok_spec,
        out_shape=jax.ShapeDtypeStruct((bsz, s, w), _BF16),
        scratch_shapes=[pltpu.VMEM((S5_SLABS, tt * nchain, LANES), _F32),
                        pltpu.VMEM((S5_SLABS, tt * nchain, LANES), _F32),
                        pltpu.VMEM((S5_SLABS, nchain, LANES), _F32),
                        pltpu.VMEM((S5_SLABS, nchain, LANES), _F32)],
        compiler_params=pltpu.CompilerParams(
            dimension_semantics=("arbitrary",), vmem_limit_bytes=VMEM_LIMIT),
        name="s5_scan",
    )(u, zb, ar_t, ai_t, bblk, cblk, d_skip.reshape(1, w), w_glu, b_glu.reshape(1, w))


def _out_kernel(x_ref, ya_ref, yb_ref, ga_ref, gb_ref, gate_ref, wua_ref, wub_ref, wo_ref, gf_ref, o_ref):
    ma = ga_ref[0].astype(_F32) * _dot(ya_ref[0], wua_ref[...])
    mb = gb_ref[0].astype(_F32) * _dot(yb_ref[0], wub_ref[...])
    merged = (ma + mb).astype(_BF16)
    xn = x_ref[0] + gate_ref[0] * _dot(merged, wo_ref[...])
    ms = jnp.mean(xn * xn, axis=-1, keepdims=True)
    o_ref[0] = xn * lax.rsqrt(ms + EPS) * gf_ref[...]


def _out_proj(x, ya, yb, ga, gb, gate, w_up_a, w_up_b, w_out, g_final, tm=512):
    bsz, s, d = x.shape
    tok_spec = lambda width: pl.BlockSpec((1, tm, width), lambda b, i: (b, i, 0))
    const = lambda shape: pl.BlockSpec(shape, lambda b, i: (0,) * len(shape))
    return pl.pallas_call(
        _out_kernel,
        grid=(bsz, s // tm),
        in_specs=[tok_spec(d), tok_spec(FOX_WIDTH), tok_spec(S5_WIDTH), tok_spec(d), tok_spec(d),
                  pl.BlockSpec((1, 1, d), lambda b, i: (b, 0, 0)),
                  const(w_up_a.shape), const(w_up_b.shape), const(w_out.shape), const((1, d))],
        out_specs=tok_spec(d),
        out_shape=jax.ShapeDtypeStruct((bsz, s, d), x.dtype),
        compiler_params=pltpu.CompilerParams(
            dimension_semantics=("arbitrary", "arbitrary"), vmem_limit_bytes=VMEM_LIMIT),
        name="out_proj",
    )(x, ya, yb, ga, gb, gate, w_up_a, w_up_b, w_out, g_final.reshape(1, d))


def _block_diag_in(bb):
    gh = S5_GROUPS // S5_HALVES
    bb = bb.reshape(S5_HALVES, gh, S5_GROUP, S5_STATE)
    eye = jnp.eye(gh, dtype=bb.dtype)
    out = bb[:, :, :, None, :] * eye[None, :, None, :, None]
    return out.reshape(S5_HALVES, gh * S5_GROUP, gh * S5_STATE)


def _block_diag_out(cc):
    gh = S5_GROUPS // S5_HALVES
    cc = cc.reshape(S5_HALVES, gh, S5_GROUP, S5_STATE)
    eye = jnp.eye(gh, dtype=cc.dtype)
    out = jnp.swapaxes(cc, 2, 3)[:, :, :, None, :] * eye[None, :, None, :, None]
    return out.reshape(S5_HALVES, gh * S5_STATE, gh * S5_GROUP)


def _chain_tiles(a, nbatch):
    a = a.reshape(S5_HALVES, S5_SLABS, LANES)
    a = jnp.swapaxes(a, 0, 1)
    return jnp.tile(a, (1, nbatch, 1))


def kernel(x, c, w_ada, b_ada, g_norm, w_in, b_f, a_re, a_im, log_dt, b_re, b_im, c_re, c_im,
           d_skip, w_glu, b_glu, w_up_a, w_up_b, w_out, g_final):
    bsz, s, d = x.shape
    depth = w_ada.shape[0]
    assert depth == 1 and d == D_MODEL

    mod = _adaln_mod(c, w_ada[0], b_ada[0])
    shift = mod[:, 0:d].reshape(bsz, 1, d)
    scale = mod[:, d:2 * d].reshape(bsz, 1, d)
    gate = mod[:, 2 * d:3 * d].reshape(bsz, 1, d)

    w = w_in[0]
    o_f = 3 * FOX_WIDTH
    o_z = o_f + HEADS
    o_g = o_z + FOX_WIDTH + 2 * S5_WIDTH
    wq = w[:, 0:FOX_WIDTH].astype(_BF16)
    wkt = w[:, FOX_WIDTH:2 * FOX_WIDTH].T.astype(_BF16)
    wv = w[:, 2 * FOX_WIDTH:o_f].astype(_BF16)
    wf = jnp.pad(w[:, o_f:o_z], ((0, 0), (0, LANES - HEADS))).astype(_BF16)
    wz = w[:, o_z:o_g].astype(_BF16)
    wg = w[:, o_g:].astype(_BF16)
    bf_pad = jnp.pad(b_f[0], (0, LANES - HEADS)).reshape(1, LANES)

    q, kt, v, qaug, kaugt, za, u, zb, ga, gb = _in_proj(
        x, g_norm[0], scale, shift, wq, wkt, wv, wf, wz, wg, bf_pad)

    ya = _fox_attention(q, kt, kaugt, v, qaug, za)

    abr, abi, bbr, bbi = _s5_params(a_re[0], a_im[0], log_dt[0],
                                    jnp.swapaxes(b_re[0], 1, 2), jnp.swapaxes(b_im[0], 1, 2))
    bblk = jnp.concatenate([_block_diag_in(bbr), _block_diag_in(bbi)], axis=2).astype(_BF16)
    cblk = jnp.concatenate([_block_diag_out(c_re[0]), -_block_diag_out(c_im[0])], axis=1).astype(_BF16)
    yb = _s5_scan(u, zb, _chain_tiles(abr, bsz), _chain_tiles(abi, bsz), bblk, cblk,
                  d_skip[0], w_glu[0].astype(_BF16), b_glu[0])

    return _out_proj(x, ya, yb, ga, gb, gate, w_up_a[0].astype(_BF16), w_up_b[0].astype(_BF16),
                     w_out[0].astype(_BF16), g_final)
```

```python
import functools
import math

import jax
import jax.numpy as jnp
from jax import lax
from jax.experimental import pallas as pl
from jax.experimental.pallas import tpu as pltpu

D_MODEL = 1024
EPS = 1e-6
FOX_WIDTH = 512
HEAD_DIM = 64
HEADS = 8
HEAD_PAIRS = HEADS // 2
S5_WIDTH = 512
S5_GROUP = 16
S5_GROUPS = 32
S5_STATE = 64

LANES = 128
SUBLANES = 8
VMEM_LIMIT = 56 * 1024 * 1024

S5_HALVES = 2
S5_HALF_CH = S5_WIDTH // S5_HALVES
S5_HALF_STATE = S5_GROUPS * S5_STATE // S5_HALVES
S5_SLABS = S5_HALF_STATE // LANES

NEG_BIG = -1e30
LOG2E = math.log2(math.e)

ATTN_TILE = 512
SKIP_LOG2 = 40.0
NORM_SAFETY = 1.02

_F32 = jnp.float32
_BF16 = jnp.bfloat16


def _dot(a, b):
    return jnp.dot(a, b, preferred_element_type=_F32)


def _split3(x):
    hi = x.astype(_BF16).astype(_F32)
    r1 = x - hi
    mid = r1.astype(_BF16).astype(_F32)
    lo = (r1 - mid).astype(_BF16).astype(_F32)
    return hi, mid, lo


def _sigmoid(x):
    return 1.0 / (1.0 + jnp.exp(-x))


def _silu(x):
    return x * _sigmoid(x)


def _adaln_kernel(c_ref, w_ref, b_ref, o_ref):
    o_ref[...] = jnp.dot(c_ref[...], w_ref[...], precision=lax.Precision.HIGHEST,
                         preferred_element_type=_F32) + b_ref[...]


def _adaln_mod(c, w, b):
    bsz, d = c.shape
    n = w.shape[1]
    bn = 1024
    return pl.pallas_call(
        _adaln_kernel,
        grid=(n // bn,),
        in_specs=[pl.BlockSpec((bsz, d), lambda j: (0, 0)),
                  pl.BlockSpec((d, bn), lambda j: (0, j)),
                  pl.BlockSpec((1, bn), lambda j: (0, j))],
        out_specs=pl.BlockSpec((bsz, bn), lambda j: (0, j)),
        out_shape=jax.ShapeDtypeStruct((bsz, n), _F32),
        name="adaln_mod",
    )(c, w, b.reshape(1, n))


def _s5_param_kernel(are_ref, aim_ref, ldt_ref, bre_ref, bim_ref,
                     abr_ref, abi_ref, bbr_ref, bbi_ref):
    lr = are_ref[...]
    li = aim_ref[...]
    dt = jnp.exp(ldt_ref[...])
    mag = jnp.exp(lr * dt)
    ang = li * dt
    abr = mag * jnp.cos(ang)
    abi = mag * jnp.sin(ang)
    abr_ref[...] = abr
    abi_ref[...] = abi
    nr = abr - 1.0
    ni = abi
    den = lr * lr + li * li
    cr = (nr * lr + ni * li) / den
    ci = (ni * lr - nr * li) / den
    cr3 = cr[:, None, :]
    ci3 = ci[:, None, :]
    br = bre_ref[...]
    bi = bim_ref[...]
    bbr_ref[...] = cr3 * br - ci3 * bi
    bbi_ref[...] = cr3 * bi + ci3 * br


def _s5_params(a_re, a_im, log_dt, b_re_t, b_im_t):
    g, p = a_re.shape
    cg = b_re_t.shape[1]
    return pl.pallas_call(
        _s5_param_kernel,
        out_shape=(jax.ShapeDtypeStruct((g, p), _F32), jax.ShapeDtypeStruct((g, p), _F32),
                   jax.ShapeDtypeStruct((g, cg, p), _F32), jax.ShapeDtypeStruct((g, cg, p), _F32)),
        name="s5_params",
    )(a_re, a_im, log_dt.reshape(g, 1), b_re_t, b_im_t)


def _in_proj_kernel(x_ref, gn_ref, sc_ref, sh_ref, wq_ref, wkt_ref, wv_ref, wf_ref, wz_ref, wg_ref, bf_ref,
                    q_ref, kt_ref, v_ref, qaug_ref, kaugt_ref, za_ref, u_ref, zb_ref, ga_ref, gb_ref,
                    stats_ref, carry_ref, kmax_ref, *, tm):
    i = pl.program_id(1)

    @pl.when(i == 0)
    def _():
        carry_ref[...] = jnp.zeros_like(carry_ref)
        kmax_ref[...] = jnp.zeros_like(kmax_ref)

    x = x_ref[0]
    ms = jnp.mean(x * x, axis=-1, keepdims=True)
    h = x * lax.rsqrt(ms + EPS) * gn_ref[...]
    h = (h * (1.0 + sc_ref[0]) + sh_ref[0]).astype(_BF16)

    w = FOX_WIDTH
    qb = (_dot(h, wq_ref[...]) * (HEAD_DIM ** -0.5 * LOG2E)).astype(_BF16)
    q_ref[0] = qb
    ktb = lax.dot_general(wkt_ref[...], h, (((1,), (1,)), ((), ())),
                          preferred_element_type=_F32).astype(_BF16)
    kt_ref[0] = ktb
    v_ref[0] = _dot(h, wv_ref[...]).astype(_BF16)

    seg = (lax.broadcasted_iota(jnp.int32, (w, LANES), 0) // HEAD_DIM
           == lax.broadcasted_iota(jnp.int32, (w, LANES), 1)).astype(_BF16)
    seg_t = (lax.broadcasted_iota(jnp.int32, (LANES, w), 1) // HEAD_DIM
             == lax.broadcasted_iota(jnp.int32, (LANES, w), 0)).astype(_BF16)
    qf = qb.astype(_F32)
    kf = ktb.astype(_F32)
    qn2 = _dot((qf * qf).astype(_BF16), seg)
    kn2 = _dot(seg_t, (kf * kf).astype(_BF16)).T
    kmax = jnp.maximum(kmax_ref[...], jnp.max(kn2, axis=0, keepdims=True))
    kmax_ref[...] = kmax
    za_ref[0] = _silu(_dot(h, wz_ref[:, 0:w])).astype(_BF16)
    u_ref[0] = _dot(h, wz_ref[:, w:2 * w]).astype(_BF16)
    zb_ref[0] = _silu(_dot(h, wz_ref[:, 2 * w:3 * w])).astype(_BF16)
    ga_ref[0] = _sigmoid(_dot(h, wg_ref[:, 0:D_MODEL])).astype(_BF16)
    gb_ref[0] = _sigmoid(_dot(h, wg_ref[:, D_MODEL:2 * D_MODEL])).astype(_BF16)

    fl = _dot(h, wf_ref[...]) + bf_ref[...]
    logf = -(jnp.maximum(-fl, 0.0) + jnp.log1p(jnp.exp(-jnp.abs(fl)))) * LOG2E
    hi, mid, lo = _split3(logf)
    row = lax.broadcasted_iota(jnp.int32, (tm, tm), 0)
    col = lax.broadcasted_iota(jnp.int32, (tm, tm), 1)
    tri = (col <= row).astype(_BF16)
    parts = _dot(tri, jnp.concatenate([hi, mid, lo], axis=1).astype(_BF16))
    cum = (parts[:, 0:LANES] + parts[:, LANES:2 * LANES]) + parts[:, 2 * LANES:3 * LANES]
    cum = cum + carry_ref[...]
    carry_ref[...] = cum[tm - 1:tm, :]

    nt = tm // ATTN_TILE
    for t in range(nt):
        r0 = t * ATTN_TILE
        tile = i * nt + t
        stats_ref[0, 0, pl.ds(tile, 1), :] = jnp.max(qn2[r0:r0 + ATTN_TILE], axis=0, keepdims=True)
        stats_ref[0, 1, pl.ds(tile, 1), :] = cum[r0:r0 + 1, :]
        stats_ref[0, 2, pl.ds(tile, 1), :] = cum[r0 + ATTN_TILE - 1:r0 + ATTN_TILE, :]
    stats_ref[0, 3] = jnp.broadcast_to(kmax, stats_ref.shape[2:])

    lane = lax.broadcasted_iota(jnp.int32, (1, LANES), 1)
    fq = jnp.where(lane < HEADS, cum, 0.0)
    qh, qm, ql = _split3(fq)
    qaug_ref[0] = (qh + pltpu.roll(qm, HEADS, 1) + pltpu.roll(ql, 2 * HEADS, 1)).astype(_BF16)
    kh, km, kl = _split3(cum.T[0:HEADS, :])
    ones = jnp.ones((HEADS, tm), _F32)
    zeros = jnp.zeros((HEADS, tm), _F32)
    kaugt_ref[0] = jnp.concatenate([ones, ones, ones, zeros, kh, km, kl] + [zeros] * 9, axis=0).astype(_BF16)


def _in_proj(x, g_norm, scale, shift, wq, wkt, wv, wf, wz, wg, bf_pad, tm=512):
    bsz, s, d = x.shape
    ni = s // tm
    tok = lambda width, dt: jax.ShapeDtypeStruct((bsz, s, width), dt)
    tok_spec = lambda width: pl.BlockSpec((1, tm, width), lambda b, i: (b, i, 0))
    tr = lambda rows: jax.ShapeDtypeStruct((bsz, rows, s), _BF16)
    tr_spec = lambda rows: pl.BlockSpec((1, rows, tm), lambda b, i: (b, 0, i))
    const = lambda shape: pl.BlockSpec(shape, lambda b, i: (0,) * len(shape))
    per_b = pl.BlockSpec((1, 1, d), lambda b, i: (b, 0, 0))
    return pl.pallas_call(
        functools.partial(_in_proj_kernel, tm=tm),
        grid=(bsz, ni),
        in_specs=[tok_spec(d), const((1, d)), per_b, per_b,
                  const(wq.shape), const(wkt.shape), const(wv.shape), const(wf.shape),
                  const(wz.shape), const(wg.shape), const((1, LANES))],
        out_specs=[tok_spec(FOX_WIDTH), tr_spec(FOX_WIDTH), tok_spec(FOX_WIDTH),
                   tok_spec(LANES), tr_spec(LANES),
                   tok_spec(FOX_WIDTH), tok_spec(S5_WIDTH), tok_spec(S5_WIDTH),
                   tok_spec(d), tok_spec(d),
                   pl.BlockSpec((1, 4, s // ATTN_TILE, LANES), lambda b, i: (b, 0, 0, 0))],
        out_shape=[tok(FOX_WIDTH, _BF16), tr(FOX_WIDTH), tok(FOX_WIDTH, _BF16),
                   tok(LANES, _BF16), tr(LANES),
                   tok(FOX_WIDTH, _BF16), tok(S5_WIDTH, _BF16), tok(S5_WIDTH, _BF16),
                   tok(d, _BF16), tok(d, _BF16),
                   jax.ShapeDtypeStruct((bsz, 4, s // ATTN_TILE, LANES), _F32)],
        scratch_shapes=[pltpu.VMEM((1, LANES), _F32), pltpu.VMEM((1, LANES), _F32)],
        compiler_params=pltpu.CompilerParams(
            dimension_semantics=("arbitrary", "arbitrary"), vmem_limit_bytes=VMEM_LIMIT),
        name="in_proj",
    )(x, g_norm.reshape(1, d), scale, shift, wq, wkt, wv, wf, wz, wg, bf_pad)


def _plan_kernel(st_ref, o_ref, *, nq):
    qn2 = st_ref[0, 0]
    fstart = st_ref[0, 1]
    fend = st_ref[0, 2]
    kmax2 = st_ref[0, 3]
    thr = fstart + (2.0 * NORM_SAFETY * jnp.sqrt(qn2 * kmax2) + SKIP_LOG2)
    rid = lax.broadcasted_iota(jnp.int32, (nq, LANES), 0)
    cnt = jnp.zeros((nq, LANES), _F32)
    for t in range(nq):
        c = jnp.sum(jnp.where(fend >= thr[t:t + 1, :], 1.0, 0.0), axis=0, keepdims=True)
        cnt = jnp.where(rid == t, c, cnt)
    pair = jnp.minimum(cnt, pltpu.roll(cnt, LANES - 1, 1))
    o_ref[0] = pair.astype(jnp.int32)


def _skip_plan(stats):
    bsz, _, nq, _ = stats.shape
    return pl.pallas_call(
        functools.partial(_plan_kernel, nq=nq),
        grid=(bsz,),
        in_specs=[pl.BlockSpec((1, 4, nq, LANES), lambda b: (b, 0, 0, 0))],
        out_specs=pl.BlockSpec((1, nq, LANES), lambda b: (b, 0, 0)),
        out_shape=jax.ShapeDtypeStruct((bsz, nq, LANES), jnp.int32),
        name="skip_plan",
    )(stats)


def _attn_kernel(jst_ref, q_ref, kt_ref, kaugt_ref, v_ref, qaug_ref, za_ref, o_ref,
                 lhs_ref, s_ref, p_ref, m_ref, alpha_ref, acc_ref, *, tq, tk, rb):
    b = pl.program_id(0)
    hp = pl.program_id(1)
    i = pl.program_id(2)
    j_first = jst_ref[(b * HEAD_PAIRS + hp) * pl.num_programs(2) + i]
    lane = lax.broadcasted_iota(jnp.int32, (1, LANES), 1)
    q = q_ref[0]
    qa = qaug_ref[0]

    head_lanes = [(lane >= hh * HEAD_DIM) & (lane < (hh + 1) * HEAD_DIM) for hh in range(2)]
    for hh in range(2):
        h = 2 * hp + hh
        f_lanes = (lane == h) | (lane == HEADS + h) | (lane == 2 * HEADS + h)
        neg_lanes = (lane == 4 * HEADS + h) | (lane == 5 * HEADS + h) | (lane == 6 * HEADS + h)
        qm = jnp.where(head_lanes[hh], q, jnp.zeros_like(q))
        qaug = jnp.where(f_lanes, qa, jnp.where(neg_lanes, -1.0, 0.0).astype(_BF16))
        lhs_ref[hh] = jnp.concatenate([qm, qaug], axis=1)
        m_ref[hh] = jnp.full((tq, LANES), NEG_BIG, _F32)
        acc_ref[hh] = jnp.zeros((tq, LANES), _F32)

    def step(j, masked):
        k0 = pl.multiple_of(j * tk, tk)
        w = jnp.concatenate([kt_ref[0, :, pl.ds(k0, tk)], kaugt_ref[0, :, pl.ds(k0, tk)]], axis=0)
        vb = v_ref[0, pl.ds(k0, tk), :]
        for hh in range(2):
            s = _dot(lhs_ref[hh], w)
            if masked:
                r = lax.broadcasted_iota(jnp.int32, (tq, tk), 0)
                c = lax.broadcasted_iota(jnp.int32, (tq, tk), 1)
                s = jnp.where(c <= r, s, NEG_BIG)
            s_ref[hh] = s
        for hh in range(2):
            for r0 in range(0, tq, rb):
                sr = s_ref[hh, r0:r0 + rb, :]
                m_old = m_ref[hh, r0:r0 + rb, 0:1]
                m_new = jnp.maximum(m_old, jnp.max(sr, axis=1, keepdims=True))
                alpha_ref[hh, r0:r0 + rb, :] = jnp.broadcast_to(jnp.exp2(m_old - m_new), (rb, LANES))
                m_ref[hh, r0:r0 + rb, :] = jnp.broadcast_to(m_new, (rb, LANES))
                p_ref[hh, r0:r0 + rb, :] = jnp.exp2(sr - m_new).astype(_BF16)
        for hh in range(2):
            sum_lane = (1 - hh) * HEAD_DIM
            vaug = jnp.where(head_lanes[hh], vb, jnp.where(lane == sum_lane, 1.0, 0.0).astype(_BF16))
            acc_ref[hh] = alpha_ref[hh] * acc_ref[hh] + _dot(p_ref[hh], vaug)

    def full_step(j, carry):
        step(j, False)
        return carry

    lax.fori_loop(j_first, i, full_step, 0)
    step(i, True)

    a0 = acc_ref[0]
    a1 = acc_ref[1]
    l0 = a0[:, HEAD_DIM:HEAD_DIM + 1]
    l1 = a1[:, 0:1]
    y = jnp.where(lane < HEAD_DIM, a0 / l0, a1 / l1)
    o_ref[0] = (y * za_ref[0].astype(_F32)).astype(_BF16)


def _fox_attention(j_first, q, kt, kaugt, v, qaug, za, rb=32):
    bsz, s, _ = q.shape
    tq = tk = ATTN_TILE
    nq = s // tq
    qspec = pl.BlockSpec((1, tq, LANES), lambda b, hp, i, jst: (b, i, hp))
    grid_spec = pltpu.PrefetchScalarGridSpec(
        num_scalar_prefetch=1,
        grid=(bsz, HEAD_PAIRS, nq),
        in_specs=[qspec,
                  pl.BlockSpec((1, LANES, s), lambda b, hp, i, jst: (b, hp, 0)),
                  pl.BlockSpec((1, LANES, s), lambda b, hp, i, jst: (b, 0, 0)),
                  pl.BlockSpec((1, s, LANES), lambda b, hp, i, jst: (b, 0, hp)),
                  pl.BlockSpec((1, tq, LANES), lambda b, hp, i, jst: (b, i, 0)),
                  qspec],
        out_specs=qspec,
        scratch_shapes=[pltpu.VMEM((2, tq, 2 * LANES), _BF16),
                        pltpu.VMEM((2, tq, tk), _F32), pltpu.VMEM((2, tq, tk), _BF16),
                        pltpu.VMEM((2, tq, LANES), _F32), pltpu.VMEM((2, tq, LANES), _F32),
                        pltpu.VMEM((2, tq, LANES), _F32)])
    return pl.pallas_call(
        functools.partial(_attn_kernel, tq=tq, tk=tk, rb=rb),
        grid_spec=grid_spec,
        out_shape=jax.ShapeDtypeStruct((bsz, s, FOX_WIDTH), _BF16),
        compiler_params=pltpu.CompilerParams(
            dimension_semantics=("arbitrary", "arbitrary", "arbitrary"), vmem_limit_bytes=VMEM_LIMIT),
        name="fox_attn",
    )(j_first, q, kt, kaugt, v, qaug, za)


def _gelu_tanh(x):
    return 0.5 * x * (1.0 + jnp.tanh(math.sqrt(2.0 / math.pi) * (x + 0.044715 * (x * x * x))))


def _s5_kernel(u_ref, zb_ref, ar_ref, ai_ref, bblk_ref, cblk_ref, dskip_ref, wglu_ref, bglu_ref,
               o_ref, xr_ref, xi_ref, sr_ref, si_ref, *, tt, nbatch, slabs_per_pass):
    t = pl.program_id(0)
    nchain = nbatch * S5_HALVES

    @pl.when(t == 0)
    def _():
        sr_ref[...] = jnp.zeros_like(sr_ref)
        si_ref[...] = jnp.zeros_like(si_ref)

    for kc in range(nchain):
        b, hf = divmod(kc, S5_HALVES)
        ub = u_ref[b, :, hf * S5_HALF_CH:(hf + 1) * S5_HALF_CH]
        bu = _dot(ub, bblk_ref[hf])
        for j in range(S5_SLABS):
            xr_ref[j, pl.ds(kc, tt, stride=nchain), :] = bu[:, j * LANES:(j + 1) * LANES]
            xi_ref[j, pl.ds(kc, tt, stride=nchain), :] = bu[:, (S5_SLABS + j) * LANES:(S5_SLABS + j + 1) * LANES]

    for j0 in range(0, S5_SLABS, slabs_per_pass):
        js = list(range(j0, j0 + slabs_per_pass))
        ars = [ar_ref[j] for j in js]
        ais = [ai_ref[j] for j in js]

        def body(s, carry, js=js, ars=ars, ais=ais):
            r0 = pl.multiple_of(s * nchain, nchain)
            new = []
            for n, j in enumerate(js):
                sr, si = carry[2 * n], carry[2 * n + 1]
                br = xr_ref[j, pl.ds(r0, nchain), :]
                bi = xi_ref[j, pl.ds(r0, nchain), :]
                nr = (ars[n] * sr - ais[n] * si) + br
                ni = (ars[n] * si + ais[n] * sr) + bi
                xr_ref[j, pl.ds(r0, nchain), :] = nr
                xi_ref[j, pl.ds(r0, nchain), :] = ni
                new += [nr, ni]
            return tuple(new)

        init = []
        for j in js:
            init += [sr_ref[j], si_ref[j]]
        fin = lax.fori_loop(0, tt, body, tuple(init), unroll=4)
        for n, j in enumerate(js):
            sr_ref[j] = fin[2 * n]
            si_ref[j] = fin[2 * n + 1]

    for b in range(nbatch):
        halves = []
        for hf in range(S5_HALVES):
            kc = b * S5_HALVES + hf
            xs = [xr_ref[j, pl.ds(kc, tt, stride=nchain), :] for j in range(S5_SLABS)]
            xs += [xi_ref[j, pl.ds(kc, tt, stride=nchain), :] for j in range(S5_SLABS)]
            xcat = jnp.concatenate(xs, axis=1).astype(_BF16)
            halves.append(_dot(xcat, cblk_ref[hf]))
        y = jnp.concatenate(halves, axis=1)
        y = y + dskip_ref[...] * u_ref[b].astype(_F32)
        y = _gelu_tanh(y)
        y = y * _sigmoid(_dot(y.astype(_BF16), wglu_ref[...]) + bglu_ref[...])
        o_ref[b] = (y * zb_ref[b].astype(_F32)).astype(_BF16)


def _s5_scan(u, zb, ar_t, ai_t, bblk, cblk, d_skip, w_glu, b_glu, tt=256, slabs_per_pass=4):
    bsz, s, w = u.shape
    nchain = bsz * S5_HALVES
    assert nchain == SUBLANES, "the scan packs batch x state-half chains into the 8 sublanes"
    tok_spec = pl.BlockSpec((bsz, tt, w), lambda t: (0, t, 0))
    const = lambda shape: pl.BlockSpec(shape, lambda t: (0,) * len(shape))
    return pl.pallas_call(
        functools.partial(_s5_kernel, tt=tt, nbatch=bsz, slabs_per_pass=slabs_per_pass),
        grid=(s // tt,),
        in_specs=[tok_spec, tok_spec, const(ar_t.shape), const(ai_t.shape),
                  const(bblk.shape), const(cblk.shape), const((1, w)), const(w_glu.shape), const((1, w))],
        out_specs=tok_spec,
        out_shape=jax.ShapeDtypeStruct((bsz, s, w), _BF16),
        scratch_shapes=[pltpu.VMEM((S5_SLABS, tt * nchain, LANES), _F32),
                        pltpu.VMEM((S5_SLABS, tt * nchain, LANES), _F32),
                        pltpu.VMEM((S5_SLABS, nchain, LANES), _F32),
                        pltpu.VMEM((S5_SLABS, nchain, LANES), _F32)],
        compiler_params=pltpu.CompilerParams(
            dimension_semantics=("arbitrary",), vmem_limit_bytes=VMEM_LIMIT),
        name="s5_scan",
    )(u, zb, ar_t, ai_t, bblk, cblk, d_skip.reshape(1, w), w_glu, b_glu.reshape(1, w))


def _out_kernel(x_ref, ya_ref, yb_ref, ga_ref, gb_ref, gate_ref, wua_ref, wub_ref, wo_ref, gf_ref, o_ref):
    ma = ga_ref[0].astype(_F32) * _dot(ya_ref[0], wua_ref[...])
    mb = gb_ref[0].astype(_F32) * _dot(yb_ref[0], wub_ref[...])
    merged = (ma + mb).astype(_BF16)
    xn = x_ref[0] + gate_ref[0] * _dot(merged, wo_ref[...])
    ms = jnp.mean(xn * xn, axis=-1, keepdims=True)
    o_ref[0] = xn * lax.rsqrt(ms + EPS) * gf_ref[...]


def _out_proj(x, ya, yb, ga, gb, gate, w_up_a, w_up_b, w_out, g_final, tm=512):
    bsz, s, d = x.shape
    tok_spec = lambda width: pl.BlockSpec((1, tm, width), lambda b, i: (b, i, 0))
    const = lambda shape: pl.BlockSpec(shape, lambda b, i: (0,) * len(shape))
    return pl.pallas_call(
        _out_kernel,
        grid=(bsz, s // tm),
        in_specs=[tok_spec(d), tok_spec(FOX_WIDTH), tok_spec(S5_WIDTH), tok_spec(d), tok_spec(d),
                  pl.BlockSpec((1, 1, d), lambda b, i: (b, 0, 0)),
                  const(w_up_a.shape), const(w_up_b.shape), const(w_out.shape), const((1, d))],
        out_specs=tok_spec(d),
        out_shape=jax.ShapeDtypeStruct((bsz, s, d), x.dtype),
        compiler_params=pltpu.CompilerParams(
            dimension_semantics=("arbitrary", "arbitrary"), vmem_limit_bytes=VMEM_LIMIT),
        name="out_proj",
    )(x, ya, yb, ga, gb, gate, w_up_a, w_up_b, w_out, g_final.reshape(1, d))


def _block_diag_in(bb):
    gh = S5_GROUPS // S5_HALVES
    bb = bb.reshape(S5_HALVES, gh, S5_GROUP, S5_STATE)
    eye = jnp.eye(gh, dtype=bb.dtype)
    out = bb[:, :, :, None, :] * eye[None, :, None, :, None]
    return out.reshape(S5_HALVES, gh * S5_GROUP, gh * S5_STATE)


def _block_diag_out(cc):
    gh = S5_GROUPS // S5_HALVES
    cc = cc.reshape(S5_HALVES, gh, S5_GROUP, S5_STATE)
    eye = jnp.eye(gh, dtype=cc.dtype)
    out = jnp.swapaxes(cc, 2, 3)[:, :, :, None, :] * eye[None, :, None, :, None]
    return out.reshape(S5_HALVES, gh * S5_STATE, gh * S5_GROUP)


def _chain_tiles(a, nbatch):
    a = a.reshape(S5_HALVES, S5_SLABS, LANES)
    a = jnp.swapaxes(a, 0, 1)
    return jnp.tile(a, (1, nbatch, 1))


def kernel(x, c, w_ada, b_ada, g_norm, w_in, b_f, a_re, a_im, log_dt, b_re, b_im, c_re, c_im,
           d_skip, w_glu, b_glu, w_up_a, w_up_b, w_out, g_final):
    bsz, s, d = x.shape
    depth = w_ada.shape[0]
    assert depth == 1 and d == D_MODEL

    mod = _adaln_mod(c, w_ada[0], b_ada[0])
    shift = mod[:, 0:d].reshape(bsz, 1, d)
    scale = mod[:, d:2 * d].reshape(bsz, 1, d)
    gate = mod[:, 2 * d:3 * d].reshape(bsz, 1, d)

    w = w_in[0]
    o_f = 3 * FOX_WIDTH
    o_z = o_f + HEADS
    o_g = o_z + FOX_WIDTH + 2 * S5_WIDTH
    wq = w[:, 0:FOX_WIDTH].astype(_BF16)
    wkt = w[:, FOX_WIDTH:2 * FOX_WIDTH].T.astype(_BF16)
    wv = w[:, 2 * FOX_WIDTH:o_f].astype(_BF16)
    wf = jnp.pad(w[:, o_f:o_z], ((0, 0), (0, LANES - HEADS))).astype(_BF16)
    wz = w[:, o_z:o_g].astype(_BF16)
    wg = w[:, o_g:].astype(_BF16)
    bf_pad = jnp.pad(b_f[0], (0, LANES - HEADS)).reshape(1, LANES)

    q, kt, v, qaug, kaugt, za, u, zb, ga, gb, stats = _in_proj(
        x, g_norm[0], scale, shift, wq, wkt, wv, wf, wz, wg, bf_pad)

    plan = _skip_plan(stats)
    j_first = jnp.swapaxes(plan[:, :, 0:HEADS:2], 1, 2).reshape(-1)
    ya = _fox_attention(j_first, q, kt, kaugt, v, qaug, za)

    abr, abi, bbr, bbi = _s5_params(a_re[0], a_im[0], log_dt[0],
                                    jnp.swapaxes(b_re[0], 1, 2), jnp.swapaxes(b_im[0], 1, 2))
    bblk = jnp.concatenate([_block_diag_in(bbr), _block_diag_in(bbi)], axis=2).astype(_BF16)
    cblk = jnp.concatenate([_block_diag_out(c_re[0]), -_block_diag_out(c_im[0])], axis=1).astype(_BF16)
    yb = _s5_scan(u, zb, _chain_tiles(abr, bsz), _chain_tiles(abi, bsz), bblk, cblk,
                  d_skip[0], w_glu[0].astype(_BF16), b_glu[0])

    return _out_proj(x, ya, yb, ga, gb, gate, w_up_a[0].astype(_BF16), w_up_b[0].astype(_BF16),
                     w_out[0].astype(_BF16), g_final)
```

```python
import functools
import math

import jax
import jax.numpy as jnp
from jax import lax
from jax.experimental import pallas as pl
from jax.experimental.pallas import tpu as pltpu

D_MODEL = 1024
EPS = 1e-6
FOX_WIDTH = 512
HEAD_DIM = 64
HEADS = 8
HEAD_PAIRS = HEADS // 2
S5_WIDTH = 512
S5_GROUP = 16
S5_GROUPS = 32
S5_STATE = 64

LANES = 128
SUBLANES = 8
VMEM_LIMIT = 56 * 1024 * 1024

S5_HALVES = 2
S5_HALF_CH = S5_WIDTH // S5_HALVES
S5_HALF_STATE = S5_GROUPS * S5_STATE // S5_HALVES
S5_SLABS = S5_HALF_STATE // LANES

NEG_BIG = -1e30
LOG2E = math.log2(math.e)

ATTN_TILE = 512
SKIP_LOG2 = 40.0
NORM_SAFETY = 1.02

_F32 = jnp.float32
_BF16 = jnp.bfloat16


def _dot(a, b):
    return jnp.dot(a, b, preferred_element_type=_F32)


def _split3(x):
    hi = x.astype(_BF16).astype(_F32)
    r1 = x - hi
    mid = r1.astype(_BF16).astype(_F32)
    lo = (r1 - mid).astype(_BF16).astype(_F32)
    return hi, mid, lo


def _sigmoid(x):
    return 1.0 / (1.0 + jnp.exp(-x))


def _silu(x):
    return x * _sigmoid(x)


def _adaln_kernel(c_ref, w_ref, b_ref, o_ref):
    o_ref[...] = jnp.dot(c_ref[...], w_ref[...], precision=lax.Precision.HIGHEST,
                         preferred_element_type=_F32) + b_ref[...]


def _adaln_mod(c, w, b):
    bsz, d = c.shape
    n = w.shape[1]
    bn = 1024
    return pl.pallas_call(
        _adaln_kernel,
        grid=(n // bn,),
        in_specs=[pl.BlockSpec((bsz, d), lambda j: (0, 0)),
                  pl.BlockSpec((d, bn), lambda j: (0, j)),
                  pl.BlockSpec((1, bn), lambda j: (0, j))],
        out_specs=pl.BlockSpec((bsz, bn), lambda j: (0, j)),
        out_shape=jax.ShapeDtypeStruct((bsz, n), _F32),
        name="adaln_mod",
    )(c, w, b.reshape(1, n))


def _s5_param_kernel(are_ref, aim_ref, ldt_ref, bre_ref, bim_ref,
                     abr_ref, abi_ref, bbr_ref, bbi_ref):
    lr = are_ref[...]
    li = aim_ref[...]
    dt = jnp.exp(ldt_ref[...])
    mag = jnp.exp(lr * dt)
    ang = li * dt
    abr = mag * jnp.cos(ang)
    abi = mag * jnp.sin(ang)
    abr_ref[...] = abr
    abi_ref[...] = abi
    nr = abr - 1.0
    ni = abi
    den = lr * lr + li * li
    cr = (nr * lr + ni * li) / den
    ci = (ni * lr - nr * li) / den
    cr3 = cr[:, None, :]
    ci3 = ci[:, None, :]
    br = bre_ref[...]
    bi = bim_ref[...]
    bbr_ref[...] = cr3 * br - ci3 * bi
    bbi_ref[...] = cr3 * bi + ci3 * br


def _s5_params(a_re, a_im, log_dt, b_re_t, b_im_t):
    g, p = a_re.shape
    cg = b_re_t.shape[1]
    return pl.pallas_call(
        _s5_param_kernel,
        out_shape=(jax.ShapeDtypeStruct((g, p), _F32), jax.ShapeDtypeStruct((g, p), _F32),
                   jax.ShapeDtypeStruct((g, cg, p), _F32), jax.ShapeDtypeStruct((g, cg, p), _F32)),
        name="s5_params",
    )(a_re, a_im, log_dt.reshape(g, 1), b_re_t, b_im_t)


def _in_proj_kernel(x_ref, gn_ref, sc_ref, sh_ref, wq_ref, wkt_ref, wv_ref, wf_ref, wz_ref, wg_ref, bf_ref,
                    q_ref, kt_ref, v_ref, qaug_ref, kaugt_ref, za_ref, u_ref, zb_ref, ga_ref, gb_ref,
                    stats_ref, carry_ref, kmax_ref, *, tm):
    i = pl.program_id(1)

    @pl.when(i == 0)
    def _():
        carry_ref[...] = jnp.zeros_like(carry_ref)
        kmax_ref[...] = jnp.zeros_like(kmax_ref)

    x = x_ref[0]
    ms = jnp.mean(x * x, axis=-1, keepdims=True)
    h = x * lax.rsqrt(ms + EPS) * gn_ref[...]
    h = (h * (1.0 + sc_ref[0]) + sh_ref[0]).astype(_BF16)

    w = FOX_WIDTH
    qb = (_dot(h, wq_ref[...]) * (HEAD_DIM ** -0.5 * LOG2E)).astype(_BF16)
    q_ref[0] = qb
    ktb = lax.dot_general(wkt_ref[...], h, (((1,), (1,)), ((), ())),
                          preferred_element_type=_F32).astype(_BF16)
    kt_ref[0] = ktb
    v_ref[0] = _dot(h, wv_ref[...]).astype(_BF16)

    seg = (lax.broadcasted_iota(jnp.int32, (w, LANES), 0) // HEAD_DIM
           == lax.broadcasted_iota(jnp.int32, (w, LANES), 1)).astype(_BF16)
    seg_t = (lax.broadcasted_iota(jnp.int32, (LANES, w), 1) // HEAD_DIM
             == lax.broadcasted_iota(jnp.int32, (LANES, w), 0)).astype(_BF16)
    qf = qb.astype(_F32)
    kf = ktb.astype(_F32)
    qn2 = _dot((qf * qf).astype(_BF16), seg)
    kn2 = _dot(seg_t, (kf * kf).astype(_BF16)).T
    kmax = jnp.maximum(kmax_ref[...], jnp.max(kn2, axis=0, keepdims=True))
    kmax_ref[...] = kmax
    za_ref[0] = _silu(_dot(h, wz_ref[:, 0:w])).astype(_BF16)
    u_ref[0] = _dot(h, wz_ref[:, w:2 * w]).astype(_BF16)
    zb_ref[0] = _silu(_dot(h, wz_ref[:, 2 * w:3 * w])).astype(_BF16)
    ga_ref[0] = _sigmoid(_dot(h, wg_ref[:, 0:D_MODEL])).astype(_BF16)
    gb_ref[0] = _sigmoid(_dot(h, wg_ref[:, D_MODEL:2 * D_MODEL])).astype(_BF16)

    fl = _dot(h, wf_ref[...]) + bf_ref[...]
    logf = -(jnp.maximum(-fl, 0.0) + jnp.log1p(jnp.exp(-jnp.abs(fl)))) * LOG2E
    hi, mid, lo = _split3(logf)
    lmh = jnp.concatenate([hi, mid, lo], axis=1).astype(_BF16)

    ta = ATTN_TILE
    row = lax.broadcasted_iota(jnp.int32, (ta, ta), 0)
    col = lax.broadcasted_iota(jnp.int32, (ta, ta), 1)
    tri = (col <= row).astype(_BF16)
    lane = lax.broadcasted_iota(jnp.int32, (1, LANES), 1)
    ones = jnp.ones((HEADS, ta), _F32)
    zeros = jnp.zeros((HEADS, ta), _F32)
    nt = tm // ta
    carry = carry_ref[...]
    for t in range(nt):
        r0 = t * ta
        parts = _dot(tri, lmh[r0:r0 + ta])
        cum = (parts[:, 0:LANES] + parts[:, LANES:2 * LANES]) + parts[:, 2 * LANES:3 * LANES] + carry
        carry = cum[ta - 1:ta, :]

        tile = i * nt + t
        stats_ref[0, 0, pl.ds(tile, 1), :] = jnp.max(qn2[r0:r0 + ta], axis=0, keepdims=True)
        stats_ref[0, 1, pl.ds(tile, 1), :] = cum[0:1, :]
        stats_ref[0, 2, pl.ds(tile, 1), :] = carry

        qh, qm, ql = _split3(jnp.where(lane < HEADS, cum, 0.0))
        qaug_ref[0, r0:r0 + ta, :] = (qh + pltpu.roll(qm, HEADS, 1)
                                      + pltpu.roll(ql, 2 * HEADS, 1)).astype(_BF16)
        kh, km, kl = _split3(cum.T[0:HEADS, :])
        kaugt_ref[0, :, r0:r0 + ta] = jnp.concatenate(
            [ones, ones, ones, zeros, kh, km, kl] + [zeros] * 9, axis=0).astype(_BF16)
    carry_ref[...] = carry
    stats_ref[0, 3] = jnp.broadcast_to(kmax, stats_ref.shape[2:])


def _in_proj(x, g_norm, scale, shift, wq, wkt, wv, wf, wz, wg, bf_pad, tm=1024):
    bsz, s, d = x.shape
    ni = s // tm
    tok = lambda width, dt: jax.ShapeDtypeStruct((bsz, s, width), dt)
    tok_spec = lambda width: pl.BlockSpec((1, tm, width), lambda b, i: (b, i, 0))
    tr = lambda rows: jax.ShapeDtypeStruct((bsz, rows, s), _BF16)
    tr_spec = lambda rows: pl.BlockSpec((1, rows, tm), lambda b, i: (b, 0, i))
    const = lambda shape: pl.BlockSpec(shape, lambda b, i: (0,) * len(shape), pipeline_mode=pl.Buffered(1))
    per_b = pl.BlockSpec((1, 1, d), lambda b, i: (b, 0, 0))
    return pl.pallas_call(
        functools.partial(_in_proj_kernel, tm=tm),
        grid=(bsz, ni),
        in_specs=[tok_spec(d), const((1, d)), per_b, per_b,
                  const(wq.shape), const(wkt.shape), const(wv.shape), const(wf.shape),
                  const(wz.shape), const(wg.shape), const((1, LANES))],
        out_specs=[tok_spec(FOX_WIDTH), tr_spec(FOX_WIDTH), tok_spec(FOX_WIDTH),
                   tok_spec(LANES), tr_spec(LANES),
                   tok_spec(FOX_WIDTH), tok_spec(S5_WIDTH), tok_spec(S5_WIDTH),
                   tok_spec(d), tok_spec(d),
                   pl.BlockSpec((1, 4, s // ATTN_TILE, LANES), lambda b, i: (b, 0, 0, 0))],
        out_shape=[tok(FOX_WIDTH, _BF16), tr(FOX_WIDTH), tok(FOX_WIDTH, _BF16),
                   tok(LANES, _BF16), tr(LANES),
                   tok(FOX_WIDTH, _BF16), tok(S5_WIDTH, _BF16), tok(S5_WIDTH, _BF16),
                   tok(d, _BF16), tok(d, _BF16),
                   jax.ShapeDtypeStruct((bsz, 4, s // ATTN_TILE, LANES), _F32)],
        scratch_shapes=[pltpu.VMEM((1, LANES), _F32), pltpu.VMEM((1, LANES), _F32)],
        compiler_params=pltpu.CompilerParams(
            dimension_semantics=("arbitrary", "arbitrary"), vmem_limit_bytes=VMEM_LIMIT),
        name="in_proj",
    )(x, g_norm.reshape(1, d), scale, shift, wq, wkt, wv, wf, wz, wg, bf_pad)


def _plan_kernel(st_ref, o_ref, *, nq):
    qn2 = st_ref[0, 0]
    fstart = st_ref[0, 1]
    fend = st_ref[0, 2]
    kmax2 = st_ref[0, 3]
    thr = fstart + (2.0 * NORM_SAFETY * jnp.sqrt(qn2 * kmax2) + SKIP_LOG2)
    rid = lax.broadcasted_iota(jnp.int32, (nq, LANES), 0)
    cnt = jnp.zeros((nq, LANES), _F32)
    for t in range(nq):
        c = jnp.sum(jnp.where(fend >= thr[t:t + 1, :], 1.0, 0.0), axis=0, keepdims=True)
        cnt = jnp.where(rid == t, c, cnt)
    pair = jnp.minimum(cnt, pltpu.roll(cnt, LANES - 1, 1))
    o_ref[0] = pair.astype(jnp.int32)


def _skip_plan(stats):
    bsz, _, nq, _ = stats.shape
    return pl.pallas_call(
        functools.partial(_plan_kernel, nq=nq),
        grid=(bsz,),
        in_specs=[pl.BlockSpec((1, 4, nq, LANES), lambda b: (b, 0, 0, 0))],
        out_specs=pl.BlockSpec((1, nq, LANES), lambda b: (b, 0, 0)),
        out_shape=jax.ShapeDtypeStruct((bsz, nq, LANES), jnp.int32),
        name="skip_plan",
    )(stats)


def _attn_kernel(jst_ref, q_ref, kt_ref, kaugt_ref, v_ref, qaug_ref, za_ref, o_ref,
                 lhs_ref, s_ref, p_ref, m_ref, alpha_ref, acc_ref, *, tq, tk, rb):
    b = pl.program_id(0)
    hp = pl.program_id(1)
    i = pl.program_id(2)
    j_first = jst_ref[(b * HEAD_PAIRS + hp) * pl.num_programs(2) + i]
    lane = lax.broadcasted_iota(jnp.int32, (1, LANES), 1)
    q = q_ref[0]
    qa = qaug_ref[0]

    head_lanes = [(lane >= hh * HEAD_DIM) & (lane < (hh + 1) * HEAD_DIM) for hh in range(2)]
    for hh in range(2):
        h = 2 * hp + hh
        f_lanes = (lane == h) | (lane == HEADS + h) | (lane == 2 * HEADS + h)
        neg_lanes = (lane == 4 * HEADS + h) | (lane == 5 * HEADS + h) | (lane == 6 * HEADS + h)
        qm = jnp.where(head_lanes[hh], q, jnp.zeros_like(q))
        qaug = jnp.where(f_lanes, qa, jnp.where(neg_lanes, -1.0, 0.0).astype(_BF16))
        lhs_ref[hh] = jnp.concatenate([qm, qaug], axis=1)
        m_ref[hh] = jnp.full((tq, LANES), NEG_BIG, _F32)
        acc_ref[hh] = jnp.zeros((tq, LANES), _F32)

    def step(j, masked):
        k0 = pl.multiple_of(j * tk, tk)
        w = jnp.concatenate([kt_ref[0, :, pl.ds(k0, tk)], kaugt_ref[0, :, pl.ds(k0, tk)]], axis=0)
        vb = v_ref[0, pl.ds(k0, tk), :]
        for hh in range(2):
            s = _dot(lhs_ref[hh], w)
            if masked:
                r = lax.broadcasted_iota(jnp.int32, (tq, tk), 0)
                c = lax.broadcasted_iota(jnp.int32, (tq, tk), 1)
                s = jnp.where(c <= r, s, NEG_BIG)
            s_ref[hh] = s
        for hh in range(2):
            for r0 in range(0, tq, rb):
                chunks = [s_ref[hh, r0:r0 + rb, c0:c0 + LANES] for c0 in range(0, tk, LANES)]
                m_old = m_ref[hh, r0:r0 + rb, :]
                m_new = jnp.maximum(m_old, jnp.max(functools.reduce(jnp.maximum, chunks), axis=1, keepdims=True))
                alpha_ref[hh, r0:r0 + rb, :] = jnp.exp2(m_old - m_new)
                m_ref[hh, r0:r0 + rb, :] = m_new
                for c0, ch in zip(range(0, tk, LANES), chunks):
                    p_ref[hh, r0:r0 + rb, c0:c0 + LANES] = jnp.exp2(ch - m_new).astype(_BF16)
        for hh in range(2):
            sum_lane = (1 - hh) * HEAD_DIM
            vaug = jnp.where(head_lanes[hh], vb, jnp.where(lane == sum_lane, 1.0, 0.0).astype(_BF16))
            acc_ref[hh] = alpha_ref[hh] * acc_ref[hh] + _dot(p_ref[hh], vaug)

    def full_step(j, carry):
        step(j, False)
        return carry

    lax.fori_loop(j_first, i, full_step, 0)
    step(i, True)

    a0 = acc_ref[0]
    a1 = acc_ref[1]
    l0 = a0[:, HEAD_DIM:HEAD_DIM + 1]
    l1 = a1[:, 0:1]
    y = jnp.where(lane < HEAD_DIM, a0 / l0, a1 / l1)
    o_ref[0] = (y * za_ref[0].astype(_F32)).astype(_BF16)


def _fox_attention(j_first, q, kt, kaugt, v, qaug, za, rb=32):
    bsz, s, _ = q.shape
    tq = tk = ATTN_TILE
    nq = s // tq
    qspec = pl.BlockSpec((1, tq, LANES), lambda b, hp, i, jst: (b, i, hp))
    grid_spec = pltpu.PrefetchScalarGridSpec(
        num_scalar_prefetch=1,
        grid=(bsz, HEAD_PAIRS, nq),
        in_specs=[qspec,
                  pl.BlockSpec((1, LANES, s), lambda b, hp, i, jst: (b, hp, 0)),
                  pl.BlockSpec((1, LANES, s), lambda b, hp, i, jst: (b, 0, 0)),
                  pl.BlockSpec((1, s, LANES), lambda b, hp, i, jst: (b, 0, hp)),
                  pl.BlockSpec((1, tq, LANES), lambda b, hp, i, jst: (b, i, 0)),
                  qspec],
        out_specs=qspec,
        scratch_shapes=[pltpu.VMEM((2, tq, 2 * LANES), _BF16),
                        pltpu.VMEM((2, tq, tk), _F32), pltpu.VMEM((2, tq, tk), _BF16),
                        pltpu.VMEM((2, tq, LANES), _F32), pltpu.VMEM((2, tq, LANES), _F32),
                        pltpu.VMEM((2, tq, LANES), _F32)])
    return pl.pallas_call(
        functools.partial(_attn_kernel, tq=tq, tk=tk, rb=rb),
        grid_spec=grid_spec,
        out_shape=jax.ShapeDtypeStruct((bsz, s, FOX_WIDTH), _BF16),
        compiler_params=pltpu.CompilerParams(
            dimension_semantics=("arbitrary", "arbitrary", "arbitrary"), vmem_limit_bytes=VMEM_LIMIT),
        name="fox_attn",
    )(j_first, q, kt, kaugt, v, qaug, za)


def _gelu_tanh(x):
    return 0.5 * x * (1.0 + jnp.tanh(math.sqrt(2.0 / math.pi) * (x + 0.044715 * (x * x * x))))


def _s5_kernel(u_ref, zb_ref, ar_ref, ai_ref, bblk_ref, cblk_ref, dskip_ref, wglu_ref, bglu_ref,
               o_ref, xr_ref, xi_ref, sr_ref, si_ref, *, tt, nbatch, slabs_per_pass):
    t = pl.program_id(0)
    nchain = nbatch * S5_HALVES

    @pl.when(t == 0)
    def _():
        sr_ref[...] = jnp.zeros_like(sr_ref)
        si_ref[...] = jnp.zeros_like(si_ref)

    for hf in range(S5_HALVES):
        ub = jnp.concatenate([u_ref[b, :, hf * S5_HALF_CH:(hf + 1) * S5_HALF_CH] for b in range(nbatch)],
                             axis=0)
        bu = _dot(ub, bblk_ref[hf])
        for b in range(nbatch):
            kc = b * S5_HALVES + hf
            rows = slice(b * tt, (b + 1) * tt)
            for j in range(S5_SLABS):
                xr_ref[j, pl.ds(kc, tt, stride=nchain), :] = bu[rows, j * LANES:(j + 1) * LANES]
                xi_ref[j, pl.ds(kc, tt, stride=nchain), :] = bu[rows, (S5_SLABS + j) * LANES:
                                                                (S5_SLABS + j + 1) * LANES]

    for j0 in range(0, S5_SLABS, slabs_per_pass):
        js = list(range(j0, j0 + slabs_per_pass))
        ars = [ar_ref[j] for j in js]
        ais = [ai_ref[j] for j in js]

        def body(s, carry, js=js, ars=ars, ais=ais):
            r0 = pl.multiple_of(s * nchain, nchain)
            new = []
            for n, j in enumerate(js):
                sr, si = carry[2 * n], carry[2 * n + 1]
                br = xr_ref[j, pl.ds(r0, nchain), :]
                bi = xi_ref[j, pl.ds(r0, nchain), :]
                nr = (ars[n] * sr - ais[n] * si) + br
                ni = (ars[n] * si + ais[n] * sr) + bi
                xr_ref[j, pl.ds(r0, nchain), :] = nr
                xi_ref[j, pl.ds(r0, nchain), :] = ni
                new += [nr, ni]
            return tuple(new)

        init = []
        for j in js:
            init += [sr_ref[j], si_ref[j]]
        fin = lax.fori_loop(0, tt, body, tuple(init), unroll=4)
        for n, j in enumerate(js):
            sr_ref[j] = fin[2 * n]
            si_ref[j] = fin[2 * n + 1]

    halves = []
    for hf in range(S5_HALVES):
        xall = []
        for b in range(nbatch):
            kc = b * S5_HALVES + hf
            xs = [xr_ref[j, pl.ds(kc, tt, stride=nchain), :] for j in range(S5_SLABS)]
            xs += [xi_ref[j, pl.ds(kc, tt, stride=nchain), :] for j in range(S5_SLABS)]
            xall.append(jnp.concatenate(xs, axis=1).astype(_BF16))
        halves.append(_dot(jnp.concatenate(xall, axis=0), cblk_ref[hf]))
    w = u_ref.shape[-1]
    y = jnp.concatenate(halves, axis=1)
    y = y + dskip_ref[...] * u_ref[...].reshape(nbatch * tt, w).astype(_F32)
    y = _gelu_tanh(y)
    y = y * _sigmoid(_dot(y.astype(_BF16), wglu_ref[...]) + bglu_ref[...])
    y = y * zb_ref[...].reshape(nbatch * tt, w).astype(_F32)
    o_ref[...] = y.astype(_BF16).reshape(nbatch, tt, w)


def _s5_scan(u, zb, ar_t, ai_t, bblk, cblk, d_skip, w_glu, b_glu, tt=256, slabs_per_pass=4):
    bsz, s, w = u.shape
    nchain = bsz * S5_HALVES
    assert nchain == SUBLANES, "the scan packs batch x state-half chains into the 8 sublanes"
    tok_spec = pl.BlockSpec((bsz, tt, w), lambda t: (0, t, 0))
    const = lambda shape: pl.BlockSpec(shape, lambda t: (0,) * len(shape), pipeline_mode=pl.Buffered(1))
    return pl.pallas_call(
        functools.partial(_s5_kernel, tt=tt, nbatch=bsz, slabs_per_pass=slabs_per_pass),
        grid=(s // tt,),
        in_specs=[tok_spec, tok_spec, const(ar_t.shape), const(ai_t.shape),
                  const(bblk.shape), const(cblk.shape), const((1, w)), const(w_glu.shape), const((1, w))],
        out_specs=tok_spec,
        out_shape=jax.ShapeDtypeStruct((bsz, s, w), _BF16),
        scratch_shapes=[pltpu.VMEM((S5_SLABS, tt * nchain, LANES), _F32),
                        pltpu.VMEM((S5_SLABS, tt * nchain, LANES), _F32),
                        pltpu.VMEM((S5_SLABS, nchain, LANES), _F32),
                        pltpu.VMEM((S5_SLABS, nchain, LANES), _F32)],
        compiler_params=pltpu.CompilerParams(
            dimension_semantics=("arbitrary",), vmem_limit_bytes=VMEM_LIMIT),
        name="s5_scan",
    )(u, zb, ar_t, ai_t, bblk, cblk, d_skip.reshape(1, w), w_glu, b_glu.reshape(1, w))


def _out_kernel(x_ref, ya_ref, yb_ref, ga_ref, gb_ref, gate_ref, wua_ref, wub_ref, wo_ref, gf_ref, o_ref):
    ma = ga_ref[0].astype(_F32) * _dot(ya_ref[0], wua_ref[...])
    mb = gb_ref[0].astype(_F32) * _dot(yb_ref[0], wub_ref[...])
    merged = (ma + mb).astype(_BF16)
    xn = x_ref[0] + gate_ref[0] * _dot(merged, wo_ref[...])
    ms = jnp.mean(xn * xn, axis=-1, keepdims=True)
    o_ref[0] = xn * lax.rsqrt(ms + EPS) * gf_ref[...]


def _out_proj(x, ya, yb, ga, gb, gate, w_up_a, w_up_b, w_out, g_final, tm=1024):
    bsz, s, d = x.shape
    tok_spec = lambda width: pl.BlockSpec((1, tm, width), lambda b, i: (b, i, 0))
    const = lambda shape: pl.BlockSpec(shape, lambda b, i: (0,) * len(shape), pipeline_mode=pl.Buffered(1))
    return pl.pallas_call(
        _out_kernel,
        grid=(bsz, s // tm),
        in_specs=[tok_spec(d), tok_spec(FOX_WIDTH), tok_spec(S5_WIDTH), tok_spec(d), tok_spec(d),
                  pl.BlockSpec((1, 1, d), lambda b, i: (b, 0, 0)),
                  const(w_up_a.shape), const(w_up_b.shape), const(w_out.shape), const((1, d))],
        out_specs=tok_spec(d),
        out_shape=jax.ShapeDtypeStruct((bsz, s, d), x.dtype),
        compiler_params=pltpu.CompilerParams(
            dimension_semantics=("arbitrary", "arbitrary"), vmem_limit_bytes=VMEM_LIMIT),
        name="out_proj",
    )(x, ya, yb, ga, gb, gate, w_up_a, w_up_b, w_out, g_final.reshape(1, d))


def _block_diag_in(bb):
    gh = S5_GROUPS // S5_HALVES
    bb = bb.reshape(S5_HALVES, gh, S5_GROUP, S5_STATE)
    eye = jnp.eye(gh, dtype=bb.dtype)
    out = bb[:, :, :, None, :] * eye[None, :, None, :, None]
    return out.reshape(S5_HALVES, gh * S5_GROUP, gh * S5_STATE)


def _block_diag_out(cc):
    gh = S5_GROUPS // S5_HALVES
    cc = cc.reshape(S5_HALVES, gh, S5_GROUP, S5_STATE)
    eye = jnp.eye(gh, dtype=cc.dtype)
    out = jnp.swapaxes(cc, 2, 3)[:, :, :, None, :] * eye[None, :, None, :, None]
    return out.reshape(S5_HALVES, gh * S5_STATE, gh * S5_GROUP)


def _chain_tiles(a, nbatch):
    a = a.reshape(S5_HALVES, S5_SLABS, LANES)
    a = jnp.swapaxes(a, 0, 1)
    return jnp.tile(a, (1, nbatch, 1))


def kernel(x, c, w_ada, b_ada, g_norm, w_in, b_f, a_re, a_im, log_dt, b_re, b_im, c_re, c_im,
           d_skip, w_glu, b_glu, w_up_a, w_up_b, w_out, g_final):
    bsz, s, d = x.shape
    depth = w_ada.shape[0]
    assert depth == 1 and d == D_MODEL

    mod = _adaln_mod(c, w_ada[0], b_ada[0])
    shift = mod[:, 0:d].reshape(bsz, 1, d)
    scale = mod[:, d:2 * d].reshape(bsz, 1, d)
    gate = mod[:, 2 * d:3 * d].reshape(bsz, 1, d)

    w = w_in[0]
    o_f = 3 * FOX_WIDTH
    o_z = o_f + HEADS
    o_g = o_z + FOX_WIDTH + 2 * S5_WIDTH
    wq = w[:, 0:FOX_WIDTH].astype(_BF16)
    wkt = w[:, FOX_WIDTH:2 * FOX_WIDTH].T.astype(_BF16)
    wv = w[:, 2 * FOX_WIDTH:o_f].astype(_BF16)
    wf = jnp.pad(w[:, o_f:o_z], ((0, 0), (0, LANES - HEADS))).astype(_BF16)
    wz = w[:, o_z:o_g].astype(_BF16)
    wg = w[:, o_g:].astype(_BF16)
    bf_pad = jnp.pad(b_f[0], (0, LANES - HEADS)).reshape(1, LANES)

    q, kt, v, qaug, kaugt, za, u, zb, ga, gb, stats = _in_proj(
        x, g_norm[0], scale, shift, wq, wkt, wv, wf, wz, wg, bf_pad)

    plan = _skip_plan(stats)
    j_first = jnp.swapaxes(plan[:, :, 0:HEADS:2], 1, 2).reshape(-1)
    ya = _fox_attention(j_first, q, kt, kaugt, v, qaug, za)

    abr, abi, bbr, bbi = _s5_params(a_re[0], a_im[0], log_dt[0],
                                    jnp.swapaxes(b_re[0], 1, 2), jnp.swapaxes(b_im[0], 1, 2))
    bblk = jnp.concatenate([_block_diag_in(bbr), _block_diag_in(bbi)], axis=2).astype(_BF16)
    cblk = jnp.concatenate([_block_diag_out(c_re[0]), -_block_diag_out(c_im[0])], axis=1).astype(_BF16)
    yb = _s5_scan(u, zb, _chain_tiles(abr, bsz), _chain_tiles(abi, bsz), bblk, cblk,
                  d_skip[0], w_glu[0].astype(_BF16), b_glu[0])

    return _out_proj(x, ya, yb, ga, gb, gate, w_up_a[0].astype(_BF16), w_up_b[0].astype(_BF16),
                     w_out[0].astype(_BF16), g_final)
```

```python
import functools
import math

import jax
import jax.numpy as jnp
from jax import lax
from jax.experimental import pallas as pl
from jax.experimental.pallas import tpu as pltpu

D_MODEL = 1024
EPS = 1e-6
FOX_WIDTH = 512
HEAD_DIM = 64
HEADS = 8
HEAD_PAIRS = HEADS // 2
S5_WIDTH = 512
S5_GROUP = 16
S5_GROUPS = 32
S5_STATE = 64

LANES = 128
SUBLANES = 8
VMEM_LIMIT = 56 * 1024 * 1024

S5_HALVES = 2
S5_HALF_CH = S5_WIDTH // S5_HALVES
S5_HALF_STATE = S5_GROUPS * S5_STATE // S5_HALVES
S5_SLABS = S5_HALF_STATE // LANES

NEG_BIG = -1e30
LOG2E = math.log2(math.e)

ATTN_TILE = 512
SKIP_LOG2 = 40.0
NORM_SAFETY = 1.02

_F32 = jnp.float32
_BF16 = jnp.bfloat16


def _dot(a, b):
    return jnp.dot(a, b, preferred_element_type=_F32)


def _split3(x):
    hi = x.astype(_BF16).astype(_F32)
    r1 = x - hi
    mid = r1.astype(_BF16).astype(_F32)
    lo = (r1 - mid).astype(_BF16).astype(_F32)
    return hi, mid, lo


def _sigmoid(x):
    return 1.0 / (1.0 + jnp.exp(-x))


def _silu(x):
    return x * _sigmoid(x)


def _adaln_kernel(c_ref, w_ref, b_ref, o_ref):
    o_ref[...] = jnp.dot(c_ref[...], w_ref[...], precision=lax.Precision.HIGHEST,
                         preferred_element_type=_F32) + b_ref[...]


def _adaln_mod(c, w, b):
    bsz, d = c.shape
    n = w.shape[1]
    bn = 1024
    return pl.pallas_call(
        _adaln_kernel,
        grid=(n // bn,),
        in_specs=[pl.BlockSpec((bsz, d), lambda j: (0, 0)),
                  pl.BlockSpec((d, bn), lambda j: (0, j)),
                  pl.BlockSpec((1, bn), lambda j: (0, j))],
        out_specs=pl.BlockSpec((bsz, bn), lambda j: (0, j)),
        out_shape=jax.ShapeDtypeStruct((bsz, n), _F32),
        name="adaln_mod",
    )(c, w, b.reshape(1, n))


def _s5_param_kernel(are_ref, aim_ref, ldt_ref, bre_ref, bim_ref,
                     abr_ref, abi_ref, bbr_ref, bbi_ref):
    lr = are_ref[...]
    li = aim_ref[...]
    dt = jnp.exp(ldt_ref[...])
    mag = jnp.exp(lr * dt)
    ang = li * dt
    abr = mag * jnp.cos(ang)
    abi = mag * jnp.sin(ang)
    abr_ref[...] = abr
    abi_ref[...] = abi
    nr = abr - 1.0
    ni = abi
    den = lr * lr + li * li
    cr = (nr * lr + ni * li) / den
    ci = (ni * lr - nr * li) / den
    cr3 = cr[:, None, :]
    ci3 = ci[:, None, :]
    br = bre_ref[...]
    bi = bim_ref[...]
    bbr_ref[...] = cr3 * br - ci3 * bi
    bbi_ref[...] = cr3 * bi + ci3 * br


def _s5_params(a_re, a_im, log_dt, b_re_t, b_im_t):
    g, p = a_re.shape
    cg = b_re_t.shape[1]
    return pl.pallas_call(
        _s5_param_kernel,
        out_shape=(jax.ShapeDtypeStruct((g, p), _F32), jax.ShapeDtypeStruct((g, p), _F32),
                   jax.ShapeDtypeStruct((g, cg, p), _F32), jax.ShapeDtypeStruct((g, cg, p), _F32)),
        name="s5_params",
    )(a_re, a_im, log_dt.reshape(g, 1), b_re_t, b_im_t)


def _in_proj_kernel(x_ref, gn_ref, sc_ref, sh_ref, wq_ref, wkt_ref, wv_ref, wf_ref, wz_ref, wg_ref, bf_ref,
                    q_ref, kt_ref, v_ref, qaug_ref, kaugt_ref, za_ref, u_ref, zb_ref, ga_ref, gb_ref,
                    stats_ref, carry_ref, kmax_ref, *, tm):
    i = pl.program_id(1)

    @pl.when(i == 0)
    def _():
        carry_ref[...] = jnp.zeros_like(carry_ref)
        kmax_ref[...] = jnp.zeros_like(kmax_ref)

    x = x_ref[0]
    ms = jnp.mean(x * x, axis=-1, keepdims=True)
    h = x * lax.rsqrt(ms + EPS) * gn_ref[...]
    h = (h * (1.0 + sc_ref[0]) + sh_ref[0]).astype(_BF16)

    w = FOX_WIDTH
    qb = (_dot(h, wq_ref[...]) * (HEAD_DIM ** -0.5 * LOG2E)).astype(_BF16)
    q_ref[0] = qb
    ktb = lax.dot_general(wkt_ref[...], h, (((1,), (1,)), ((), ())),
                          preferred_element_type=_F32).astype(_BF16)
    kt_ref[0] = ktb
    v_ref[0] = _dot(h, wv_ref[...]).astype(_BF16)

    seg = (lax.broadcasted_iota(jnp.int32, (w, LANES), 0) // HEAD_DIM
           == lax.broadcasted_iota(jnp.int32, (w, LANES), 1)).astype(_BF16)
    seg_t = (lax.broadcasted_iota(jnp.int32, (LANES, w), 1) // HEAD_DIM
             == lax.broadcasted_iota(jnp.int32, (LANES, w), 0)).astype(_BF16)
    qf = qb.astype(_F32)
    kf = ktb.astype(_F32)
    qn2 = _dot((qf * qf).astype(_BF16), seg)
    kn2 = _dot(seg_t, (kf * kf).astype(_BF16)).T
    kmax = jnp.maximum(kmax_ref[...], jnp.max(kn2, axis=0, keepdims=True))
    kmax_ref[...] = kmax
    za_ref[0] = _silu(_dot(h, wz_ref[:, 0:w])).astype(_BF16)
    u_ref[0] = _dot(h, wz_ref[:, w:2 * w]).astype(_BF16)
    zb_ref[0] = _silu(_dot(h, wz_ref[:, 2 * w:3 * w])).astype(_BF16)
    ga_ref[0] = _sigmoid(_dot(h, wg_ref[:, 0:D_MODEL])).astype(_BF16)
    gb_ref[0] = _sigmoid(_dot(h, wg_ref[:, D_MODEL:2 * D_MODEL])).astype(_BF16)

    fl = _dot(h, wf_ref[...]) + bf_ref[...]
    logf = -(jnp.maximum(-fl, 0.0) + jnp.log1p(jnp.exp(-jnp.abs(fl)))) * LOG2E
    hi, mid, lo = _split3(logf)
    lmh = jnp.concatenate([hi, mid, lo], axis=1).astype(_BF16)

    ta = ATTN_TILE
    row = lax.broadcasted_iota(jnp.int32, (ta, ta), 0)
    col = lax.broadcasted_iota(jnp.int32, (ta, ta), 1)
    tri = (col <= row).astype(_BF16)
    lane = lax.broadcasted_iota(jnp.int32, (1, LANES), 1)
    ones = jnp.ones((HEADS, ta), _F32)
    zeros = jnp.zeros((HEADS, ta), _F32)
    nt = tm // ta
    carry = carry_ref[...]
    for t in range(nt):
        r0 = t * ta
        parts = _dot(tri, lmh[r0:r0 + ta])
        cum = (parts[:, 0:LANES] + parts[:, LANES:2 * LANES]) + parts[:, 2 * LANES:3 * LANES] + carry
        carry = cum[ta - 1:ta, :]

        tile = i * nt + t
        stats_ref[0, 0, pl.ds(tile, 1), :] = jnp.max(qn2[r0:r0 + ta], axis=0, keepdims=True)
        stats_ref[0, 1, pl.ds(tile, 1), :] = cum[0:1, :]
        stats_ref[0, 2, pl.ds(tile, 1), :] = carry

        qh, qm, ql = _split3(jnp.where(lane < HEADS, cum, 0.0))
        qaug_ref[0, r0:r0 + ta, :] = (qh + pltpu.roll(qm, HEADS, 1)
                                      + pltpu.roll(ql, 2 * HEADS, 1)).astype(_BF16)
        kh, km, kl = _split3(cum.T[0:HEADS, :])
        kaugt_ref[0, :, r0:r0 + ta] = jnp.concatenate(
            [ones, ones, ones, zeros, kh, km, kl] + [zeros] * 9, axis=0).astype(_BF16)
    carry_ref[...] = carry
    stats_ref[0, 3] = jnp.broadcast_to(kmax, stats_ref.shape[2:])


def _in_proj(x, g_norm, scale, shift, wq, wkt, wv, wf, wz, wg, bf_pad, tm=1024):
    bsz, s, d = x.shape
    ni = s // tm
    tok = lambda width, dt: jax.ShapeDtypeStruct((bsz, s, width), dt)
    tok_spec = lambda width: pl.BlockSpec((1, tm, width), lambda b, i: (b, i, 0))
    tr = lambda rows: jax.ShapeDtypeStruct((bsz, rows, s), _BF16)
    tr_spec = lambda rows: pl.BlockSpec((1, rows, tm), lambda b, i: (b, 0, i))
    const = lambda shape: pl.BlockSpec(shape, lambda b, i: (0,) * len(shape), pipeline_mode=pl.Buffered(1))
    per_b = pl.BlockSpec((1, 1, d), lambda b, i: (b, 0, 0))
    return pl.pallas_call(
        functools.partial(_in_proj_kernel, tm=tm),
        grid=(bsz, ni),
        in_specs=[tok_spec(d), const((1, d)), per_b, per_b,
                  const(wq.shape), const(wkt.shape), const(wv.shape), const(wf.shape),
                  const(wz.shape), const(wg.shape), const((1, LANES))],
        out_specs=[tok_spec(FOX_WIDTH), tr_spec(FOX_WIDTH), tok_spec(FOX_WIDTH),
                   tok_spec(LANES), tr_spec(LANES),
                   tok_spec(FOX_WIDTH), tok_spec(S5_WIDTH), tok_spec(S5_WIDTH),
                   tok_spec(d), tok_spec(d),
                   pl.BlockSpec((1, 4, s // ATTN_TILE, LANES), lambda b, i: (b, 0, 0, 0))],
        out_shape=[tok(FOX_WIDTH, _BF16), tr(FOX_WIDTH), tok(FOX_WIDTH, _BF16),
                   tok(LANES, _BF16), tr(LANES),
                   tok(FOX_WIDTH, _BF16), tok(S5_WIDTH, _BF16), tok(S5_WIDTH, _BF16),
                   tok(d, _BF16), tok(d, _BF16),
                   jax.ShapeDtypeStruct((bsz, 4, s // ATTN_TILE, LANES), _F32)],
        scratch_shapes=[pltpu.VMEM((1, LANES), _F32), pltpu.VMEM((1, LANES), _F32)],
        compiler_params=pltpu.CompilerParams(
            dimension_semantics=("arbitrary", "arbitrary"), vmem_limit_bytes=VMEM_LIMIT),
        name="in_proj",
    )(x, g_norm.reshape(1, d), scale, shift, wq, wkt, wv, wf, wz, wg, bf_pad)


def _plan_kernel(st_ref, o_ref):
    qn2 = st_ref[0, 0]
    fstart = st_ref[0, 1]
    kmax2 = st_ref[0, 3]
    o_ref[0] = fstart + (NORM_SAFETY * jnp.sqrt(qn2 * kmax2) + SKIP_LOG2)


def _skip_plan(stats):
    bsz, _, nq, _ = stats.shape
    return pl.pallas_call(
        _plan_kernel,
        grid=(bsz,),
        in_specs=[pl.BlockSpec((1, 4, nq, LANES), lambda b: (b, 0, 0, 0))],
        out_specs=pl.BlockSpec((1, nq, LANES), lambda b: (b, 0, 0)),
        out_shape=jax.ShapeDtypeStruct((bsz, nq, LANES), _F32),
        name="skip_plan",
    )(stats)


def _attn_kernel(base_ref, fend_ref, q_ref, kt_ref, kaugt_ref, v_ref, qaug_ref, za_ref, o_ref,
                 lhs_ref, s_ref, p_ref, m_ref, alpha_ref, acc_ref, *, tq, tk, rb):
    b = pl.program_id(0)
    hp = pl.program_id(1)
    i = pl.program_id(2)
    nq = pl.num_programs(2)
    lane = lax.broadcasted_iota(jnp.int32, (1, LANES), 1)
    q = q_ref[0]
    qa = qaug_ref[0]

    head_lanes = [(lane >= hh * HEAD_DIM) & (lane < (hh + 1) * HEAD_DIM) for hh in range(2)]
    for hh in range(2):
        h = 2 * hp + hh
        f_lanes = (lane == h) | (lane == HEADS + h) | (lane == 2 * HEADS + h)
        neg_lanes = (lane == 4 * HEADS + h) | (lane == 5 * HEADS + h) | (lane == 6 * HEADS + h)
        qm = jnp.where(head_lanes[hh], q, jnp.zeros_like(q))
        qaug = jnp.where(f_lanes, qa, jnp.where(neg_lanes, -1.0, 0.0).astype(_BF16))
        lhs_ref[hh] = jnp.concatenate([qm, qaug], axis=1)
        m_ref[hh] = jnp.full((tq, LANES), NEG_BIG, _F32)
        acc_ref[hh] = jnp.zeros((tq, LANES), _F32)

    def step(j, masked):
        k0 = pl.multiple_of(j * tk, tk)
        w = jnp.concatenate([kt_ref[0, :, pl.ds(k0, tk)], kaugt_ref[0, :, pl.ds(k0, tk)]], axis=0)
        vb = v_ref[0, pl.ds(k0, tk), :]
        for hh in range(2):
            s = _dot(lhs_ref[hh], w)
            if masked:
                r = lax.broadcasted_iota(jnp.int32, (tq, tk), 0)
                c = lax.broadcasted_iota(jnp.int32, (tq, tk), 1)
                s = jnp.where(c <= r, s, NEG_BIG)
            s_ref[hh] = s
        for hh in range(2):
            for r0 in range(0, tq, rb):
                chunks = [s_ref[hh, r0:r0 + rb, c0:c0 + LANES] for c0 in range(0, tk, LANES)]
                m_old = m_ref[hh, r0:r0 + rb, :]
                m_new = jnp.maximum(m_old, jnp.max(functools.reduce(jnp.maximum, chunks), axis=1, keepdims=True))
                alpha_ref[hh, r0:r0 + rb, :] = jnp.exp2(m_old - m_new)
                m_ref[hh, r0:r0 + rb, :] = m_new
                for c0, ch in zip(range(0, tk, LANES), chunks):
                    p_ref[hh, r0:r0 + rb, c0:c0 + LANES] = jnp.exp2(ch - m_new).astype(_BF16)
        for hh in range(2):
            sum_lane = (1 - hh) * HEAD_DIM
            vaug = jnp.where(head_lanes[hh], vb, jnp.where(lane == sum_lane, 1.0, 0.0).astype(_BF16))
            acc_ref[hh] = alpha_ref[hh] * acc_ref[hh] + _dot(p_ref[hh], vaug)

    step(i, True)

    def first_needed(hh):
        row = (b * HEADS + 2 * hp + hh) * nq
        thr = base_ref[row + i] - jnp.min(m_ref[hh])

        def count(jj, c):
            return c + (fend_ref[row + jj] >= thr).astype(jnp.int32)

        return lax.fori_loop(0, i, count, jnp.int32(0))

    j_first = jnp.minimum(first_needed(0), first_needed(1))

    def full_step(j, carry):
        step(j, False)
        return carry

    lax.fori_loop(j_first, i, full_step, 0)

    a0 = acc_ref[0]
    a1 = acc_ref[1]
    l0 = a0[:, HEAD_DIM:HEAD_DIM + 1]
    l1 = a1[:, 0:1]
    y = jnp.where(lane < HEAD_DIM, a0 / l0, a1 / l1)
    o_ref[0] = (y * za_ref[0].astype(_F32)).astype(_BF16)


def _fox_attention(base, fend, q, kt, kaugt, v, qaug, za, rb=32):
    bsz, s, _ = q.shape
    tq = tk = ATTN_TILE
    nq = s // tq
    qspec = pl.BlockSpec((1, tq, LANES), lambda b, hp, i, *_: (b, i, hp))
    grid_spec = pltpu.PrefetchScalarGridSpec(
        num_scalar_prefetch=2,
        grid=(bsz, HEAD_PAIRS, nq),
        in_specs=[qspec,
                  pl.BlockSpec((1, LANES, s), lambda b, hp, i, *_: (b, hp, 0)),
                  pl.BlockSpec((1, LANES, s), lambda b, hp, i, *_: (b, 0, 0)),
                  pl.BlockSpec((1, s, LANES), lambda b, hp, i, *_: (b, 0, hp)),
                  pl.BlockSpec((1, tq, LANES), lambda b, hp, i, *_: (b, i, 0)),
                  qspec],
        out_specs=qspec,
        scratch_shapes=[pltpu.VMEM((2, tq, 2 * LANES), _BF16),
                        pltpu.VMEM((2, tq, tk), _F32), pltpu.VMEM((2, tq, tk), _BF16),
                        pltpu.VMEM((2, tq, LANES), _F32), pltpu.VMEM((2, tq, LANES), _F32),
                        pltpu.VMEM((2, tq, LANES), _F32)])
    return pl.pallas_call(
        functools.partial(_attn_kernel, tq=tq, tk=tk, rb=rb),
        grid_spec=grid_spec,
        out_shape=jax.ShapeDtypeStruct((bsz, s, FOX_WIDTH), _BF16),
        compiler_params=pltpu.CompilerParams(
            dimension_semantics=("arbitrary", "arbitrary", "arbitrary"), vmem_limit_bytes=VMEM_LIMIT),
        name="fox_attn",
    )(base, fend, q, kt, kaugt, v, qaug, za)


def _gelu_tanh(x):
    return 0.5 * x * (1.0 + jnp.tanh(math.sqrt(2.0 / math.pi) * (x + 0.044715 * (x * x * x))))


def _s5_kernel(u_ref, zb_ref, ar_ref, ai_ref, bblk_ref, cblk_ref, dskip_ref, wglu_ref, bglu_ref,
               o_ref, xr_ref, xi_ref, sr_ref, si_ref, *, tt, nbatch, slabs_per_pass):
    t = pl.program_id(0)
    nchain = nbatch * S5_HALVES

    @pl.when(t == 0)
    def _():
        sr_ref[...] = jnp.zeros_like(sr_ref)
        si_ref[...] = jnp.zeros_like(si_ref)

    for hf in range(S5_HALVES):
        ub = jnp.concatenate([u_ref[b, :, hf * S5_HALF_CH:(hf + 1) * S5_HALF_CH] for b in range(nbatch)],
                             axis=0)
        bu = _dot(ub, bblk_ref[hf])
        for b in range(nbatch):
            kc = b * S5_HALVES + hf
            rows = slice(b * tt, (b + 1) * tt)
            for j in range(S5_SLABS):
                xr_ref[j, pl.ds(kc, tt, stride=nchain), :] = bu[rows, j * LANES:(j + 1) * LANES]
                xi_ref[j, pl.ds(kc, tt, stride=nchain), :] = bu[rows, (S5_SLABS + j) * LANES:
                                                                (S5_SLABS + j + 1) * LANES]

    for j0 in range(0, S5_SLABS, slabs_per_pass):
        js = list(range(j0, j0 + slabs_per_pass))
        ars = [ar_ref[j] for j in js]
        ais = [ai_ref[j] for j in js]

        def body(s, carry, js=js, ars=ars, ais=ais):
            r0 = pl.multiple_of(s * nchain, nchain)
            new = []
            for n, j in enumerate(js):
                sr, si = carry[2 * n], carry[2 * n + 1]
                br = xr_ref[j, pl.ds(r0, nchain), :]
                bi = xi_ref[j, pl.ds(r0, nchain), :]
                nr = (ars[n] * sr - ais[n] * si) + br
                ni = (ars[n] * si + ais[n] * sr) + bi
                xr_ref[j, pl.ds(r0, nchain), :] = nr
                xi_ref[j, pl.ds(r0, nchain), :] = ni
                new += [nr, ni]
            return tuple(new)

        init = []
        for j in js:
            init += [sr_ref[j], si_ref[j]]
        fin = lax.fori_loop(0, tt, body, tuple(init), unroll=4)
        for n, j in enumerate(js):
            sr_ref[j] = fin[2 * n]
            si_ref[j] = fin[2 * n + 1]

    halves = []
    for hf in range(S5_HALVES):
        xall = []
        for b in range(nbatch):
            kc = b * S5_HALVES + hf
            xs = [xr_ref[j, pl.ds(kc, tt, stride=nchain), :] for j in range(S5_SLABS)]
            xs += [xi_ref[j, pl.ds(kc, tt, stride=nchain), :] for j in range(S5_SLABS)]
            xall.append(jnp.concatenate(xs, axis=1).astype(_BF16))
        halves.append(_dot(jnp.concatenate(xall, axis=0), cblk_ref[hf]))
    w = u_ref.shape[-1]
    y = jnp.concatenate(halves, axis=1)
    y = y + dskip_ref[...] * u_ref[...].reshape(nbatch * tt, w).astype(_F32)
    y = _gelu_tanh(y)
    y = y * _sigmoid(_dot(y.astype(_BF16), wglu_ref[...]) + bglu_ref[...])
    y = y * zb_ref[...].reshape(nbatch * tt, w).astype(_F32)
    o_ref[...] = y.astype(_BF16).reshape(nbatch, tt, w)


def _s5_scan(u, zb, ar_t, ai_t, bblk, cblk, d_skip, w_glu, b_glu, tt=256, slabs_per_pass=4):
    bsz, s, w = u.shape
    nchain = bsz * S5_HALVES
    assert nchain == SUBLANES, "the scan packs batch x state-half chains into the 8 sublanes"
    tok_spec = pl.BlockSpec((bsz, tt, w), lambda t: (0, t, 0))
    const = lambda shape: pl.BlockSpec(shape, lambda t: (0,) * len(shape), pipeline_mode=pl.Buffered(1))
    return pl.pallas_call(
        functools.partial(_s5_kernel, tt=tt, nbatch=bsz, slabs_per_pass=slabs_per_pass),
        grid=(s // tt,),
        in_specs=[tok_spec, tok_spec, const(ar_t.shape), const(ai_t.shape),
                  const(bblk.shape), const(cblk.shape), const((1, w)), const(w_glu.shape), const((1, w))],
        out_specs=tok_spec,
        out_shape=jax.ShapeDtypeStruct((bsz, s, w), _BF16),
        scratch_shapes=[pltpu.VMEM((S5_SLABS, tt * nchain, LANES), _F32),
                        pltpu.VMEM((S5_SLABS, tt * nchain, LANES), _F32),
                        pltpu.VMEM((S5_SLABS, nchain, LANES), _F32),
                        pltpu.VMEM((S5_SLABS, nchain, LANES), _F32)],
        compiler_params=pltpu.CompilerParams(
            dimension_semantics=("arbitrary",), vmem_limit_bytes=VMEM_LIMIT),
        name="s5_scan",
    )(u, zb, ar_t, ai_t, bblk, cblk, d_skip.reshape(1, w), w_glu, b_glu.reshape(1, w))


def _out_kernel(x_ref, ya_ref, yb_ref, ga_ref, gb_ref, gate_ref, wua_ref, wub_ref, wo_ref, gf_ref, o_ref):
    ma = ga_ref[0].astype(_F32) * _dot(ya_ref[0], wua_ref[...])
    mb = gb_ref[0].astype(_F32) * _dot(yb_ref[0], wub_ref[...])
    merged = (ma + mb).astype(_BF16)
    xn = x_ref[0] + gate_ref[0] * _dot(merged, wo_ref[...])
    ms = jnp.mean(xn * xn, axis=-1, keepdims=True)
    o_ref[0] = xn * lax.rsqrt(ms + EPS) * gf_ref[...]


def _out_proj(x, ya, yb, ga, gb, gate, w_up_a, w_up_b, w_out, g_final, tm=1024):
    bsz, s, d = x.shape
    tok_spec = lambda width: pl.BlockSpec((1, tm, width), lambda b, i: (b, i, 0))
    const = lambda shape: pl.BlockSpec(shape, lambda b, i: (0,) * len(shape), pipeline_mode=pl.Buffered(1))
    return pl.pallas_call(
        _out_kernel,
        grid=(bsz, s // tm),
        in_specs=[tok_spec(d), tok_spec(FOX_WIDTH), tok_spec(S5_WIDTH), tok_spec(d), tok_spec(d),
                  pl.BlockSpec((1, 1, d), lambda b, i: (b, 0, 0)),
                  const(w_up_a.shape), const(w_up_b.shape), const(w_out.shape), const((1, d))],
        out_specs=tok_spec(d),
        out_shape=jax.ShapeDtypeStruct((bsz, s, d), x.dtype),
        compiler_params=pltpu.CompilerParams(
            dimension_semantics=("arbitrary", "arbitrary"), vmem_limit_bytes=VMEM_LIMIT),
        name="out_proj",
    )(x, ya, yb, ga, gb, gate, w_up_a, w_up_b, w_out, g_final.reshape(1, d))


def _block_diag_in(bb):
    gh = S5_GROUPS // S5_HALVES
    bb = bb.reshape(S5_HALVES, gh, S5_GROUP, S5_STATE)
    eye = jnp.eye(gh, dtype=bb.dtype)
    out = bb[:, :, :, None, :] * eye[None, :, None, :, None]
    return out.reshape(S5_HALVES, gh * S5_GROUP, gh * S5_STATE)


def _block_diag_out(cc):
    gh = S5_GROUPS // S5_HALVES
    cc = cc.reshape(S5_HALVES, gh, S5_GROUP, S5_STATE)
    eye = jnp.eye(gh, dtype=cc.dtype)
    out = jnp.swapaxes(cc, 2, 3)[:, :, :, None, :] * eye[None, :, None, :, None]
    return out.reshape(S5_HALVES, gh * S5_STATE, gh * S5_GROUP)


def _chain_tiles(a, nbatch):
    a = a.reshape(S5_HALVES, S5_SLABS, LANES)
    a = jnp.swapaxes(a, 0, 1)
    return jnp.tile(a, (1, nbatch, 1))


def kernel(x, c, w_ada, b_ada, g_norm, w_in, b_f, a_re, a_im, log_dt, b_re, b_im, c_re, c_im,
           d_skip, w_glu, b_glu, w_up_a, w_up_b, w_out, g_final):
    bsz, s, d = x.shape
    depth = w_ada.shape[0]
    assert depth == 1 and d == D_MODEL

    mod = _adaln_mod(c, w_ada[0], b_ada[0])
    shift = mod[:, 0:d].reshape(bsz, 1, d)
    scale = mod[:, d:2 * d].reshape(bsz, 1, d)
    gate = mod[:, 2 * d:3 * d].reshape(bsz, 1, d)

    w = w_in[0]
    o_f = 3 * FOX_WIDTH
    o_z = o_f + HEADS
    o_g = o_z + FOX_WIDTH + 2 * S5_WIDTH
    wq = w[:, 0:FOX_WIDTH].astype(_BF16)
    wkt = w[:, FOX_WIDTH:2 * FOX_WIDTH].T.astype(_BF16)
    wv = w[:, 2 * FOX_WIDTH:o_f].astype(_BF16)
    wf = jnp.pad(w[:, o_f:o_z], ((0, 0), (0, LANES - HEADS))).astype(_BF16)
    wz = w[:, o_z:o_g].astype(_BF16)
    wg = w[:, o_g:].astype(_BF16)
    bf_pad = jnp.pad(b_f[0], (0, LANES - HEADS)).reshape(1, LANES)

    q, kt, v, qaug, kaugt, za, u, zb, ga, gb, stats = _in_proj(
        x, g_norm[0], scale, shift, wq, wkt, wv, wf, wz, wg, bf_pad)

    base = _skip_plan(stats)
    by_head = lambda t: jnp.swapaxes(t[:, :, 0:HEADS], 1, 2).reshape(-1)
    ya = _fox_attention(by_head(base), by_head(stats[:, 2]), q, kt, kaugt, v, qaug, za)

    abr, abi, bbr, bbi = _s5_params(a_re[0], a_im[0], log_dt[0],
                                    jnp.swapaxes(b_re[0], 1, 2), jnp.swapaxes(b_im[0], 1, 2))
    bblk = jnp.concatenate([_block_diag_in(bbr), _block_diag_in(bbi)], axis=2).astype(_BF16)
    cblk = jnp.concatenate([_block_diag_out(c_re[0]), -_block_diag_out(c_im[0])], axis=1).astype(_BF16)
    yb = _s5_scan(u, zb, _chain_tiles(abr, bsz), _chain_tiles(abi, bsz), bblk, cblk,
                  d_skip[0], w_glu[0].astype(_BF16), b_glu[0])

    return _out_proj(x, ya, yb, ga, gb, gate, w_up_a[0].astype(_BF16), w_up_b[0].astype(_BF16),
                     w_out[0].astype(_BF16), g_final)
```

```python
import functools
import math

import jax
import jax.numpy as jnp
from jax import lax
from jax.experimental import pallas as pl
from jax.experimental.pallas import tpu as pltpu

D_MODEL = 1024
EPS = 1e-6
FOX_WIDTH = 512
HEAD_DIM = 64
HEADS = 8
HEAD_PAIRS = HEADS // 2
S5_WIDTH = 512
S5_GROUP = 16
S5_GROUPS = 32
S5_STATE = 64

LANES = 128
SUBLANES = 8
VMEM_LIMIT = 56 * 1024 * 1024

S5_HALVES = 2
S5_HALF_CH = S5_WIDTH // S5_HALVES
S5_HALF_STATE = S5_GROUPS * S5_STATE // S5_HALVES
S5_SLABS = S5_HALF_STATE // LANES

NEG_BIG = -1e30
LOG2E = math.log2(math.e)

ATTN_TILE = 512
SKIP_LOG2 = 40.0
NORM_SAFETY = 1.02

_F32 = jnp.float32
_BF16 = jnp.bfloat16


def _dot(a, b):
    return jnp.dot(a, b, preferred_element_type=_F32)


def _split3(x):
    hi = x.astype(_BF16).astype(_F32)
    r1 = x - hi
    mid = r1.astype(_BF16).astype(_F32)
    lo = (r1 - mid).astype(_BF16).astype(_F32)
    return hi, mid, lo


def _sigmoid(x):
    return 1.0 / (1.0 + jnp.exp(-x))


def _silu(x):
    return x * _sigmoid(x)


def _adaln_kernel(c_ref, w_ref, b_ref, o_ref):
    o_ref[...] = jnp.dot(c_ref[...], w_ref[...], precision=lax.Precision.HIGHEST,
                         preferred_element_type=_F32) + b_ref[...]


def _adaln_mod(c, w, b):
    bsz, d = c.shape
    n = w.shape[1]
    bn = 1024
    return pl.pallas_call(
        _adaln_kernel,
        grid=(n // bn,),
        in_specs=[pl.BlockSpec((bsz, d), lambda j: (0, 0)),
                  pl.BlockSpec((d, bn), lambda j: (0, j)),
                  pl.BlockSpec((1, bn), lambda j: (0, j))],
        out_specs=pl.BlockSpec((bsz, bn), lambda j: (0, j)),
        out_shape=jax.ShapeDtypeStruct((bsz, n), _F32),
        name="adaln_mod",
    )(c, w, b.reshape(1, n))


def _s5_param_kernel(are_ref, aim_ref, ldt_ref, bre_ref, bim_ref,
                     abr_ref, abi_ref, bbr_ref, bbi_ref):
    lr = are_ref[...]
    li = aim_ref[...]
    dt = jnp.exp(ldt_ref[...])
    mag = jnp.exp(lr * dt)
    ang = li * dt
    abr = mag * jnp.cos(ang)
    abi = mag * jnp.sin(ang)
    abr_ref[...] = abr
    abi_ref[...] = abi
    nr = abr - 1.0
    ni = abi
    den = lr * lr + li * li
    cr = (nr * lr + ni * li) / den
    ci = (ni * lr - nr * li) / den
    cr3 = cr[:, None, :]
    ci3 = ci[:, None, :]
    br = bre_ref[...]
    bi = bim_ref[...]
    bbr_ref[...] = cr3 * br - ci3 * bi
    bbi_ref[...] = cr3 * bi + ci3 * br


def _s5_params(a_re, a_im, log_dt, b_re_t, b_im_t):
    g, p = a_re.shape
    cg = b_re_t.shape[1]
    return pl.pallas_call(
        _s5_param_kernel,
        out_shape=(jax.ShapeDtypeStruct((g, p), _F32), jax.ShapeDtypeStruct((g, p), _F32),
                   jax.ShapeDtypeStruct((g, cg, p), _F32), jax.ShapeDtypeStruct((g, cg, p), _F32)),
        name="s5_params",
    )(a_re, a_im, log_dt.reshape(g, 1), b_re_t, b_im_t)


def _in_proj_kernel(x_ref, gn_ref, sc_ref, sh_ref, w_ref, wkt_ref, bf_ref,
                    q_ref, kt_ref, v_ref, qaug_ref, kaugt_ref, za_ref, u_ref, zb_ref, ga_ref, gb_ref,
                    stats_ref, carry_ref, kmax_ref, *, tm):
    i = pl.program_id(1)
    fw = FOX_WIDTH
    c_v, c_f = 2 * fw, 3 * fw
    c_za = c_f + LANES
    c_u, c_zb = c_za + fw, c_za + fw + S5_WIDTH
    c_ga = c_zb + S5_WIDTH
    c_gb = c_ga + D_MODEL
    wq_ref = w_ref.at[:, 0:fw]
    wv_ref = w_ref.at[:, c_v:c_f]
    wf_ref = w_ref.at[:, c_f:c_za]

    @pl.when(i == 0)
    def _():
        carry_ref[...] = jnp.zeros_like(carry_ref)
        kmax_ref[...] = jnp.zeros_like(kmax_ref)

    x = x_ref[0]
    ms = jnp.mean(x * x, axis=-1, keepdims=True)
    h = x * lax.rsqrt(ms + EPS) * gn_ref[...]
    h = (h * (1.0 + sc_ref[0]) + sh_ref[0]).astype(_BF16)

    w = FOX_WIDTH
    qb = (_dot(h, wq_ref[...]) * (HEAD_DIM ** -0.5 * LOG2E)).astype(_BF16)
    q_ref[0] = qb
    ktb = lax.dot_general(wkt_ref[...], h, (((1,), (1,)), ((), ())),
                          preferred_element_type=_F32).astype(_BF16)
    kt_ref[0] = ktb
    v_ref[0] = _dot(h, wv_ref[...]).astype(_BF16)

    seg = (lax.broadcasted_iota(jnp.int32, (w, LANES), 0) // HEAD_DIM
           == lax.broadcasted_iota(jnp.int32, (w, LANES), 1)).astype(_BF16)
    seg_t = (lax.broadcasted_iota(jnp.int32, (LANES, w), 1) // HEAD_DIM
             == lax.broadcasted_iota(jnp.int32, (LANES, w), 0)).astype(_BF16)
    qf = qb.astype(_F32)
    kf = ktb.astype(_F32)
    qn2 = _dot((qf * qf).astype(_BF16), seg)
    kn2 = _dot(seg_t, (kf * kf).astype(_BF16)).T
    kmax = jnp.maximum(kmax_ref[...], jnp.max(kn2, axis=0, keepdims=True))
    kmax_ref[...] = kmax
    za_ref[0] = _silu(_dot(h, w_ref[:, c_za:c_u])).astype(_BF16)
    u_ref[0] = _dot(h, w_ref[:, c_u:c_zb]).astype(_BF16)
    zb_ref[0] = _silu(_dot(h, w_ref[:, c_zb:c_ga])).astype(_BF16)
    ga_ref[0] = _sigmoid(_dot(h, w_ref[:, c_ga:c_gb])).astype(_BF16)
    gb_ref[0] = _sigmoid(_dot(h, w_ref[:, c_gb:c_gb + D_MODEL])).astype(_BF16)

    lane = lax.broadcasted_iota(jnp.int32, (1, LANES), 1)
    fl = _dot(h, wf_ref[...]) + bf_ref[...]
    logf = -(jnp.maximum(-fl, 0.0) + jnp.log1p(jnp.exp(-jnp.abs(fl)))) * LOG2E
    hi, mid, lo = _split3(jnp.where(lane < HEADS, logf, 0.0))
    lmh = (hi + pltpu.roll(mid, HEADS, 1) + pltpu.roll(lo, 2 * HEADS, 1)).astype(_BF16)

    ta = ATTN_TILE
    row = lax.broadcasted_iota(jnp.int32, (ta, ta), 0)
    col = lax.broadcasted_iota(jnp.int32, (ta, ta), 1)
    tri = (col <= row).astype(_BF16)
    ones = jnp.ones((HEADS, ta), _F32)
    zeros = jnp.zeros((HEADS, ta), _F32)
    nt = tm // ta
    carry = carry_ref[...]
    for t in range(nt):
        r0 = t * ta
        parts = _dot(tri, lmh[r0:r0 + ta])
        cum = (parts + pltpu.roll(parts, LANES - HEADS, 1)) + pltpu.roll(parts, LANES - 2 * HEADS, 1)
        cum = jnp.where(lane < HEADS, cum, 0.0) + carry
        carry = cum[ta - 1:ta, :]

        tile = i * nt + t
        stats_ref[0, 0, pl.ds(tile, 1), :] = jnp.max(qn2[r0:r0 + ta], axis=0, keepdims=True)
        stats_ref[0, 1, pl.ds(tile, 1), :] = cum[0:1, :]
        stats_ref[0, 2, pl.ds(tile, 1), :] = carry

        qh, qm, ql = _split3(jnp.where(lane < HEADS, cum, 0.0))
        qaug_ref[0, r0:r0 + ta, :] = (qh + pltpu.roll(qm, HEADS, 1)
                                      + pltpu.roll(ql, 2 * HEADS, 1)).astype(_BF16)
        kh, km, kl = _split3(cum.T[0:HEADS, :])
        kaugt_ref[0, :, r0:r0 + ta] = jnp.concatenate(
            [ones, ones, ones, zeros, kh, km, kl] + [zeros] * 9, axis=0).astype(_BF16)
    carry_ref[...] = carry
    stats_ref[0, 3] = jnp.broadcast_to(kmax, stats_ref.shape[2:])


def _in_proj(x, g_norm, scale, shift, w_al, wkt, bf_pad, tm=1024):
    bsz, s, d = x.shape
    ni = s // tm
    tok = lambda width, dt: jax.ShapeDtypeStruct((bsz, s, width), dt)
    tok_spec = lambda width: pl.BlockSpec((1, tm, width), lambda b, i: (b, i, 0))
    tr = lambda rows: jax.ShapeDtypeStruct((bsz, rows, s), _BF16)
    tr_spec = lambda rows: pl.BlockSpec((1, rows, tm), lambda b, i: (b, 0, i))
    const = lambda shape: pl.BlockSpec(shape, lambda b, i: (0,) * len(shape), pipeline_mode=pl.Buffered(1))
    per_b = pl.BlockSpec((1, 1, d), lambda b, i: (b, 0, 0))
    return pl.pallas_call(
        functools.partial(_in_proj_kernel, tm=tm),
        grid=(bsz, ni),
        in_specs=[tok_spec(d), const((1, d)), per_b, per_b,
                  const(w_al.shape), const(wkt.shape), const((1, LANES))],
        out_specs=[tok_spec(FOX_WIDTH), tr_spec(FOX_WIDTH), tok_spec(FOX_WIDTH),
                   tok_spec(LANES), tr_spec(LANES),
                   tok_spec(FOX_WIDTH), tok_spec(S5_WIDTH), tok_spec(S5_WIDTH),
                   tok_spec(d), tok_spec(d),
                   pl.BlockSpec((1, 4, s // ATTN_TILE, LANES), lambda b, i: (b, 0, 0, 0))],
        out_shape=[tok(FOX_WIDTH, _BF16), tr(FOX_WIDTH), tok(FOX_WIDTH, _BF16),
                   tok(LANES, _BF16), tr(LANES),
                   tok(FOX_WIDTH, _BF16), tok(S5_WIDTH, _BF16), tok(S5_WIDTH, _BF16),
                   tok(d, _BF16), tok(d, _BF16),
                   jax.ShapeDtypeStruct((bsz, 4, s // ATTN_TILE, LANES), _F32)],
        scratch_shapes=[pltpu.VMEM((1, LANES), _F32), pltpu.VMEM((1, LANES), _F32)],
        compiler_params=pltpu.CompilerParams(
            dimension_semantics=("arbitrary", "arbitrary"), vmem_limit_bytes=VMEM_LIMIT),
        name="in_proj",
    )(x, g_norm.reshape(1, d), scale, shift, w_al, wkt, bf_pad)


def _plan_kernel(st_ref, o_ref):
    qn2 = st_ref[0, 0]
    fstart = st_ref[0, 1]
    kmax2 = st_ref[0, 3]
    o_ref[0] = fstart + (NORM_SAFETY * jnp.sqrt(qn2 * kmax2) + SKIP_LOG2)


def _skip_plan(stats):
    bsz, _, nq, _ = stats.shape
    return pl.pallas_call(
        _plan_kernel,
        grid=(bsz,),
        in_specs=[pl.BlockSpec((1, 4, nq, LANES), lambda b: (b, 0, 0, 0))],
        out_specs=pl.BlockSpec((1, nq, LANES), lambda b: (b, 0, 0)),
        out_shape=jax.ShapeDtypeStruct((bsz, nq, LANES), _F32),
        name="skip_plan",
    )(stats)


def _attn_kernel(base_ref, fend_ref, q_ref, kt_ref, kaugt_ref, v_ref, qaug_ref, za_ref, o_ref,
                 lhs_ref, s_ref, p_ref, m_ref, alpha_ref, acc_ref, *, tq, tk, rb, nq):
    b = pl.program_id(0)
    hp = pl.program_id(1)
    i = pl.program_id(2)
    lane = lax.broadcasted_iota(jnp.int32, (1, LANES), 1)
    q = q_ref[0]
    qa = qaug_ref[0]

    head_lanes = [(lane >= hh * HEAD_DIM) & (lane < (hh + 1) * HEAD_DIM) for hh in range(2)]
    for hh in range(2):
        h = 2 * hp + hh
        f_lanes = (lane == h) | (lane == HEADS + h) | (lane == 2 * HEADS + h)
        neg_lanes = (lane == 4 * HEADS + h) | (lane == 5 * HEADS + h) | (lane == 6 * HEADS + h)
        qm = jnp.where(head_lanes[hh], q, jnp.zeros_like(q))
        qaug = jnp.where(f_lanes, qa, jnp.where(neg_lanes, -1.0, 0.0).astype(_BF16))
        lhs_ref[hh] = jnp.concatenate([qm, qaug], axis=1)
        m_ref[hh] = jnp.full((tq, LANES), NEG_BIG, _F32)
        acc_ref[hh] = jnp.zeros((tq, LANES), _F32)

    def step(j, masked):
        k0 = pl.multiple_of(j * tk, tk)
        w = jnp.concatenate([kt_ref[0, :, pl.ds(k0, tk)], kaugt_ref[0, :, pl.ds(k0, tk)]], axis=0)
        vb = v_ref[0, pl.ds(k0, tk), :]
        for hh in range(2):
            s = _dot(lhs_ref[hh], w)
            if masked:
                r = lax.broadcasted_iota(jnp.int32, (tq, tk), 0)
                c = lax.broadcasted_iota(jnp.int32, (tq, tk), 1)
                s = jnp.where(c <= r, s, NEG_BIG)
            s_ref[hh] = s
        for hh in range(2):
            for r0 in range(0, tq, rb):
                chunks = [s_ref[hh, r0:r0 + rb, c0:c0 + LANES] for c0 in range(0, tk, LANES)]
                m_old = m_ref[hh, r0:r0 + rb, :]
                m_new = jnp.maximum(m_old, jnp.max(functools.reduce(jnp.maximum, chunks), axis=1, keepdims=True))
                alpha_ref[hh, r0:r0 + rb, :] = jnp.exp2(m_old - m_new)
                m_ref[hh, r0:r0 + rb, :] = m_new
                for c0, ch in zip(range(0, tk, LANES), chunks):
                    p_ref[hh, r0:r0 + rb, c0:c0 + LANES] = jnp.exp2(ch - m_new).astype(_BF16)
        for hh in range(2):
            sum_lane = (1 - hh) * HEAD_DIM
            vaug = jnp.where(head_lanes[hh], vb, jnp.where(lane == sum_lane, 1.0, 0.0).astype(_BF16))
            acc_ref[hh] = alpha_ref[hh] * acc_ref[hh] + _dot(p_ref[hh], vaug)

    step(i, True)

    def first_needed(hh):
        row = (b * HEADS + 2 * hp + hh) * nq
        thr = base_ref[row + i] - jnp.min(m_ref[hh])
        count = jnp.int32(0)
        for jj in range(nq):
            count += ((fend_ref[row + jj] >= thr) & (jj < i)).astype(jnp.int32)
        return count

    j_first = jnp.minimum(first_needed(0), first_needed(1))

    n_old = i - j_first
    odd = n_old & 1

    @pl.when(odd == 1)
    def _():
        step(j_first, False)

    j_even = j_first + odd

    def pair_step(t, carry):
        step(j_even + 2 * t, False)
        step(j_even + 2 * t + 1, False)
        return carry

    lax.fori_loop(0, n_old >> 1, pair_step, 0)

    a0 = acc_ref[0]
    a1 = acc_ref[1]
    l0 = a0[:, HEAD_DIM:HEAD_DIM + 1]
    l1 = a1[:, 0:1]
    y = jnp.where(lane < HEAD_DIM, a0 / l0, a1 / l1)
    o_ref[0] = (y * za_ref[0].astype(_F32)).astype(_BF16)


def _fox_attention(base, fend, q, kt, kaugt, v, qaug, za, rb=32):
    bsz, s, _ = q.shape
    tq = tk = ATTN_TILE
    nq = s // tq
    qspec = pl.BlockSpec((1, tq, LANES), lambda b, hp, i, *_: (b, i, hp))
    grid_spec = pltpu.PrefetchScalarGridSpec(
        num_scalar_prefetch=2,
        grid=(bsz, HEAD_PAIRS, nq),
        in_specs=[qspec,
                  pl.BlockSpec((1, LANES, s), lambda b, hp, i, *_: (b, hp, 0)),
                  pl.BlockSpec((1, LANES, s), lambda b, hp, i, *_: (b, 0, 0)),
                  pl.BlockSpec((1, s, LANES), lambda b, hp, i, *_: (b, 0, hp)),
                  pl.BlockSpec((1, tq, LANES), lambda b, hp, i, *_: (b, i, 0)),
                  qspec],
        out_specs=qspec,
        scratch_shapes=[pltpu.VMEM((2, tq, 2 * LANES), _BF16),
                        pltpu.VMEM((2, tq, tk), _F32), pltpu.VMEM((2, tq, tk), _BF16),
                        pltpu.VMEM((2, tq, LANES), _F32), pltpu.VMEM((2, tq, LANES), _F32),
                        pltpu.VMEM((2, tq, LANES), _F32)])
    return pl.pallas_call(
        functools.partial(_attn_kernel, tq=tq, tk=tk, rb=rb, nq=nq),
        grid_spec=grid_spec,
        out_shape=jax.ShapeDtypeStruct((bsz, s, FOX_WIDTH), _BF16),
        compiler_params=pltpu.CompilerParams(
            dimension_semantics=("arbitrary", "arbitrary", "arbitrary"), vmem_limit_bytes=VMEM_LIMIT),
        name="fox_attn",
    )(base, fend, q, kt, kaugt, v, qaug, za)


def _gelu_tanh(x):
    return 0.5 * x * (1.0 + jnp.tanh(math.sqrt(2.0 / math.pi) * (x + 0.044715 * (x * x * x))))


def _s5_kernel(u_ref, zb_ref, ar_ref, ai_ref, bblk_ref, cblk_ref, dskip_ref, wglu_ref, bglu_ref,
               o_ref, xr_ref, xi_ref, sr_ref, si_ref, *, tt, nbatch, slabs_per_pass):
    t = pl.program_id(0)
    nchain = nbatch * S5_HALVES

    @pl.when(t == 0)
    def _():
        sr_ref[...] = jnp.zeros_like(sr_ref)
        si_ref[...] = jnp.zeros_like(si_ref)

    for hf in range(S5_HALVES):
        ub = jnp.concatenate([u_ref[b, :, hf * S5_HALF_CH:(hf + 1) * S5_HALF_CH] for b in range(nbatch)],
                             axis=0)
        bu = _dot(ub, bblk_ref[hf])
        for b in range(nbatch):
            kc = b * S5_HALVES + hf
            rows = slice(b * tt, (b + 1) * tt)
            for j in range(S5_SLABS):
                xr_ref[j, pl.ds(kc, tt, stride=nchain), :] = bu[rows, j * LANES:(j + 1) * LANES]
                xi_ref[j, pl.ds(kc, tt, stride=nchain), :] = bu[rows, (S5_SLABS + j) * LANES:
                                                                (S5_SLABS + j + 1) * LANES]

    for j0 in range(0, S5_SLABS, slabs_per_pass):
        js = list(range(j0, j0 + slabs_per_pass))
        ars = [ar_ref[j] for j in js]
        ais = [ai_ref[j] for j in js]

        def body(s, carry, js=js, ars=ars, ais=ais):
            r0 = pl.multiple_of(s * nchain, nchain)
            new = []
            for n, j in enumerate(js):
                sr, si = carry[2 * n], carry[2 * n + 1]
                br = xr_ref[j, pl.ds(r0, nchain), :]
                bi = xi_ref[j, pl.ds(r0, nchain), :]
                nr = (ars[n] * sr - ais[n] * si) + br
                ni = (ars[n] * si + ais[n] * sr) + bi
                xr_ref[j, pl.ds(r0, nchain), :] = nr
                xi_ref[j, pl.ds(r0, nchain), :] = ni
                new += [nr, ni]
            return tuple(new)

        init = []
        for j in js:
            init += [sr_ref[j], si_ref[j]]
        fin = lax.fori_loop(0, tt, body, tuple(init), unroll=4)
        for n, j in enumerate(js):
            sr_ref[j] = fin[2 * n]
            si_ref[j] = fin[2 * n + 1]

    halves = []
    for hf in range(S5_HALVES):
        xall = []
        for b in range(nbatch):
            kc = b * S5_HALVES + hf
            xs = [xr_ref[j, pl.ds(kc, tt, stride=nchain), :] for j in range(S5_SLABS)]
            xs += [xi_ref[j, pl.ds(kc, tt, stride=nchain), :] for j in range(S5_SLABS)]
            xall.append(jnp.concatenate(xs, axis=1).astype(_BF16))
        halves.append(_dot(jnp.concatenate(xall, axis=0), cblk_ref[hf]))
    w = u_ref.shape[-1]
    y = jnp.concatenate(halves, axis=1)
    y = y + dskip_ref[...] * u_ref[...].reshape(nbatch * tt, w).astype(_F32)
    y = _gelu_tanh(y)
    y = y * _sigmoid(_dot(y.astype(_BF16), wglu_ref[...]) + bglu_ref[...])
    y = y * zb_ref[...].reshape(nbatch * tt, w).astype(_F32)
    o_ref[...] = y.astype(_BF16).reshape(nbatch, tt, w)


def _s5_scan(u, zb, ar_t, ai_t, bblk, cblk, d_skip, w_glu, b_glu, tt=256, slabs_per_pass=4):
    bsz, s, w = u.shape
    nchain = bsz * S5_HALVES
    assert nchain == SUBLANES, "the scan packs batch x state-half chains into the 8 sublanes"
    tok_spec = pl.BlockSpec((bsz, tt, w), lambda t: (0, t, 0))
    const = lambda shape: pl.BlockSpec(shape, lambda t: (0,) * len(shape), pipeline_mode=pl.Buffered(1))
    return pl.pallas_call(
        functools.partial(_s5_kernel, tt=tt, nbatch=bsz, slabs_per_pass=slabs_per_pass),
        grid=(s // tt,),
        in_specs=[tok_spec, tok_spec, const(ar_t.shape), const(ai_t.shape),
                  const(bblk.shape), const(cblk.shape), const((1, w)), const(w_glu.shape), const((1, w))],
        out_specs=tok_spec,
        out_shape=jax.ShapeDtypeStruct((bsz, s, w), _BF16),
        scratch_shapes=[pltpu.VMEM((S5_SLABS, tt * nchain, LANES), _F32),
                        pltpu.VMEM((S5_SLABS, tt * nchain, LANES), _F32),
                        pltpu.VMEM((S5_SLABS, nchain, LANES), _F32),
                        pltpu.VMEM((S5_SLABS, nchain, LANES), _F32)],
        compiler_params=pltpu.CompilerParams(
            dimension_semantics=("arbitrary",), vmem_limit_bytes=VMEM_LIMIT),
        name="s5_scan",
    )(u, zb, ar_t, ai_t, bblk, cblk, d_skip.reshape(1, w), w_glu, b_glu.reshape(1, w))


def _out_kernel(x_ref, ya_ref, yb_ref, ga_ref, gb_ref, gate_ref, wua_ref, wub_ref, wo_ref, gf_ref, o_ref):
    ma = ga_ref[0].astype(_F32) * _dot(ya_ref[0], wua_ref[...])
    mb = gb_ref[0].astype(_F32) * _dot(yb_ref[0], wub_ref[...])
    merged = (ma + mb).astype(_BF16)
    xn = x_ref[0] + gate_ref[0] * _dot(merged, wo_ref[...])
    ms = jnp.mean(xn * xn, axis=-1, keepdims=True)
    o_ref[0] = xn * lax.rsqrt(ms + EPS) * gf_ref[...]


def _out_proj(x, ya, yb, ga, gb, gate, w_up_a, w_up_b, w_out, g_final, tm=1024):
    bsz, s, d = x.shape
    tok_spec = lambda width: pl.BlockSpec((1, tm, width), lambda b, i: (b, i, 0))
    const = lambda shape: pl.BlockSpec(shape, lambda b, i: (0,) * len(shape), pipeline_mode=pl.Buffered(1))
    return pl.pallas_call(
        _out_kernel,
        grid=(bsz, s // tm),
        in_specs=[tok_spec(d), tok_spec(FOX_WIDTH), tok_spec(S5_WIDTH), tok_spec(d), tok_spec(d),
                  pl.BlockSpec((1, 1, d), lambda b, i: (b, 0, 0)),
                  const(w_up_a.shape), const(w_up_b.shape), const(w_out.shape), const((1, d))],
        out_specs=tok_spec(d),
        out_shape=jax.ShapeDtypeStruct((bsz, s, d), x.dtype),
        compiler_params=pltpu.CompilerParams(
            dimension_semantics=("arbitrary", "arbitrary"), vmem_limit_bytes=VMEM_LIMIT),
        name="out_proj",
    )(x, ya, yb, ga, gb, gate, w_up_a, w_up_b, w_out, g_final.reshape(1, d))


def _block_diag_in(bb):
    gh = S5_GROUPS // S5_HALVES
    bb = bb.reshape(S5_HALVES, gh, S5_GROUP, S5_STATE)
    eye = jnp.eye(gh, dtype=bb.dtype)
    out = bb[:, :, :, None, :] * eye[None, :, None, :, None]
    return out.reshape(S5_HALVES, gh * S5_GROUP, gh * S5_STATE)


def _block_diag_out(cc):
    gh = S5_GROUPS // S5_HALVES
    cc = cc.reshape(S5_HALVES, gh, S5_GROUP, S5_STATE)
    eye = jnp.eye(gh, dtype=cc.dtype)
    out = jnp.swapaxes(cc, 2, 3)[:, :, :, None, :] * eye[None, :, None, :, None]
    return out.reshape(S5_HALVES, gh * S5_STATE, gh * S5_GROUP)


def _chain_tiles(a, nbatch):
    a = a.reshape(S5_HALVES, S5_SLABS, LANES)
    a = jnp.swapaxes(a, 0, 1)
    return jnp.tile(a, (1, nbatch, 1))


def kernel(x, c, w_ada, b_ada, g_norm, w_in, b_f, a_re, a_im, log_dt, b_re, b_im, c_re, c_im,
           d_skip, w_glu, b_glu, w_up_a, w_up_b, w_out, g_final):
    bsz, s, d = x.shape
    depth = w_ada.shape[0]
    assert depth == 1 and d == D_MODEL

    mod = _adaln_mod(c, w_ada[0], b_ada[0])
    shift = mod[:, 0:d].reshape(bsz, 1, d)
    scale = mod[:, d:2 * d].reshape(bsz, 1, d)
    gate = mod[:, 2 * d:3 * d].reshape(bsz, 1, d)

    w = w_in[0]
    o_f = 3 * FOX_WIDTH
    o_z = o_f + HEADS
    w_al = jnp.concatenate([w[:, 0:o_z], jnp.zeros((d, LANES - HEADS), w.dtype), w[:, o_z:]],
                           axis=1).astype(_BF16)
    wkt = w[:, FOX_WIDTH:2 * FOX_WIDTH].T.astype(_BF16)
    bf_pad = jnp.pad(b_f[0], (0, LANES - HEADS)).reshape(1, LANES)

    q, kt, v, qaug, kaugt, za, u, zb, ga, gb, stats = _in_proj(
        x, g_norm[0], scale, shift, w_al, wkt, bf_pad)

    base = _skip_plan(stats)
    by_head = lambda t: jnp.swapaxes(t[:, :, 0:HEADS], 1, 2).reshape(-1)
    ya = _fox_attention(by_head(base), by_head(stats[:, 2]), q, kt, kaugt, v, qaug, za)

    abr, abi, bbr, bbi = _s5_params(a_re[0], a_im[0], log_dt[0],
                                    jnp.swapaxes(b_re[0], 1, 2), jnp.swapaxes(b_im[0], 1, 2))
    bblk = jnp.concatenate([_block_diag_in(bbr), _block_diag_in(bbi)], axis=2).astype(_BF16)
    cblk = jnp.concatenate([_block_diag_out(c_re[0]), -_block_diag_out(c_im[0])], axis=1).astype(_BF16)
    yb = _s5_scan(u, zb, _chain_tiles(abr, bsz), _chain_tiles(abi, bsz), bblk, cblk,
                  d_skip[0], w_glu[0].astype(_BF16), b_glu[0])

    return _out_proj(x, ya, yb, ga, gb, gate, w_up_a[0].astype(_BF16), w_up_b[0].astype(_BF16),
                     w_out[0].astype(_BF16), g_final)
```

```python
import functools
import math

import jax
import jax.numpy as jnp
from jax import lax
from jax.experimental import pallas as pl
from jax.experimental.pallas import tpu as pltpu

D_MODEL = 1024
EPS = 1e-6
FOX_WIDTH = 512
HEAD_DIM = 64
HEADS = 8
HEAD_PAIRS = HEADS // 2
S5_WIDTH = 512
S5_GROUP = 16
S5_GROUPS = 32
S5_STATE = 64

LANES = 128
SUBLANES = 8
VMEM_LIMIT = 56 * 1024 * 1024

S5_HALVES = 2
S5_HALF_CH = S5_WIDTH // S5_HALVES
S5_HALF_STATE = S5_GROUPS * S5_STATE // S5_HALVES
S5_SLABS = S5_HALF_STATE // LANES

NEG_BIG = -1e30
LOG2E = math.log2(math.e)

ATTN_TILE = 512
SKIP_LOG2 = 40.0
NORM_SAFETY = 1.02

_F32 = jnp.float32
_BF16 = jnp.bfloat16


def _dot(a, b):
    return jnp.dot(a, b, preferred_element_type=_F32)


def _dot_nt(a, b):
    return lax.dot_general(a, b, (((1,), (1,)), ((), ())), preferred_element_type=_F32)


def _split3(x):
    hi = x.astype(_BF16).astype(_F32)
    r1 = x - hi
    mid = r1.astype(_BF16).astype(_F32)
    lo = (r1 - mid).astype(_BF16).astype(_F32)
    return hi, mid, lo


def _sigmoid(x):
    return 1.0 / (1.0 + jnp.exp(-x))


def _silu(x):
    return x * _sigmoid(x)


def _adaln_kernel(c_ref, w_ref, b_ref, o_ref):
    o_ref[...] = jnp.dot(c_ref[...], w_ref[...], precision=lax.Precision.HIGHEST,
                         preferred_element_type=_F32) + b_ref[...]


def _adaln_mod(c, w, b):
    bsz, d = c.shape
    n = w.shape[1]
    bn = 1024
    return pl.pallas_call(
        _adaln_kernel,
        grid=(n // bn,),
        in_specs=[pl.BlockSpec((bsz, d), lambda j: (0, 0)),
                  pl.BlockSpec((d, bn), lambda j: (0, j)),
                  pl.BlockSpec((1, bn), lambda j: (0, j))],
        out_specs=pl.BlockSpec((bsz, bn), lambda j: (0, j)),
        out_shape=jax.ShapeDtypeStruct((bsz, n), _F32),
        name="adaln_mod",
    )(c, w, b.reshape(1, n))


def _s5_param_kernel(are_ref, aim_ref, ldt_ref, bre_ref, bim_ref,
                     abr_ref, abi_ref, bbr_ref, bbi_ref):
    lr = are_ref[...]
    li = aim_ref[...]
    dt = jnp.exp(ldt_ref[...])
    mag = jnp.exp(lr * dt)
    ang = li * dt
    abr = mag * jnp.cos(ang)
    abi = mag * jnp.sin(ang)
    abr_ref[...] = abr
    abi_ref[...] = abi
    nr = abr - 1.0
    ni = abi
    den = lr * lr + li * li
    cr = (nr * lr + ni * li) / den
    ci = (ni * lr - nr * li) / den
    cr3 = cr[:, None, :]
    ci3 = ci[:, None, :]
    br = bre_ref[...]
    bi = bim_ref[...]
    bbr_ref[...] = cr3 * br - ci3 * bi
    bbi_ref[...] = cr3 * bi + ci3 * br


def _s5_params(a_re, a_im, log_dt, b_re_t, b_im_t):
    g, p = a_re.shape
    cg = b_re_t.shape[1]
    return pl.pallas_call(
        _s5_param_kernel,
        out_shape=(jax.ShapeDtypeStruct((g, p), _F32), jax.ShapeDtypeStruct((g, p), _F32),
                   jax.ShapeDtypeStruct((g, cg, p), _F32), jax.ShapeDtypeStruct((g, cg, p), _F32)),
        name="s5_params",
    )(a_re, a_im, log_dt.reshape(g, 1), b_re_t, b_im_t)


def _in_proj_kernel(x_ref, gn_ref, sc_ref, sh_ref, wt_ref, bf_ref,
                    q_ref, kt_ref, v_ref, qaug_ref, kaugt_ref, za_ref, u_ref, zb_ref, ga_ref, gb_ref,
                    stats_ref, carry_ref, kmax_ref, *, tm):
    i = pl.program_id(1)
    fw = FOX_WIDTH
    c_v, c_f = 2 * fw, 3 * fw
    c_za = c_f + LANES
    c_u, c_zb = c_za + fw, c_za + fw + S5_WIDTH
    c_ga = c_zb + S5_WIDTH
    c_gb = c_ga + D_MODEL

    @pl.when(i == 0)
    def _():
        carry_ref[...] = jnp.zeros_like(carry_ref)
        kmax_ref[...] = jnp.zeros_like(kmax_ref)

    x = x_ref[0]
    ms = jnp.mean(x * x, axis=-1, keepdims=True)
    h = x * lax.rsqrt(ms + EPS) * gn_ref[...]
    h = (h * (1.0 + sc_ref[0]) + sh_ref[0]).astype(_BF16)

    w = FOX_WIDTH
    qb = (_dot_nt(h, wt_ref[0:fw, :]) * (HEAD_DIM ** -0.5 * LOG2E)).astype(_BF16)
    q_ref[0] = qb
    ktb = _dot_nt(wt_ref[fw:c_v, :], h).astype(_BF16)
    kt_ref[0] = ktb
    v_ref[0] = _dot_nt(h, wt_ref[c_v:c_f, :]).astype(_BF16)

    seg = (lax.broadcasted_iota(jnp.int32, (w, LANES), 0) // HEAD_DIM
           == lax.broadcasted_iota(jnp.int32, (w, LANES), 1)).astype(_BF16)
    seg_t = (lax.broadcasted_iota(jnp.int32, (LANES, w), 1) // HEAD_DIM
             == lax.broadcasted_iota(jnp.int32, (LANES, w), 0)).astype(_BF16)
    qf = qb.astype(_F32)
    kf = ktb.astype(_F32)
    qn2 = _dot((qf * qf).astype(_BF16), seg)
    kn2 = _dot(seg_t, (kf * kf).astype(_BF16)).T
    kmax = jnp.maximum(kmax_ref[...], jnp.max(kn2, axis=0, keepdims=True))
    kmax_ref[...] = kmax
    za_ref[0] = _silu(_dot_nt(h, wt_ref[c_za:c_u, :])).astype(_BF16)
    u_ref[0] = _dot_nt(h, wt_ref[c_u:c_zb, :]).astype(_BF16)
    zb_ref[0] = _silu(_dot_nt(h, wt_ref[c_zb:c_ga, :])).astype(_BF16)
    ga_ref[0] = _sigmoid(_dot_nt(h, wt_ref[c_ga:c_gb, :])).astype(_BF16)
    gb_ref[0] = _sigmoid(_dot_nt(h, wt_ref[c_gb:c_gb + D_MODEL, :])).astype(_BF16)

    flt = _dot_nt(wt_ref[c_f:c_f + 2 * HEADS, :], h)[0:HEADS] + bf_ref[...]
    logft = -(jnp.maximum(-flt, 0.0) + jnp.log1p(jnp.exp(-jnp.abs(flt)))) * LOG2E
    lmh_t = jnp.concatenate(_split3(logft), axis=0).astype(_BF16)

    ta = ATTN_TILE
    row = lax.broadcasted_iota(jnp.int32, (ta, ta), 0)
    col = lax.broadcasted_iota(jnp.int32, (ta, ta), 1)
    triu = (row <= col).astype(_BF16)
    ones = jnp.ones((HEADS, ta), _F32)
    zeros = jnp.zeros((HEADS, ta), _F32)
    nt = tm // ta
    carry = carry_ref[...]
    for t in range(nt):
        r0 = t * ta
        parts = _dot(lmh_t[:, r0:r0 + ta], triu)
        cum_t = (parts[0:HEADS] + parts[HEADS:2 * HEADS]) + parts[2 * HEADS:3 * HEADS] + carry[:, 0:1]
        carry = jnp.broadcast_to(cum_t[:, ta - 1:ta], (HEADS, LANES))

        kh, km, kl = _split3(cum_t)
        kaugt_ref[0, :, r0:r0 + ta] = jnp.concatenate(
            [ones, ones, ones, zeros, kh, km, kl] + [zeros] * 9, axis=0).astype(_BF16)
        cum = jnp.concatenate([cum_t] + [zeros] * (LANES // HEADS - 1), axis=0).T
        qh, qm, ql = _split3(cum)
        qaug_ref[0, r0:r0 + ta, :] = (qh + pltpu.roll(qm, HEADS, 1)
                                      + pltpu.roll(ql, 2 * HEADS, 1)).astype(_BF16)

        tile = i * nt + t
        stats_ref[0, 0, pl.ds(tile, 1), :] = jnp.max(qn2[r0:r0 + ta], axis=0, keepdims=True)
        stats_ref[0, 1, pl.ds(tile, 1), :] = cum[0:1, :]
        stats_ref[0, 2, pl.ds(tile, 1), :] = cum[ta - 1:ta, :]
    carry_ref[...] = carry
    stats_ref[0, 3] = jnp.broadcast_to(kmax, stats_ref.shape[2:])


def _in_proj(x, g_norm, scale, shift, wt_al, b_f, tm=1024):
    bsz, s, d = x.shape
    ni = s // tm
    tok = lambda width, dt: jax.ShapeDtypeStruct((bsz, s, width), dt)
    tok_spec = lambda width: pl.BlockSpec((1, tm, width), lambda b, i: (b, i, 0))
    tr = lambda rows: jax.ShapeDtypeStruct((bsz, rows, s), _BF16)
    tr_spec = lambda rows: pl.BlockSpec((1, rows, tm), lambda b, i: (b, 0, i))
    const = lambda shape: pl.BlockSpec(shape, lambda b, i: (0,) * len(shape), pipeline_mode=pl.Buffered(1))
    per_b = pl.BlockSpec((1, 1, d), lambda b, i: (b, 0, 0))
    return pl.pallas_call(
        functools.partial(_in_proj_kernel, tm=tm),
        grid=(bsz, ni),
        in_specs=[tok_spec(d), const((1, d)), per_b, per_b,
                  const(wt_al.shape), const((HEADS, 1))],
        out_specs=[tok_spec(FOX_WIDTH), tr_spec(FOX_WIDTH), tok_spec(FOX_WIDTH),
                   tok_spec(LANES), tr_spec(LANES),
                   tok_spec(FOX_WIDTH), tok_spec(S5_WIDTH), tok_spec(S5_WIDTH),
                   tok_spec(d), tok_spec(d),
                   pl.BlockSpec((1, 4, s // ATTN_TILE, LANES), lambda b, i: (b, 0, 0, 0))],
        out_shape=[tok(FOX_WIDTH, _BF16), tr(FOX_WIDTH), tok(FOX_WIDTH, _BF16),
                   tok(LANES, _BF16), tr(LANES),
                   tok(FOX_WIDTH, _BF16), tok(S5_WIDTH, _BF16), tok(S5_WIDTH, _BF16),
                   tok(d, _BF16), tok(d, _BF16),
                   jax.ShapeDtypeStruct((bsz, 4, s // ATTN_TILE, LANES), _F32)],
        scratch_shapes=[pltpu.VMEM((HEADS, LANES), _F32), pltpu.VMEM((1, LANES), _F32)],
        compiler_params=pltpu.CompilerParams(
            dimension_semantics=("arbitrary", "arbitrary"), vmem_limit_bytes=VMEM_LIMIT),
        name="in_proj",
    )(x, g_norm.reshape(1, d), scale, shift, wt_al, b_f.reshape(HEADS, 1))


def _plan_kernel(st_ref, o_ref):
    qn2 = st_ref[0, 0]
    fstart = st_ref[0, 1]
    kmax2 = st_ref[0, 3]
    o_ref[0] = fstart + (NORM_SAFETY * jnp.sqrt(qn2 * kmax2) + SKIP_LOG2)


def _skip_plan(stats):
    bsz, _, nq, _ = stats.shape
    return pl.pallas_call(
        _plan_kernel,
        grid=(bsz,),
        in_specs=[pl.BlockSpec((1, 4, nq, LANES), lambda b: (b, 0, 0, 0))],
        out_specs=pl.BlockSpec((1, nq, LANES), lambda b: (b, 0, 0)),
        out_shape=jax.ShapeDtypeStruct((bsz, nq, LANES), _F32),
        name="skip_plan",
    )(stats)


def _attn_kernel(base_ref, fend_ref, q_ref, kt_ref, kaugt_ref, v_ref, qaug_ref, za_ref, o_ref,
                 lhs_ref, s_ref, p_ref, m_ref, alpha_ref, acc_ref, *, tq, tk, rb, nq):
    b = pl.program_id(0)
    hp = pl.program_id(1)
    i = pl.program_id(2)
    lane = lax.broadcasted_iota(jnp.int32, (1, LANES), 1)
    q = q_ref[0]
    qa = qaug_ref[0]

    head_lanes = [(lane >= hh * HEAD_DIM) & (lane < (hh + 1) * HEAD_DIM) for hh in range(2)]
    for hh in range(2):
        h = 2 * hp + hh
        f_lanes = (lane == h) | (lane == HEADS + h) | (lane == 2 * HEADS + h)
        neg_lanes = (lane == 4 * HEADS + h) | (lane == 5 * HEADS + h) | (lane == 6 * HEADS + h)
        qm = jnp.where(head_lanes[hh], q, jnp.zeros_like(q))
        qaug = jnp.where(f_lanes, qa, jnp.where(neg_lanes, -1.0, 0.0).astype(_BF16))
        lhs_ref[hh] = jnp.concatenate([qm, qaug], axis=1)
        m_ref[hh] = jnp.full((tq, LANES), NEG_BIG, _F32)
        acc_ref[hh] = jnp.zeros((tq, LANES), _F32)

    def step(j, masked):
        k0 = pl.multiple_of(j * tk, tk)
        w = jnp.concatenate([kt_ref[0, :, pl.ds(k0, tk)], kaugt_ref[0, :, pl.ds(k0, tk)]], axis=0)
        vb = v_ref[0, pl.ds(k0, tk), :]
        for hh in range(2):
            s = _dot(lhs_ref[hh], w)
            if masked:
                r = lax.broadcasted_iota(jnp.int32, (tq, tk), 0)
                c = lax.broadcasted_iota(jnp.int32, (tq, tk), 1)
                s = jnp.where(c <= r, s, NEG_BIG)
            s_ref[hh] = s
        for hh in range(2):
            for r0 in range(0, tq, rb):
                chunks = [s_ref[hh, r0:r0 + rb, c0:c0 + LANES] for c0 in range(0, tk, LANES)]
                m_old = m_ref[hh, r0:r0 + rb, :]
                m_new = jnp.maximum(m_old, jnp.max(functools.reduce(jnp.maximum, chunks), axis=1, keepdims=True))
                alpha_ref[hh, r0:r0 + rb, :] = jnp.exp2(m_old - m_new)
                m_ref[hh, r0:r0 + rb, :] = m_new
                for c0, ch in zip(range(0, tk, LANES), chunks):
                    p_ref[hh, r0:r0 + rb, c0:c0 + LANES] = jnp.exp2(ch - m_new).astype(_BF16)
        for hh in range(2):
            sum_lane = (1 - hh) * HEAD_DIM
            vaug = jnp.where(head_lanes[hh], vb, jnp.where(lane == sum_lane, 1.0, 0.0).astype(_BF16))
            acc_ref[hh] = alpha_ref[hh] * acc_ref[hh] + _dot(p_ref[hh], vaug)

    step(i, True)

    def first_needed(hh):
        row = (b * HEADS + 2 * hp + hh) * nq
        thr = base_ref[row + i] - jnp.min(m_ref[hh])
        count = jnp.int32(0)
        for jj in range(nq):
            count += ((fend_ref[row + jj] >= thr) & (jj < i)).astype(jnp.int32)
        return count

    j_first = jnp.minimum(first_needed(0), first_needed(1))

    n_old = i - j_first
    odd = n_old & 1

    @pl.when(odd == 1)
    def _():
        step(j_first, False)

    j_even = j_first + odd

    def pair_step(t, carry):
        step(j_even + 2 * t, False)
        step(j_even + 2 * t + 1, False)
        return carry

    lax.fori_loop(0, n_old >> 1, pair_step, 0)

    a0 = acc_ref[0]
    a1 = acc_ref[1]
    l0 = a0[:, HEAD_DIM:HEAD_DIM + 1]
    l1 = a1[:, 0:1]
    y = jnp.where(lane < HEAD_DIM, a0 / l0, a1 / l1)
    o_ref[0] = (y * za_ref[0].astype(_F32)).astype(_BF16)


def _fox_attention(base, fend, q, kt, kaugt, v, qaug, za, rb=32):
    bsz, s, _ = q.shape
    tq = tk = ATTN_TILE
    nq = s // tq
    qspec = pl.BlockSpec((1, tq, LANES), lambda b, hp, i, *_: (b, i, hp))
    grid_spec = pltpu.PrefetchScalarGridSpec(
        num_scalar_prefetch=2,
        grid=(bsz, HEAD_PAIRS, nq),
        in_specs=[qspec,
                  pl.BlockSpec((1, LANES, s), lambda b, hp, i, *_: (b, hp, 0)),
                  pl.BlockSpec((1, LANES, s), lambda b, hp, i, *_: (b, 0, 0)),
                  pl.BlockSpec((1, s, LANES), lambda b, hp, i, *_: (b, 0, hp)),
                  pl.BlockSpec((1, tq, LANES), lambda b, hp, i, *_: (b, i, 0)),
                  qspec],
        out_specs=qspec,
        scratch_shapes=[pltpu.VMEM((2, tq, 2 * LANES), _BF16),
                        pltpu.VMEM((2, tq, tk), _F32), pltpu.VMEM((2, tq, tk), _BF16),
                        pltpu.VMEM((2, tq, LANES), _F32), pltpu.VMEM((2, tq, LANES), _F32),
                        pltpu.VMEM((2, tq, LANES), _F32)])
    return pl.pallas_call(
        functools.partial(_attn_kernel, tq=tq, tk=tk, rb=rb, nq=nq),
        grid_spec=grid_spec,
        out_shape=jax.ShapeDtypeStruct((bsz, s, FOX_WIDTH), _BF16),
        compiler_params=pltpu.CompilerParams(
            dimension_semantics=("arbitrary", "arbitrary", "arbitrary"), vmem_limit_bytes=VMEM_LIMIT),
        name="fox_attn",
    )(base, fend, q, kt, kaugt, v, qaug, za)


def _gelu_tanh(x):
    return 0.5 * x * (1.0 + jnp.tanh(math.sqrt(2.0 / math.pi) * (x + 0.044715 * (x * x * x))))


def _s5_kernel(u_ref, zb_ref, ar_ref, ai_ref, bblk_ref, cblk_ref, dskip_ref, wglu_ref, bglu_ref,
               o_ref, xr_ref, xi_ref, sr_ref, si_ref, *, tt, nbatch, slabs_per_pass):
    t = pl.program_id(0)
    nchain = nbatch * S5_HALVES

    @pl.when(t == 0)
    def _():
        sr_ref[...] = jnp.zeros_like(sr_ref)
        si_ref[...] = jnp.zeros_like(si_ref)

    for hf in range(S5_HALVES):
        ub = jnp.concatenate([u_ref[b, :, hf * S5_HALF_CH:(hf + 1) * S5_HALF_CH] for b in range(nbatch)],
                             axis=0)
        bu = _dot(ub, bblk_ref[hf])
        for b in range(nbatch):
            kc = b * S5_HALVES + hf
            rows = slice(b * tt, (b + 1) * tt)
            for j in range(S5_SLABS):
                xr_ref[j, pl.ds(kc, tt, stride=nchain), :] = bu[rows, j * LANES:(j + 1) * LANES]
                xi_ref[j, pl.ds(kc, tt, stride=nchain), :] = bu[rows, (S5_SLABS + j) * LANES:
                                                                (S5_SLABS + j + 1) * LANES]

    for j0 in range(0, S5_SLABS, slabs_per_pass):
        js = list(range(j0, j0 + slabs_per_pass))
        ars = [ar_ref[j] for j in js]
        ais = [ai_ref[j] for j in js]

        def body(s, carry, js=js, ars=ars, ais=ais):
            r0 = pl.multiple_of(s * nchain, nchain)
            new = []
            for n, j in enumerate(js):
                sr, si = carry[2 * n], carry[2 * n + 1]
                br = xr_ref[j, pl.ds(r0, nchain), :]
                bi = xi_ref[j, pl.ds(r0, nchain), :]
                nr = (ars[n] * sr - ais[n] * si) + br
                ni = (ars[n] * si + ais[n] * sr) + bi
                xr_ref[j, pl.ds(r0, nchain), :] = nr
                xi_ref[j, pl.ds(r0, nchain), :] = ni
                new += [nr, ni]
            return tuple(new)

        init = []
        for j in js:
            init += [sr_ref[j], si_ref[j]]
        fin = lax.fori_loop(0, tt, body, tuple(init), unroll=4)
        for n, j in enumerate(js):
            sr_ref[j] = fin[2 * n]
            si_ref[j] = fin[2 * n + 1]

    halves = []
    for hf in range(S5_HALVES):
        xall = []
        for b in range(nbatch):
            kc = b * S5_HALVES + hf
            xs = [xr_ref[j, pl.ds(kc, tt, stride=nchain), :] for j in range(S5_SLABS)]
            xs += [xi_ref[j, pl.ds(kc, tt, stride=nchain), :] for j in range(S5_SLABS)]
            xall.append(jnp.concatenate(xs, axis=1).astype(_BF16))
        halves.append(_dot(jnp.concatenate(xall, axis=0), cblk_ref[hf]))
    w = u_ref.shape[-1]
    y = jnp.concatenate(halves, axis=1)
    y = y + dskip_ref[...] * u_ref[...].reshape(nbatch * tt, w).astype(_F32)
    y = _gelu_tanh(y)
    y = y * _sigmoid(_dot(y.astype(_BF16), wglu_ref[...]) + bglu_ref[...])
    y = y * zb_ref[...].reshape(nbatch * tt, w).astype(_F32)
    o_ref[...] = y.astype(_BF16).reshape(nbatch, tt, w)


def _s5_scan(u, zb, ar_t, ai_t, bblk, cblk, d_skip, w_glu, b_glu, tt=256, slabs_per_pass=8):
    bsz, s, w = u.shape
    nchain = bsz * S5_HALVES
    assert nchain == SUBLANES, "the scan packs batch x state-half chains into the 8 sublanes"
    tok_spec = pl.BlockSpec((bsz, tt, w), lambda t: (0, t, 0))
    const = lambda shape: pl.BlockSpec(shape, lambda t: (0,) * len(shape), pipeline_mode=pl.Buffered(1))
    return pl.pallas_call(
        functools.partial(_s5_kernel, tt=tt, nbatch=bsz, slabs_per_pass=slabs_per_pass),
        grid=(s // tt,),
        in_specs=[tok_spec, tok_spec, const(ar_t.shape), const(ai_t.shape),
                  const(bblk.shape), const(cblk.shape), const((1, w)), const(w_glu.shape), const((1, w))],
        out_specs=tok_spec,
        out_shape=jax.ShapeDtypeStruct((bsz, s, w), _BF16),
        scratch_shapes=[pltpu.VMEM((S5_SLABS, tt * nchain, LANES), _F32),
                        pltpu.VMEM((S5_SLABS, tt * nchain, LANES), _F32),
                        pltpu.VMEM((S5_SLABS, nchain, LANES), _F32),
                        pltpu.VMEM((S5_SLABS, nchain, LANES), _F32)],
        compiler_params=pltpu.CompilerParams(
            dimension_semantics=("arbitrary",), vmem_limit_bytes=VMEM_LIMIT),
        name="s5_scan",
    )(u, zb, ar_t, ai_t, bblk, cblk, d_skip.reshape(1, w), w_glu, b_glu.reshape(1, w))


def _out_kernel(x_ref, ya_ref, yb_ref, ga_ref, gb_ref, gate_ref, wua_ref, wub_ref, wo_ref, gf_ref, o_ref):
    ma = ga_ref[0].astype(_F32) * _dot(ya_ref[0], wua_ref[...])
    mb = gb_ref[0].astype(_F32) * _dot(yb_ref[0], wub_ref[...])
    merged = (ma + mb).astype(_BF16)
    xn = x_ref[0] + gate_ref[0] * _dot(merged, wo_ref[...])
    ms = jnp.mean(xn * xn, axis=-1, keepdims=True)
    o_ref[0] = xn * lax.rsqrt(ms + EPS) * gf_ref[...]


def _out_proj(x, ya, yb, ga, gb, gate, w_up_a, w_up_b, w_out, g_final, tm=1024):
    bsz, s, d = x.shape
    tok_spec = lambda width: pl.BlockSpec((1, tm, width), lambda b, i: (b, i, 0))
    const = lambda shape: pl.BlockSpec(shape, lambda b, i: (0,) * len(shape), pipeline_mode=pl.Buffered(1))
    return pl.pallas_call(
        _out_kernel,
        grid=(bsz, s // tm),
        in_specs=[tok_spec(d), tok_spec(FOX_WIDTH), tok_spec(S5_WIDTH), tok_spec(d), tok_spec(d),
                  pl.BlockSpec((1, 1, d), lambda b, i: (b, 0, 0)),
                  const(w_up_a.shape), const(w_up_b.shape), const(w_out.shape), const((1, d))],
        out_specs=tok_spec(d),
        out_shape=jax.ShapeDtypeStruct((bsz, s, d), x.dtype),
        compiler_params=pltpu.CompilerParams(
            dimension_semantics=("arbitrary", "arbitrary"), vmem_limit_bytes=VMEM_LIMIT),
        name="out_proj",
    )(x, ya, yb, ga, gb, gate, w_up_a, w_up_b, w_out, g_final.reshape(1, d))


def _block_diag_in(bb):
    gh = S5_GROUPS // S5_HALVES
    bb = bb.reshape(S5_HALVES, gh, S5_GROUP, S5_STATE)
    eye = jnp.eye(gh, dtype=bb.dtype)
    out = bb[:, :, :, None, :] * eye[None, :, None, :, None]
    return out.reshape(S5_HALVES, gh * S5_GROUP, gh * S5_STATE)


def _block_diag_out(cc):
    gh = S5_GROUPS // S5_HALVES
    cc = cc.reshape(S5_HALVES, gh, S5_GROUP, S5_STATE)
    eye = jnp.eye(gh, dtype=cc.dtype)
    out = jnp.swapaxes(cc, 2, 3)[:, :, :, None, :] * eye[None, :, None, :, None]
    return out.reshape(S5_HALVES, gh * S5_STATE, gh * S5_GROUP)


def _chain_tiles(a, nbatch):
    a = a.reshape(S5_HALVES, S5_SLABS, LANES)
    a = jnp.swapaxes(a, 0, 1)
    return jnp.tile(a, (1, nbatch, 1))


def kernel(x, c, w_ada, b_ada, g_norm, w_in, b_f, a_re, a_im, log_dt, b_re, b_im, c_re, c_im,
           d_skip, w_glu, b_glu, w_up_a, w_up_b, w_out, g_final):
    bsz, s, d = x.shape
    depth = w_ada.shape[0]
    assert depth == 1 and d == D_MODEL

    mod = _adaln_mod(c, w_ada[0], b_ada[0])
    shift = mod[:, 0:d].reshape(bsz, 1, d)
    scale = mod[:, d:2 * d].reshape(bsz, 1, d)
    gate = mod[:, 2 * d:3 * d].reshape(bsz, 1, d)

    wt = jnp.swapaxes(w_in[0], 0, 1)
    o_z = 3 * FOX_WIDTH + HEADS
    wt_al = jnp.concatenate([wt[0:o_z], jnp.zeros((LANES - HEADS, d), wt.dtype), wt[o_z:]],
                            axis=0).astype(_BF16)

    q, kt, v, qaug, kaugt, za, u, zb, ga, gb, stats = _in_proj(
        x, g_norm[0], scale, shift, wt_al, b_f[0])

    base = _skip_plan(stats)
    by_head = lambda t: jnp.swapaxes(t[:, :, 0:HEADS], 1, 2).reshape(-1)
    ya = _fox_attention(by_head(base), by_head(stats[:, 2]), q, kt, kaugt, v, qaug, za)

    abr, abi, bbr, bbi = _s5_params(a_re[0], a_im[0], log_dt[0],
                                    jnp.swapaxes(b_re[0], 1, 2), jnp.swapaxes(b_im[0], 1, 2))
    bblk = jnp.concatenate([_block_diag_in(bbr), _block_diag_in(bbi)], axis=2).astype(_BF16)
    cblk = jnp.concatenate([_block_diag_out(c_re[0]), -_block_diag_out(c_im[0])], axis=1).astype(_BF16)
    yb = _s5_scan(u, zb, _chain_tiles(abr, bsz), _chain_tiles(abi, bsz), bblk, cblk,
                  d_skip[0], w_glu[0].astype(_BF16), b_glu[0])

    return _out_proj(x, ya, yb, ga, gb, gate, w_up_a[0].astype(_BF16), w_up_b[0].astype(_BF16),
                     w_out[0].astype(_BF16), g_final)
```

```python
import functools
import math

import jax
import jax.numpy as jnp
from jax import lax
from jax.experimental import pallas as pl
from jax.experimental.pallas import tpu as pltpu

D_MODEL = 1024
EPS = 1e-6
FOX_WIDTH = 512
HEAD_DIM = 64
HEADS = 8
HEAD_PAIRS = HEADS // 2
S5_WIDTH = 512
S5_GROUP = 16
S5_GROUPS = 32
S5_STATE = 64

LANES = 128
SUBLANES = 8
VMEM_LIMIT = 56 * 1024 * 1024

S5_HALVES = 2
S5_HALF_CH = S5_WIDTH // S5_HALVES
S5_HALF_STATE = S5_GROUPS * S5_STATE // S5_HALVES
S5_SLABS = S5_HALF_STATE // LANES

NEG_BIG = -1e30
LOG2E = math.log2(math.e)

ATTN_TILE = 512
ATTN_KBLOCK = 512
KB_PER_TILE = ATTN_TILE // ATTN_KBLOCK
ST_QN2, ST_FSTART, ST_KMAX, ST_FEND = 0, 1, 2, 3
ST_PLANES = ST_FEND + KB_PER_TILE
SKIP_LOG2 = 40.0
NORM_SAFETY = 1.02

_F32 = jnp.float32
_BF16 = jnp.bfloat16


def _dot(a, b):
    return jnp.dot(a, b, preferred_element_type=_F32)


def _dot_nt(a, b):
    return lax.dot_general(a, b, (((1,), (1,)), ((), ())), preferred_element_type=_F32)


def _split3(x):
    hi = x.astype(_BF16).astype(_F32)
    r1 = x - hi
    mid = r1.astype(_BF16).astype(_F32)
    lo = (r1 - mid).astype(_BF16).astype(_F32)
    return hi, mid, lo


def _sigmoid(x):
    return 1.0 / (1.0 + jnp.exp(-x))


def _silu(x):
    return x * _sigmoid(x)


def _adaln_kernel(c_ref, w_ref, b_ref, o_ref):
    o_ref[...] = jnp.dot(c_ref[...], w_ref[...], precision=lax.Precision.HIGHEST,
                         preferred_element_type=_F32) + b_ref[...]


def _adaln_mod(c, w, b):
    bsz, d = c.shape
    n = w.shape[1]
    bn = 1024
    return pl.pallas_call(
        _adaln_kernel,
        grid=(n // bn,),
        in_specs=[pl.BlockSpec((bsz, d), lambda j: (0, 0)),
                  pl.BlockSpec((d, bn), lambda j: (0, j)),
                  pl.BlockSpec((1, bn), lambda j: (0, j))],
        out_specs=pl.BlockSpec((bsz, bn), lambda j: (0, j)),
        out_shape=jax.ShapeDtypeStruct((bsz, n), _F32),
        name="adaln_mod",
    )(c, w, b.reshape(1, n))


def _s5_param_kernel(are_ref, aim_ref, ldt_ref, bre_ref, bim_ref,
                     abr_ref, abi_ref, bbr_ref, bbi_ref):
    lr = are_ref[...]
    li = aim_ref[...]
    dt = jnp.exp(ldt_ref[...])
    mag = jnp.exp(lr * dt)
    ang = li * dt
    abr = mag * jnp.cos(ang)
    abi = mag * jnp.sin(ang)
    abr_ref[...] = abr
    abi_ref[...] = abi
    nr = abr - 1.0
    ni = abi
    den = lr * lr + li * li
    cr = (nr * lr + ni * li) / den
    ci = (ni * lr - nr * li) / den
    cr3 = cr[:, None, :]
    ci3 = ci[:, None, :]
    br = bre_ref[...]
    bi = bim_ref[...]
    bbr_ref[...] = cr3 * br - ci3 * bi
    bbi_ref[...] = cr3 * bi + ci3 * br


def _s5_params(a_re, a_im, log_dt, b_re_t, b_im_t):
    g, p = a_re.shape
    cg = b_re_t.shape[1]
    return pl.pallas_call(
        _s5_param_kernel,
        out_shape=(jax.ShapeDtypeStruct((g, p), _F32), jax.ShapeDtypeStruct((g, p), _F32),
                   jax.ShapeDtypeStruct((g, cg, p), _F32), jax.ShapeDtypeStruct((g, cg, p), _F32)),
        name="s5_params",
    )(a_re, a_im, log_dt.reshape(g, 1), b_re_t, b_im_t)


def _in_proj_kernel(x_ref, gn_ref, sc_ref, sh_ref, wt_ref, bf_ref,
                    q_ref, kt_ref, v_ref, qaug_ref, kaugt_ref, za_ref, u_ref, zb_ref, ga_ref, gb_ref,
                    stats_ref, carry_ref, kmax_ref, *, tm):
    i = pl.program_id(1)
    fw = FOX_WIDTH
    c_v, c_f = 2 * fw, 3 * fw
    c_za = c_f + LANES
    c_u, c_zb = c_za + fw, c_za + fw + S5_WIDTH
    c_ga = c_zb + S5_WIDTH
    c_gb = c_ga + D_MODEL

    @pl.when(i == 0)
    def _():
        carry_ref[...] = jnp.zeros_like(carry_ref)
        kmax_ref[...] = jnp.zeros_like(kmax_ref)

    x = x_ref[0]
    ms = jnp.mean(x * x, axis=-1, keepdims=True)
    h = x * lax.rsqrt(ms + EPS) * gn_ref[...]
    h = (h * (1.0 + sc_ref[0]) + sh_ref[0]).astype(_BF16)

    w = FOX_WIDTH
    qb = (_dot_nt(h, wt_ref[0:fw, :]) * (HEAD_DIM ** -0.5 * LOG2E)).astype(_BF16)
    q_ref[0] = qb
    ktb = _dot_nt(wt_ref[fw:c_v, :], h).astype(_BF16)
    kt_ref[0] = ktb
    v_ref[0] = _dot_nt(h, wt_ref[c_v:c_f, :]).astype(_BF16)

    seg = (lax.broadcasted_iota(jnp.int32, (w, LANES), 0) // HEAD_DIM
           == lax.broadcasted_iota(jnp.int32, (w, LANES), 1)).astype(_BF16)
    seg_t = (lax.broadcasted_iota(jnp.int32, (LANES, w), 1) // HEAD_DIM
             == lax.broadcasted_iota(jnp.int32, (LANES, w), 0)).astype(_BF16)
    qf = qb.astype(_F32)
    kf = ktb.astype(_F32)
    qn2 = _dot((qf * qf).astype(_BF16), seg)
    kn2 = _dot(seg_t, (kf * kf).astype(_BF16)).T
    kmax = jnp.maximum(kmax_ref[...], jnp.max(kn2, axis=0, keepdims=True))
    kmax_ref[...] = kmax
    za_ref[0] = _silu(_dot_nt(h, wt_ref[c_za:c_u, :])).astype(_BF16)
    u_ref[0] = _dot_nt(h, wt_ref[c_u:c_zb, :]).astype(_BF16)
    zb_ref[0] = _silu(_dot_nt(h, wt_ref[c_zb:c_ga, :])).astype(_BF16)
    ga_ref[0] = _sigmoid(_dot_nt(h, wt_ref[c_ga:c_gb, :])).astype(_BF16)
    gb_ref[0] = _sigmoid(_dot_nt(h, wt_ref[c_gb:c_gb + D_MODEL, :])).astype(_BF16)

    flt = _dot_nt(wt_ref[c_f:c_f + 2 * HEADS, :], h)[0:HEADS] + bf_ref[...]
    logft = -(jnp.maximum(-flt, 0.0) + jnp.log1p(jnp.exp(-jnp.abs(flt)))) * LOG2E
    lmh_t = jnp.concatenate(_split3(logft), axis=0).astype(_BF16)

    ta = ATTN_TILE
    row = lax.broadcasted_iota(jnp.int32, (ta, ta), 0)
    col = lax.broadcasted_iota(jnp.int32, (ta, ta), 1)
    triu = (row <= col).astype(_BF16)
    ones = jnp.ones((HEADS, ta), _F32)
    zeros = jnp.zeros((HEADS, ta), _F32)
    nt = tm // ta
    carry = carry_ref[...]
    for t in range(nt):
        r0 = t * ta
        parts = _dot(lmh_t[:, r0:r0 + ta], triu)
        cum_t = (parts[0:HEADS] + parts[HEADS:2 * HEADS]) + parts[2 * HEADS:3 * HEADS] + carry[:, 0:1]
        carry = jnp.broadcast_to(cum_t[:, ta - 1:ta], (HEADS, LANES))

        kh, km, kl = _split3(cum_t)
        kaugt_ref[0, :, r0:r0 + ta] = jnp.concatenate(
            [ones, ones, ones, zeros, kh, km, kl] + [zeros] * 9, axis=0).astype(_BF16)
        cum = jnp.concatenate([cum_t] + [zeros] * (LANES // HEADS - 1), axis=0).T
        qh, qm, ql = _split3(cum)
        qaug_ref[0, r0:r0 + ta, :] = (qh + pltpu.roll(qm, HEADS, 1)
                                      + pltpu.roll(ql, 2 * HEADS, 1)).astype(_BF16)

        tile = i * nt + t
        stats_ref[0, ST_QN2, pl.ds(tile, 1), :] = jnp.max(qn2[r0:r0 + ta], axis=0, keepdims=True)
        stats_ref[0, ST_FSTART, pl.ds(tile, 1), :] = cum[0:1, :]
        for c in range(KB_PER_TILE):
            last = (c + 1) * ATTN_KBLOCK - 1
            stats_ref[0, ST_FEND + c, pl.ds(tile, 1), :] = cum[last:last + 1, :]
    carry_ref[...] = carry
    stats_ref[0, ST_KMAX] = jnp.broadcast_to(kmax, stats_ref.shape[2:])


def _in_proj(x, g_norm, scale, shift, wt_al, b_f, tm=1024):
    bsz, s, d = x.shape
    ni = s // tm
    tok = lambda width, dt: jax.ShapeDtypeStruct((bsz, s, width), dt)
    tok_spec = lambda width: pl.BlockSpec((1, tm, width), lambda b, i: (b, i, 0))
    tr = lambda rows: jax.ShapeDtypeStruct((bsz, rows, s), _BF16)
    tr_spec = lambda rows: pl.BlockSpec((1, rows, tm), lambda b, i: (b, 0, i))
    const = lambda shape: pl.BlockSpec(shape, lambda b, i: (0,) * len(shape), pipeline_mode=pl.Buffered(1))
    per_b = pl.BlockSpec((1, 1, d), lambda b, i: (b, 0, 0))
    return pl.pallas_call(
        functools.partial(_in_proj_kernel, tm=tm),
        grid=(bsz, ni),
        in_specs=[tok_spec(d), const((1, d)), per_b, per_b,
                  const(wt_al.shape), const((HEADS, 1))],
        out_specs=[tok_spec(FOX_WIDTH), tr_spec(FOX_WIDTH), tok_spec(FOX_WIDTH),
                   tok_spec(LANES), tr_spec(LANES),
                   tok_spec(FOX_WIDTH), tok_spec(S5_WIDTH), tok_spec(S5_WIDTH),
                   tok_spec(d), tok_spec(d),
                   pl.BlockSpec((1, ST_PLANES, s // ATTN_TILE, LANES), lambda b, i: (b, 0, 0, 0))],
        out_shape=[tok(FOX_WIDTH, _BF16), tr(FOX_WIDTH), tok(FOX_WIDTH, _BF16),
                   tok(LANES, _BF16), tr(LANES),
                   tok(FOX_WIDTH, _BF16), tok(S5_WIDTH, _BF16), tok(S5_WIDTH, _BF16),
                   tok(d, _BF16), tok(d, _BF16),
                   jax.ShapeDtypeStruct((bsz, ST_PLANES, s // ATTN_TILE, LANES), _F32)],
        scratch_shapes=[pltpu.VMEM((HEADS, LANES), _F32), pltpu.VMEM((1, LANES), _F32)],
        compiler_params=pltpu.CompilerParams(
            dimension_semantics=("arbitrary", "arbitrary"), vmem_limit_bytes=VMEM_LIMIT),
        name="in_proj",
    )(x, g_norm.reshape(1, d), scale, shift, wt_al, b_f.reshape(HEADS, 1))


def _plan_kernel(st_ref, o_ref):
    qn2 = st_ref[0, ST_QN2]
    fstart = st_ref[0, ST_FSTART]
    kmax2 = st_ref[0, ST_KMAX]
    o_ref[0] = fstart + (NORM_SAFETY * jnp.sqrt(qn2 * kmax2) + SKIP_LOG2)


def _skip_plan(stats):
    bsz, _, nq, _ = stats.shape
    return pl.pallas_call(
        _plan_kernel,
        grid=(bsz,),
        in_specs=[pl.BlockSpec((1, ST_PLANES, nq, LANES), lambda b: (b, 0, 0, 0))],
        out_specs=pl.BlockSpec((1, nq, LANES), lambda b: (b, 0, 0)),
        out_shape=jax.ShapeDtypeStruct((bsz, nq, LANES), _F32),
        name="skip_plan",
    )(stats)


def _attn_kernel(base_ref, fend_ref, q_ref, kt_ref, kaugt_ref, v_ref, qaug_ref, za_ref, o_ref,
                 lhs_ref, s_ref, p_ref, m_ref, alpha_ref, acc_ref, *, tq, tk, rb, nq):
    b = pl.program_id(0)
    hp = pl.program_id(1)
    i = pl.program_id(2)
    lane = lax.broadcasted_iota(jnp.int32, (1, LANES), 1)
    q = q_ref[0]
    qa = qaug_ref[0]

    head_lanes = [(lane >= hh * HEAD_DIM) & (lane < (hh + 1) * HEAD_DIM) for hh in range(2)]
    for hh in range(2):
        h = 2 * hp + hh
        f_lanes = (lane == h) | (lane == HEADS + h) | (lane == 2 * HEADS + h)
        neg_lanes = (lane == 4 * HEADS + h) | (lane == 5 * HEADS + h) | (lane == 6 * HEADS + h)
        qm = jnp.where(head_lanes[hh], q, jnp.zeros_like(q))
        qaug = jnp.where(f_lanes, qa, jnp.where(neg_lanes, -1.0, 0.0).astype(_BF16))
        lhs_ref[hh] = jnp.concatenate([qm, qaug], axis=1)
        m_ref[hh] = jnp.full((tq, LANES), NEG_BIG, _F32)
        acc_ref[hh] = jnp.zeros((tq, LANES), _F32)

    def step(j, masked):
        k0 = pl.multiple_of(j * tk, tk)
        w = jnp.concatenate([kt_ref[0, :, pl.ds(k0, tk)], kaugt_ref[0, :, pl.ds(k0, tk)]], axis=0)
        vb = v_ref[0, pl.ds(k0, tk), :]
        for hh in range(2):
            s = _dot(lhs_ref[hh], w)
            if masked:
                r = lax.broadcasted_iota(jnp.int32, (tq, tk), 0)
                c = lax.broadcasted_iota(jnp.int32, (tq, tk), 1)
                s = jnp.where(c - r <= i * tq - j * tk, s, NEG_BIG)
            s_ref[hh] = s
        for hh in range(2):
            for r0 in range(0, tq, rb):
                chunks = [s_ref[hh, r0:r0 + rb, c0:c0 + LANES] for c0 in range(0, tk, LANES)]
                m_old = m_ref[hh, r0:r0 + rb, :]
                m_new = jnp.maximum(m_old, jnp.max(functools.reduce(jnp.maximum, chunks), axis=1, keepdims=True))
                alpha_ref[hh, r0:r0 + rb, :] = jnp.exp2(m_old - m_new)
                m_ref[hh, r0:r0 + rb, :] = m_new
                for c0, ch in zip(range(0, tk, LANES), chunks):
                    p_ref[hh, r0:r0 + rb, c0:c0 + LANES] = jnp.exp2(ch - m_new).astype(_BF16)
        for hh in range(2):
            sum_lane = (1 - hh) * HEAD_DIM
            vaug = jnp.where(head_lanes[hh], vb, jnp.where(lane == sum_lane, 1.0, 0.0).astype(_BF16))
            acc_ref[hh] = alpha_ref[hh] * acc_ref[hh] + _dot(p_ref[hh], vaug)

    kpt = tq // tk
    j_diag = i * kpt
    for c in range(kpt):
        step(j_diag + c, True)

    def first_needed(hh):
        thr = base_ref[(b * HEADS + 2 * hp + hh) * nq + i] - jnp.min(m_ref[hh])
        row = (b * HEADS + 2 * hp + hh) * (nq * kpt)
        count = jnp.int32(0)
        for jj in range(nq * kpt):
            count += ((fend_ref[row + jj] >= thr) & (jj < j_diag)).astype(jnp.int32)
        return count

    def step_one(hd, js):
        lanes_h = (lane >= hd * HEAD_DIM) & (lane < (hd + 1) * HEAD_DIM)
        sum_lane = (1 - hd) * HEAD_DIM
        lhs = lhs_ref[hd]
        vbs = []
        for n, j in enumerate(js):
            k0 = pl.multiple_of(j * tk, tk)
            w = jnp.concatenate([kt_ref[0, :, pl.ds(k0, tk)], kaugt_ref[0, :, pl.ds(k0, tk)]], axis=0)
            vbs.append(v_ref[0, pl.ds(k0, tk), :])
            s_ref[n] = _dot(lhs, w)
        for n in range(len(js)):
            for r0 in range(0, tq, rb):
                chunks = [s_ref[n, r0:r0 + rb, c0:c0 + LANES] for c0 in range(0, tk, LANES)]
                m_old = m_ref[hd, r0:r0 + rb, :]
                m_new = jnp.maximum(m_old, jnp.max(functools.reduce(jnp.maximum, chunks), axis=1, keepdims=True))
                alpha_ref[hd, r0:r0 + rb, :] = jnp.exp2(m_old - m_new)
                m_ref[hd, r0:r0 + rb, :] = m_new
                for c0, ch in zip(range(0, tk, LANES), chunks):
                    p_ref[n, r0:r0 + rb, c0:c0 + LANES] = jnp.exp2(ch - m_new).astype(_BF16)
            vaug = jnp.where(lanes_h, vbs[n], jnp.where(lane == sum_lane, 1.0, 0.0).astype(_BF16))
            acc_ref[hd] = alpha_ref[hd] * acc_ref[hd] + _dot(p_ref[n], vaug)

    jf0 = first_needed(0)
    jf1 = first_needed(1)
    j_lo = jnp.minimum(jf0, jf1)
    j_first = jnp.maximum(jf0, jf1)
    slow = (jf1 < jf0).astype(jnp.int32)
    n_one = j_first - j_lo

    @pl.when((n_one & 1) == 1)
    def _():
        step_one(slow, [j_lo])

    j_lo2 = j_lo + (n_one & 1)

    def pair_one(t, carry):
        step_one(slow, [j_lo2 + 2 * t, j_lo2 + 2 * t + 1])
        return carry

    lax.fori_loop(0, n_one >> 1, pair_one, 0)

    n_old = j_diag - j_first

    @pl.when((n_old & 1) == 1)
    def _():
        step(j_first, False)

    j_2 = j_first + (n_old & 1)

    @pl.when((n_old & 2) == 2)
    def _():
        step(j_2, False)
        step(j_2 + 1, False)

    j_4 = j_2 + (n_old & 2)

    def quad_step(t, carry):
        for c in range(4):
            step(j_4 + 4 * t + c, False)
        return carry

    lax.fori_loop(0, n_old >> 2, quad_step, 0)

    a0 = acc_ref[0]
    a1 = acc_ref[1]
    l0 = a0[:, HEAD_DIM:HEAD_DIM + 1]
    l1 = a1[:, 0:1]
    y = jnp.where(lane < HEAD_DIM, a0 / l0, a1 / l1)
    o_ref[0] = (y * za_ref[0].astype(_F32)).astype(_BF16)


def _fox_attention(base, fend, q, kt, kaugt, v, qaug, za, rb=32):
    bsz, s, _ = q.shape
    tq, tk = ATTN_TILE, ATTN_KBLOCK
    nq = s // tq
    qspec = pl.BlockSpec((1, tq, LANES), lambda b, hp, i, *_: (b, i, hp))
    grid_spec = pltpu.PrefetchScalarGridSpec(
        num_scalar_prefetch=2,
        grid=(bsz, HEAD_PAIRS, nq),
        in_specs=[qspec,
                  pl.BlockSpec((1, LANES, s), lambda b, hp, i, *_: (b, hp, 0)),
                  pl.BlockSpec((1, LANES, s), lambda b, hp, i, *_: (b, 0, 0)),
                  pl.BlockSpec((1, s, LANES), lambda b, hp, i, *_: (b, 0, hp)),
                  pl.BlockSpec((1, tq, LANES), lambda b, hp, i, *_: (b, i, 0)),
                  qspec],
        out_specs=qspec,
        scratch_shapes=[pltpu.VMEM((2, tq, 2 * LANES), _BF16),
                        pltpu.VMEM((2, tq, tk), _F32), pltpu.VMEM((2, tq, tk), _BF16),
                        pltpu.VMEM((2, tq, LANES), _F32), pltpu.VMEM((2, tq, LANES), _F32),
                        pltpu.VMEM((2, tq, LANES), _F32)])
    return pl.pallas_call(
        functools.partial(_attn_kernel, tq=tq, tk=tk, rb=rb, nq=nq),
        grid_spec=grid_spec,
        out_shape=jax.ShapeDtypeStruct((bsz, s, FOX_WIDTH), _BF16),
        compiler_params=pltpu.CompilerParams(
            dimension_semantics=("arbitrary", "arbitrary", "arbitrary"), vmem_limit_bytes=VMEM_LIMIT),
        name="fox_attn",
    )(base, fend, q, kt, kaugt, v, qaug, za)


def _gelu_tanh(x):
    return 0.5 * x * (1.0 + jnp.tanh(math.sqrt(2.0 / math.pi) * (x + 0.044715 * (x * x * x))))


def _s5_kernel(u_ref, zb_ref, ar_ref, ai_ref, bblk_ref, cblk_ref, dskip_ref, wglu_ref, bglu_ref,
               o_ref, xr_ref, xi_ref, sr_ref, si_ref, *, tt, nbatch, slabs_per_pass):
    t = pl.program_id(0)
    nchain = nbatch * S5_HALVES

    @pl.when(t == 0)
    def _():
        sr_ref[...] = jnp.zeros_like(sr_ref)
        si_ref[...] = jnp.zeros_like(si_ref)

    for hf in range(S5_HALVES):
        ub = jnp.concatenate([u_ref[b, :, hf * S5_HALF_CH:(hf + 1) * S5_HALF_CH] for b in range(nbatch)],
                             axis=0)
        bu = _dot(ub, bblk_ref[hf])
        for b in range(nbatch):
            kc = b * S5_HALVES + hf
            rows = slice(b * tt, (b + 1) * tt)
            for j in range(S5_SLABS):
                xr_ref[j, pl.ds(kc, tt, stride=nchain), :] = bu[rows, j * LANES:(j + 1) * LANES]
                xi_ref[j, pl.ds(kc, tt, stride=nchain), :] = bu[rows, (S5_SLABS + j) * LANES:
                                                                (S5_SLABS + j + 1) * LANES]

    for j0 in range(0, S5_SLABS, slabs_per_pass):
        js = list(range(j0, j0 + slabs_per_pass))
        ars = [ar_ref[j] for j in js]
        ais = [ai_ref[j] for j in js]

        def body(s, carry, js=js, ars=ars, ais=ais):
            r0 = pl.multiple_of(s * nchain, nchain)
            new = []
            for n, j in enumerate(js):
                sr, si = carry[2 * n], carry[2 * n + 1]
                br = xr_ref[j, pl.ds(r0, nchain), :]
                bi = xi_ref[j, pl.ds(r0, nchain), :]
                nr = (ars[n] * sr - ais[n] * si) + br
                ni = (ars[n] * si + ais[n] * sr) + bi
                xr_ref[j, pl.ds(r0, nchain), :] = nr
                xi_ref[j, pl.ds(r0, nchain), :] = ni
                new += [nr, ni]
            return tuple(new)

        init = []
        for j in js:
            init += [sr_ref[j], si_ref[j]]
        fin = lax.fori_loop(0, tt, body, tuple(init), unroll=4)
        for n, j in enumerate(js):
            sr_ref[j] = fin[2 * n]
            si_ref[j] = fin[2 * n + 1]

    halves = []
    for hf in range(S5_HALVES):
        xall = []
        for b in range(nbatch):
            kc = b * S5_HALVES + hf
            xs = [xr_ref[j, pl.ds(kc, tt, stride=nchain), :] for j in range(S5_SLABS)]
            xs += [xi_ref[j, pl.ds(kc, tt, stride=nchain), :] for j in range(S5_SLABS)]
            xall.append(jnp.concatenate(xs, axis=1).astype(_BF16))
        halves.append(_dot(jnp.concatenate(xall, axis=0), cblk_ref[hf]))
    w = u_ref.shape[-1]
    y = jnp.concatenate(halves, axis=1)
    y = y + dskip_ref[...] * u_ref[...].reshape(nbatch * tt, w).astype(_F32)
    y = _gelu_tanh(y)
    y = y * _sigmoid(_dot(y.astype(_BF16), wglu_ref[...]) + bglu_ref[...])
    y = y * zb_ref[...].reshape(nbatch * tt, w).astype(_F32)
    o_ref[...] = y.astype(_BF16).reshape(nbatch, tt, w)


def _s5_scan(u, zb, ar_t, ai_t, bblk, cblk, d_skip, w_glu, b_glu, tt=256, slabs_per_pass=8):
    bsz, s, w = u.shape
    nchain = bsz * S5_HALVES
    assert nchain == SUBLANES, "the scan packs batch x state-half chains into the 8 sublanes"
    tok_spec = pl.BlockSpec((bsz, tt, w), lambda t: (0, t, 0))
    const = lambda shape: pl.BlockSpec(shape, lambda t: (0,) * len(shape), pipeline_mode=pl.Buffered(1))
    return pl.pallas_call(
        functools.partial(_s5_kernel, tt=tt, nbatch=bsz, slabs_per_pass=slabs_per_pass),
        grid=(s // tt,),
        in_specs=[tok_spec, tok_spec, const(ar_t.shape), const(ai_t.shape),
                  const(bblk.shape), const(cblk.shape), const((1, w)), const(w_glu.shape), const((1, w))],
        out_specs=tok_spec,
        out_shape=jax.ShapeDtypeStruct((bsz, s, w), _BF16),
        scratch_shapes=[pltpu.VMEM((S5_SLABS, tt * nchain, LANES), _F32),
                        pltpu.VMEM((S5_SLABS, tt * nchain, LANES), _F32),
                        pltpu.VMEM((S5_SLABS, nchain, LANES), _F32),
                        pltpu.VMEM((S5_SLABS, nchain, LANES), _F32)],
        compiler_params=pltpu.CompilerParams(
            dimension_semantics=("arbitrary",), vmem_limit_bytes=VMEM_LIMIT),
        name="s5_scan",
    )(u, zb, ar_t, ai_t, bblk, cblk, d_skip.reshape(1, w), w_glu, b_glu.reshape(1, w))


def _out_kernel(x_ref, ya_ref, yb_ref, ga_ref, gb_ref, gate_ref, wua_ref, wub_ref, wo_ref, gf_ref, o_ref):
    ma = ga_ref[0].astype(_F32) * _dot(ya_ref[0], wua_ref[...])
    mb = gb_ref[0].astype(_F32) * _dot(yb_ref[0], wub_ref[...])
    merged = (ma + mb).astype(_BF16)
    xn = x_ref[0] + gate_ref[0] * _dot(merged, wo_ref[...])
    ms = jnp.mean(xn * xn, axis=-1, keepdims=True)
    o_ref[0] = xn * lax.rsqrt(ms + EPS) * gf_ref[...]


def _out_proj(x, ya, yb, ga, gb, gate, w_up_a, w_up_b, w_out, g_final, tm=1024):
    bsz, s, d = x.shape
    tok_spec = lambda width: pl.BlockSpec((1, tm, width), lambda b, i: (b, i, 0))
    const = lambda shape: pl.BlockSpec(shape, lambda b, i: (0,) * len(shape), pipeline_mode=pl.Buffered(1))
    return pl.pallas_call(
        _out_kernel,
        grid=(bsz, s // tm),
        in_specs=[tok_spec(d), tok_spec(FOX_WIDTH), tok_spec(S5_WIDTH), tok_spec(d), tok_spec(d),
                  pl.BlockSpec((1, 1, d), lambda b, i: (b, 0, 0)),
                  const(w_up_a.shape), const(w_up_b.shape), const(w_out.shape), const((1, d))],
        out_specs=tok_spec(d),
        out_shape=jax.ShapeDtypeStruct((bsz, s, d), x.dtype),
        compiler_params=pltpu.CompilerParams(
            dimension_semantics=("arbitrary", "arbitrary"), vmem_limit_bytes=VMEM_LIMIT),
        name="out_proj",
    )(x, ya, yb, ga, gb, gate, w_up_a, w_up_b, w_out, g_final.reshape(1, d))


def _block_diag_in(bb):
    gh = S5_GROUPS // S5_HALVES
    bb = bb.reshape(S5_HALVES, gh, S5_GROUP, S5_STATE)
    eye = jnp.eye(gh, dtype=bb.dtype)
    out = bb[:, :, :, None, :] * eye[None, :, None, :, None]
    return out.reshape(S5_HALVES, gh * S5_GROUP, gh * S5_STATE)


def _block_diag_out(cc):
    gh = S5_GROUPS // S5_HALVES
    cc = cc.reshape(S5_HALVES, gh, S5_GROUP, S5_STATE)
    eye = jnp.eye(gh, dtype=cc.dtype)
    out = jnp.swapaxes(cc, 2, 3)[:, :, :, None, :] * eye[None, :, None, :, None]
    return out.reshape(S5_HALVES, gh * S5_STATE, gh * S5_GROUP)


def _chain_tiles(a, nbatch):
    a = a.reshape(S5_HALVES, S5_SLABS, LANES)
    a = jnp.swapaxes(a, 0, 1)
    return jnp.tile(a, (1, nbatch, 1))


def kernel(x, c, w_ada, b_ada, g_norm, w_in, b_f, a_re, a_im, log_dt, b_re, b_im, c_re, c_im,
           d_skip, w_glu, b_glu, w_up_a, w_up_b, w_out, g_final):
    bsz, s, d = x.shape
    depth = w_ada.shape[0]
    assert depth == 1 and d == D_MODEL

    mod = _adaln_mod(c, w_ada[0], b_ada[0])
    shift = mod[:, 0:d].reshape(bsz, 1, d)
    scale = mod[:, d:2 * d].reshape(bsz, 1, d)
    gate = mod[:, 2 * d:3 * d].reshape(bsz, 1, d)

    wt = jnp.swapaxes(w_in[0], 0, 1)
    o_z = 3 * FOX_WIDTH + HEADS
    wt_al = jnp.concatenate([wt[0:o_z], jnp.zeros((LANES - HEADS, d), wt.dtype), wt[o_z:]],
                            axis=0).astype(_BF16)

    q, kt, v, qaug, kaugt, za, u, zb, ga, gb, stats = _in_proj(
        x, g_norm[0], scale, shift, wt_al, b_f[0])

    base = _skip_plan(stats)
    base_tab = jnp.swapaxes(base[:, :, 0:HEADS], 1, 2).reshape(-1)
    fend = stats[:, ST_FEND:ST_FEND + KB_PER_TILE, :, 0:HEADS]
    fend_tab = jnp.transpose(fend, (0, 3, 2, 1)).reshape(-1)
    ya = _fox_attention(base_tab, fend_tab, q, kt, kaugt, v, qaug, za)

    abr, abi, bbr, bbi = _s5_params(a_re[0], a_im[0], log_dt[0],
                                    jnp.swapaxes(b_re[0], 1, 2), jnp.swapaxes(b_im[0], 1, 2))
    bblk = jnp.concatenate([_block_diag_in(bbr), _block_diag_in(bbi)], axis=2).astype(_BF16)
    cblk = jnp.concatenate([_block_diag_out(c_re[0]), -_block_diag_out(c_im[0])], axis=1).astype(_BF16)
    yb = _s5_scan(u, zb, _chain_tiles(abr, bsz), _chain_tiles(abi, bsz), bblk, cblk,
                  d_skip[0], w_glu[0].astype(_BF16), b_glu[0])

    return _out_proj(x, ya, yb, ga, gb, gate, w_up_a[0].astype(_BF16), w_up_b[0].astype(_BF16),
                     w_out[0].astype(_BF16), g_final)
```

```python
import functools
import math

import jax
import jax.numpy as jnp
from jax import lax
from jax.experimental import pallas as pl
from jax.experimental.pallas import tpu as pltpu

D_MODEL = 1024
EPS = 1e-6
FOX_WIDTH = 512
HEAD_DIM = 64
HEADS = 8
HEAD_PAIRS = HEADS // 2
S5_WIDTH = 512
S5_GROUP = 16
S5_GROUPS = 32
S5_STATE = 64

LANES = 128
SUBLANES = 8
VMEM_LIMIT = 56 * 1024 * 1024

S5_HALVES = 2
S5_HALF_CH = S5_WIDTH // S5_HALVES
S5_HALF_STATE = S5_GROUPS * S5_STATE // S5_HALVES
S5_SLABS = S5_HALF_STATE // LANES

NEG_BIG = -1e30
LOG2E = math.log2(math.e)

ATTN_TILE = 512
ATTN_KBLOCK = 512
KB_PER_TILE = ATTN_TILE // ATTN_KBLOCK
ST_QN2, ST_FSTART, ST_KMAX, ST_FEND = 0, 1, 2, 3
ST_PLANES = ST_FEND + KB_PER_TILE
SKIP_LOG2 = 40.0
NORM_SAFETY = 1.02

_F32 = jnp.float32
_BF16 = jnp.bfloat16


def _dot(a, b):
    return jnp.dot(a, b, preferred_element_type=_F32)


def _dot_nt(a, b):
    return lax.dot_general(a, b, (((1,), (1,)), ((), ())), preferred_element_type=_F32)


def _split3(x):
    hi = x.astype(_BF16).astype(_F32)
    r1 = x - hi
    mid = r1.astype(_BF16).astype(_F32)
    lo = (r1 - mid).astype(_BF16).astype(_F32)
    return hi, mid, lo


def _sigmoid(x):
    return 1.0 / (1.0 + jnp.exp(-x))


def _silu(x):
    return x * _sigmoid(x)


def _adaln_kernel(c_ref, w_ref, b_ref, o_ref):
    o_ref[...] = jnp.dot(c_ref[...], w_ref[...], precision=lax.Precision.HIGHEST,
                         preferred_element_type=_F32) + b_ref[...]


def _adaln_mod(c, w, b):
    bsz, d = c.shape
    n = w.shape[1]
    bn = 1024
    return pl.pallas_call(
        _adaln_kernel,
        grid=(n // bn,),
        in_specs=[pl.BlockSpec((bsz, d), lambda j: (0, 0)),
                  pl.BlockSpec((d, bn), lambda j: (0, j)),
                  pl.BlockSpec((1, bn), lambda j: (0, j))],
        out_specs=pl.BlockSpec((bsz, bn), lambda j: (0, j)),
        out_shape=jax.ShapeDtypeStruct((bsz, n), _F32),
        name="adaln_mod",
    )(c, w, b.reshape(1, n))


def _s5_param_kernel(are_ref, aim_ref, ldt_ref, bre_ref, bim_ref,
                     abr_ref, abi_ref, bbr_ref, bbi_ref):
    lr = are_ref[...]
    li = aim_ref[...]
    dt = jnp.exp(ldt_ref[...])
    mag = jnp.exp(lr * dt)
    ang = li * dt
    abr = mag * jnp.cos(ang)
    abi = mag * jnp.sin(ang)
    abr_ref[...] = abr
    abi_ref[...] = abi
    nr = abr - 1.0
    ni = abi
    den = lr * lr + li * li
    cr = (nr * lr + ni * li) / den
    ci = (ni * lr - nr * li) / den
    cr3 = cr[:, None, :]
    ci3 = ci[:, None, :]
    br = bre_ref[...]
    bi = bim_ref[...]
    bbr_ref[...] = cr3 * br - ci3 * bi
    bbi_ref[...] = cr3 * bi + ci3 * br


def _s5_params(a_re, a_im, log_dt, b_re_t, b_im_t):
    g, p = a_re.shape
    cg = b_re_t.shape[1]
    return pl.pallas_call(
        _s5_param_kernel,
        out_shape=(jax.ShapeDtypeStruct((g, p), _F32), jax.ShapeDtypeStruct((g, p), _F32),
                   jax.ShapeDtypeStruct((g, cg, p), _F32), jax.ShapeDtypeStruct((g, cg, p), _F32)),
        name="s5_params",
    )(a_re, a_im, log_dt.reshape(g, 1), b_re_t, b_im_t)


def _in_proj_kernel(x_ref, gn_ref, sc_ref, sh_ref, wt_ref, bf_ref,
                    q_ref, kt_ref, v_ref, qaug_ref, kaugt_ref, za_ref, u_ref, zb_ref, ga_ref, gb_ref,
                    stats_ref, carry_ref, kmax_ref, *, tm):
    i = pl.program_id(1)
    fw = FOX_WIDTH
    c_v, c_f = 2 * fw, 3 * fw
    c_za = c_f + LANES
    c_u, c_zb = c_za + fw, c_za + fw + S5_WIDTH
    c_ga = c_zb + S5_WIDTH
    c_gb = c_ga + D_MODEL

    @pl.when(i == 0)
    def _():
        carry_ref[...] = jnp.zeros_like(carry_ref)
        kmax_ref[...] = jnp.zeros_like(kmax_ref)

    x = x_ref[0]
    ms = jnp.mean(x * x, axis=-1, keepdims=True)
    h = x * lax.rsqrt(ms + EPS) * gn_ref[...]
    h = (h * (1.0 + sc_ref[0]) + sh_ref[0]).astype(_BF16)

    w = FOX_WIDTH
    qb = (_dot_nt(h, wt_ref[0:fw, :]) * (HEAD_DIM ** -0.5 * LOG2E)).astype(_BF16)
    q_ref[0] = qb
    ktb = _dot_nt(wt_ref[fw:c_v, :], h).astype(_BF16)
    kt_ref[0] = ktb
    v_ref[0] = _dot_nt(h, wt_ref[c_v:c_f, :]).astype(_BF16)

    seg = (lax.broadcasted_iota(jnp.int32, (w, LANES), 0) // HEAD_DIM
           == lax.broadcasted_iota(jnp.int32, (w, LANES), 1)).astype(_BF16)
    seg_t = (lax.broadcasted_iota(jnp.int32, (LANES, w), 1) // HEAD_DIM
             == lax.broadcasted_iota(jnp.int32, (LANES, w), 0)).astype(_BF16)
    qf = qb.astype(_F32)
    kf = ktb.astype(_F32)
    qn2 = _dot((qf * qf).astype(_BF16), seg)
    kn2 = _dot(seg_t, (kf * kf).astype(_BF16)).T
    kmax = jnp.maximum(kmax_ref[...], jnp.max(kn2, axis=0, keepdims=True))
    kmax_ref[...] = kmax
    za_ref[0] = _silu(_dot_nt(h, wt_ref[c_za:c_u, :])).astype(_BF16)
    u_ref[0] = _dot_nt(h, wt_ref[c_u:c_zb, :]).astype(_BF16)
    zb_ref[0] = _silu(_dot_nt(h, wt_ref[c_zb:c_ga, :])).astype(_BF16)
    ga_ref[0] = _sigmoid(_dot_nt(h, wt_ref[c_ga:c_gb, :])).astype(_BF16)
    gb_ref[0] = _sigmoid(_dot_nt(h, wt_ref[c_gb:c_gb + D_MODEL, :])).astype(_BF16)

    flt = _dot_nt(wt_ref[c_f:c_f + 2 * HEADS, :], h)[0:HEADS] + bf_ref[...]
    logft = -(jnp.maximum(-flt, 0.0) + jnp.log1p(jnp.exp(-jnp.abs(flt)))) * LOG2E
    lmh_t = jnp.concatenate(_split3(logft), axis=0).astype(_BF16)

    ta = ATTN_TILE
    row = lax.broadcasted_iota(jnp.int32, (ta, ta), 0)
    col = lax.broadcasted_iota(jnp.int32, (ta, ta), 1)
    triu = (row <= col).astype(_BF16)
    ones = jnp.ones((HEADS, ta), _F32)
    zeros = jnp.zeros((HEADS, ta), _F32)
    nt = tm // ta
    carry = carry_ref[...]
    for t in range(nt):
        r0 = t * ta
        parts = _dot(lmh_t[:, r0:r0 + ta], triu)
        cum_t = (parts[0:HEADS] + parts[HEADS:2 * HEADS]) + parts[2 * HEADS:3 * HEADS] + carry[:, 0:1]
        carry = jnp.broadcast_to(cum_t[:, ta - 1:ta], (HEADS, LANES))

        kh, km, kl = _split3(cum_t)
        kaugt_ref[0, :, r0:r0 + ta] = jnp.concatenate(
            [ones, ones, ones, zeros, kh, km, kl] + [zeros] * 9, axis=0).astype(_BF16)
        cum = jnp.concatenate([cum_t] + [zeros] * (LANES // HEADS - 1), axis=0).T
        qh, qm, ql = _split3(cum)
        qaug_ref[0, r0:r0 + ta, :] = (qh + pltpu.roll(qm, HEADS, 1)
                                      + pltpu.roll(ql, 2 * HEADS, 1)).astype(_BF16)

        tile = i * nt + t
        stats_ref[0, ST_QN2, pl.ds(tile, 1), :] = jnp.max(qn2[r0:r0 + ta], axis=0, keepdims=True)
        stats_ref[0, ST_FSTART, pl.ds(tile, 1), :] = cum[0:1, :]
        for c in range(KB_PER_TILE):
            last = (c + 1) * ATTN_KBLOCK - 1
            stats_ref[0, ST_FEND + c, pl.ds(tile, 1), :] = cum[last:last + 1, :]
    carry_ref[...] = carry
    stats_ref[0, ST_KMAX] = jnp.broadcast_to(kmax, stats_ref.shape[2:])


def _in_proj(x, g_norm, scale, shift, wt_al, b_f, tm=1024):
    bsz, s, d = x.shape
    ni = s // tm
    tok = lambda width, dt: jax.ShapeDtypeStruct((bsz, s, width), dt)
    tok_spec = lambda width: pl.BlockSpec((1, tm, width), lambda b, i: (b, i, 0))
    tr = lambda rows: jax.ShapeDtypeStruct((bsz, rows, s), _BF16)
    tr_spec = lambda rows: pl.BlockSpec((1, rows, tm), lambda b, i: (b, 0, i))
    const = lambda shape: pl.BlockSpec(shape, lambda b, i: (0,) * len(shape), pipeline_mode=pl.Buffered(1))
    per_b = pl.BlockSpec((1, 1, d), lambda b, i: (b, 0, 0))
    return pl.pallas_call(
        functools.partial(_in_proj_kernel, tm=tm),
        grid=(bsz, ni),
        in_specs=[tok_spec(d), const((1, d)), per_b, per_b,
                  const(wt_al.shape), const((HEADS, 1))],
        out_specs=[tok_spec(FOX_WIDTH), tr_spec(FOX_WIDTH), tok_spec(FOX_WIDTH),
                   tok_spec(LANES), tr_spec(LANES),
                   tok_spec(FOX_WIDTH), tok_spec(S5_WIDTH), tok_spec(S5_WIDTH),
                   tok_spec(d), tok_spec(d),
                   pl.BlockSpec((1, ST_PLANES, s // ATTN_TILE, LANES), lambda b, i: (b, 0, 0, 0))],
        out_shape=[tok(FOX_WIDTH, _BF16), tr(FOX_WIDTH), tok(FOX_WIDTH, _BF16),
                   tok(LANES, _BF16), tr(LANES),
                   tok(FOX_WIDTH, _BF16), tok(S5_WIDTH, _BF16), tok(S5_WIDTH, _BF16),
                   tok(d, _BF16), tok(d, _BF16),
                   jax.ShapeDtypeStruct((bsz, ST_PLANES, s // ATTN_TILE, LANES), _F32)],
        scratch_shapes=[pltpu.VMEM((HEADS, LANES), _F32), pltpu.VMEM((1, LANES), _F32)],
        compiler_params=pltpu.CompilerParams(
            dimension_semantics=("arbitrary", "arbitrary"), vmem_limit_bytes=VMEM_LIMIT),
        name="in_proj",
    )(x, g_norm.reshape(1, d), scale, shift, wt_al, b_f.reshape(HEADS, 1))


def _plan_kernel(st_ref, o_ref):
    qn2 = st_ref[0, ST_QN2]
    fstart = st_ref[0, ST_FSTART]
    kmax2 = st_ref[0, ST_KMAX]
    o_ref[0] = fstart + (NORM_SAFETY * jnp.sqrt(qn2 * kmax2) + SKIP_LOG2)


def _skip_plan(stats):
    bsz, _, nq, _ = stats.shape
    return pl.pallas_call(
        _plan_kernel,
        grid=(bsz,),
        in_specs=[pl.BlockSpec((1, ST_PLANES, nq, LANES), lambda b: (b, 0, 0, 0))],
        out_specs=pl.BlockSpec((1, nq, LANES), lambda b: (b, 0, 0)),
        out_shape=jax.ShapeDtypeStruct((bsz, nq, LANES), _F32),
        name="skip_plan",
    )(stats)


def _attn_kernel(base_ref, fend_ref, q_ref, kt_ref, kaugt_ref, v_ref, qaug_ref, za_ref, o_ref,
                 lhs_ref, s_ref, p_ref, m_ref, alpha_ref, acc_ref, *, tq, tk, rb, nq):
    b = pl.program_id(0)
    hp = pl.program_id(1)
    i = pl.program_id(2)
    lane = lax.broadcasted_iota(jnp.int32, (1, LANES), 1)
    q = q_ref[0]
    qa = qaug_ref[0]

    head_lanes = [(lane >= hh * HEAD_DIM) & (lane < (hh + 1) * HEAD_DIM) for hh in range(2)]
    for hh in range(2):
        h = 2 * hp + hh
        f_lanes = (lane == h) | (lane == HEADS + h) | (lane == 2 * HEADS + h)
        neg_lanes = (lane == 4 * HEADS + h) | (lane == 5 * HEADS + h) | (lane == 6 * HEADS + h)
        qm = jnp.where(head_lanes[hh], q, jnp.zeros_like(q))
        qaug = jnp.where(f_lanes, qa, jnp.where(neg_lanes, -1.0, 0.0).astype(_BF16))
        lhs_ref[hh] = jnp.concatenate([qm, qaug], axis=1)
        m_ref[hh] = jnp.full((tq, LANES), NEG_BIG, _F32)
        acc_ref[hh] = jnp.zeros((tq, LANES), _F32)

    def step(j, masked):
        k0 = pl.multiple_of(j * tk, tk)
        w = jnp.concatenate([kt_ref[0, :, pl.ds(k0, tk)], kaugt_ref[0, :, pl.ds(k0, tk)]], axis=0)
        vb = v_ref[0, pl.ds(k0, tk), :]
        for hh in range(2):
            s = _dot(lhs_ref[hh], w)
            if masked:
                r = lax.broadcasted_iota(jnp.int32, (tq, tk), 0)
                c = lax.broadcasted_iota(jnp.int32, (tq, tk), 1)
                s = jnp.where(c - r <= i * tq - j * tk, s, NEG_BIG)
            s_ref[hh] = s
        for hh in range(2):
            for r0 in range(0, tq, rb):
                chunks = [s_ref[hh, r0:r0 + rb, c0:c0 + LANES] for c0 in range(0, tk, LANES)]
                m_old = m_ref[hh, r0:r0 + rb, :]
                m_new = jnp.maximum(m_old, jnp.max(functools.reduce(jnp.maximum, chunks), axis=1, keepdims=True))
                alpha_ref[hh, r0:r0 + rb, :] = jnp.exp2(m_old - m_new)
                m_ref[hh, r0:r0 + rb, :] = m_new
                for c0, ch in zip(range(0, tk, LANES), chunks):
                    p_ref[hh, r0:r0 + rb, c0:c0 + LANES] = jnp.exp2(ch - m_new).astype(_BF16)
        for hh in range(2):
            sum_lane = (1 - hh) * HEAD_DIM
            vaug = jnp.where(head_lanes[hh], vb, jnp.where(lane == sum_lane, 1.0, 0.0).astype(_BF16))
            acc_ref[hh] = alpha_ref[hh] * acc_ref[hh] + _dot(p_ref[hh], vaug)

    kpt = tq // tk
    j_diag = i * kpt
    for c in range(kpt):
        step(j_diag + c, True)

    def first_needed(hh):
        thr = base_ref[(b * HEADS + 2 * hp + hh) * nq + i] - jnp.min(m_ref[hh])
        row = (b * HEADS + 2 * hp + hh) * (nq * kpt)
        count = jnp.int32(0)
        for jj in range(nq * kpt):
            count += ((fend_ref[row + jj] >= thr) & (jj < j_diag)).astype(jnp.int32)
        return count

    def step_one(hd, js):
        lanes_h = (lane >= hd * HEAD_DIM) & (lane < (hd + 1) * HEAD_DIM)
        sum_lane = (1 - hd) * HEAD_DIM
        lhs = lhs_ref[hd]
        vbs = []
        for n, j in enumerate(js):
            k0 = pl.multiple_of(j * tk, tk)
            w = jnp.concatenate([kt_ref[0, :, pl.ds(k0, tk)], kaugt_ref[0, :, pl.ds(k0, tk)]], axis=0)
            vbs.append(v_ref[0, pl.ds(k0, tk), :])
            s_ref[n] = _dot(lhs, w)
        for n in range(len(js)):
            for r0 in range(0, tq, rb):
                chunks = [s_ref[n, r0:r0 + rb, c0:c0 + LANES] for c0 in range(0, tk, LANES)]
                m_old = m_ref[hd, r0:r0 + rb, :]
                m_new = jnp.maximum(m_old, jnp.max(functools.reduce(jnp.maximum, chunks), axis=1, keepdims=True))
                alpha_ref[hd, r0:r0 + rb, :] = jnp.exp2(m_old - m_new)
                m_ref[hd, r0:r0 + rb, :] = m_new
                for c0, ch in zip(range(0, tk, LANES), chunks):
                    p_ref[n, r0:r0 + rb, c0:c0 + LANES] = jnp.exp2(ch - m_new).astype(_BF16)
            vaug = jnp.where(lanes_h, vbs[n], jnp.where(lane == sum_lane, 1.0, 0.0).astype(_BF16))
            acc_ref[hd] = alpha_ref[hd] * acc_ref[hd] + _dot(p_ref[n], vaug)

    jf0 = first_needed(0)
    jf1 = first_needed(1)
    j_lo = jnp.minimum(jf0, jf1)
    j_first = jnp.maximum(jf0, jf1)
    slow = (jf1 < jf0).astype(jnp.int32)
    n_one = j_first - j_lo

    @pl.when((n_one & 1) == 1)
    def _():
        step_one(slow, [j_lo])

    j_lo2 = j_lo + (n_one & 1)

    def pair_one(t, carry):
        step_one(slow, [j_lo2 + 2 * t, j_lo2 + 2 * t + 1])
        return carry

    lax.fori_loop(0, n_one >> 1, pair_one, 0)

    n_old = j_diag - j_first

    @pl.when((n_old & 1) == 1)
    def _():
        step(j_first, False)

    j_2 = j_first + (n_old & 1)

    @pl.when((n_old & 2) == 2)
    def _():
        step(j_2, False)
        step(j_2 + 1, False)

    j_4 = j_2 + (n_old & 2)

    def quad_step(t, carry):
        for c in range(4):
            step(j_4 + 4 * t + c, False)
        return carry

    lax.fori_loop(0, n_old >> 2, quad_step, 0)

    a0 = acc_ref[0]
    a1 = acc_ref[1]
    l0 = a0[:, HEAD_DIM:HEAD_DIM + 1]
    l1 = a1[:, 0:1]
    y = jnp.where(lane < HEAD_DIM, a0 / l0, a1 / l1)
    o_ref[0] = (y * za_ref[0].astype(_F32)).astype(_BF16)


def _fox_attention(base, fend, q, kt, kaugt, v, qaug, za, rb=32):
    bsz, s, _ = q.shape
    tq, tk = ATTN_TILE, ATTN_KBLOCK
    nq = s // tq
    qspec = pl.BlockSpec((1, tq, LANES), lambda b, hp, i, *_: (b, i, hp))
    grid_spec = pltpu.PrefetchScalarGridSpec(
        num_scalar_prefetch=2,
        grid=(bsz, HEAD_PAIRS, nq),
        in_specs=[qspec,
                  pl.BlockSpec((1, LANES, s), lambda b, hp, i, *_: (b, hp, 0)),
                  pl.BlockSpec((1, LANES, s), lambda b, hp, i, *_: (b, 0, 0)),
                  pl.BlockSpec((1, s, LANES), lambda b, hp, i, *_: (b, 0, hp)),
                  pl.BlockSpec((1, tq, LANES), lambda b, hp, i, *_: (b, i, 0)),
                  qspec],
        out_specs=qspec,
        scratch_shapes=[pltpu.VMEM((2, tq, 2 * LANES), _BF16),
                        pltpu.VMEM((2, tq, tk), _F32), pltpu.VMEM((2, tq, tk), _BF16),
                        pltpu.VMEM((2, tq, LANES), _F32), pltpu.VMEM((2, tq, LANES), _F32),
                        pltpu.VMEM((2, tq, LANES), _F32)])
    return pl.pallas_call(
        functools.partial(_attn_kernel, tq=tq, tk=tk, rb=rb, nq=nq),
        grid_spec=grid_spec,
        out_shape=jax.ShapeDtypeStruct((bsz, s, FOX_WIDTH), _BF16),
        compiler_params=pltpu.CompilerParams(
            dimension_semantics=("arbitrary", "arbitrary", "arbitrary"), vmem_limit_bytes=VMEM_LIMIT),
        name="fox_attn",
    )(base, fend, q, kt, kaugt, v, qaug, za)


def _gelu_tanh(x):
    return 0.5 * x * (1.0 + jnp.tanh(math.sqrt(2.0 / math.pi) * (x + 0.044715 * (x * x * x))))


def _s5_kernel(u_ref, zb_ref, ar_ref, ai_ref, bblk_ref, cblk_ref, dskip_ref, wglu_ref, bglu_ref,
               o_ref, xr_ref, xi_ref, sr_ref, si_ref, y_ref, *, tt, nbatch, slabs_per_pass, ntiles):
    t = pl.program_id(0)
    nchain = nbatch * S5_HALVES
    w = u_ref.shape[-1]

    @pl.when(t == 0)
    def _():
        sr_ref[...] = jnp.zeros_like(sr_ref)
        si_ref[...] = jnp.zeros_like(si_ref)
        y_ref[...] = jnp.zeros_like(y_ref)

    y = _gelu_tanh(y_ref[...])
    y = y * _sigmoid(_dot(y.astype(_BF16), wglu_ref[...]) + bglu_ref[...])
    y = y * zb_ref[...].reshape(nbatch * tt, w).astype(_F32)
    o_ref[...] = y.astype(_BF16).reshape(nbatch, tt, w)

    for hf in range(S5_HALVES):
        ub = jnp.concatenate([u_ref[b, :, hf * S5_HALF_CH:(hf + 1) * S5_HALF_CH] for b in range(nbatch)],
                             axis=0)
        bu = _dot(ub, bblk_ref[hf])
        for b in range(nbatch):
            kc = b * S5_HALVES + hf
            rows = slice(b * tt, (b + 1) * tt)
            for j in range(S5_SLABS):
                xr_ref[j, pl.ds(kc, tt, stride=nchain), :] = bu[rows, j * LANES:(j + 1) * LANES]
                xi_ref[j, pl.ds(kc, tt, stride=nchain), :] = bu[rows, (S5_SLABS + j) * LANES:
                                                                (S5_SLABS + j + 1) * LANES]

    @pl.when(t < ntiles)
    def _():
        for j0 in range(0, S5_SLABS, slabs_per_pass):
            js = list(range(j0, j0 + slabs_per_pass))
            ars = [ar_ref[j] for j in js]
            ais = [ai_ref[j] for j in js]

            def body(s, carry, js=js, ars=ars, ais=ais):
                r0 = pl.multiple_of(s * nchain, nchain)
                new = []
                for n, j in enumerate(js):
                    sr, si = carry[2 * n], carry[2 * n + 1]
                    br = xr_ref[j, pl.ds(r0, nchain), :]
                    bi = xi_ref[j, pl.ds(r0, nchain), :]
                    nr = (ars[n] * sr - ais[n] * si) + br
                    ni = (ars[n] * si + ais[n] * sr) + bi
                    xr_ref[j, pl.ds(r0, nchain), :] = nr
                    xi_ref[j, pl.ds(r0, nchain), :] = ni
                    new += [nr, ni]
                return tuple(new)

            init = []
            for j in js:
                init += [sr_ref[j], si_ref[j]]
            fin = lax.fori_loop(0, tt, body, tuple(init), unroll=4)
            for n, j in enumerate(js):
                sr_ref[j] = fin[2 * n]
                si_ref[j] = fin[2 * n + 1]

        halves = []
        for hf in range(S5_HALVES):
            xall = []
            for b in range(nbatch):
                kc = b * S5_HALVES + hf
                xs = [xr_ref[j, pl.ds(kc, tt, stride=nchain), :] for j in range(S5_SLABS)]
                xs += [xi_ref[j, pl.ds(kc, tt, stride=nchain), :] for j in range(S5_SLABS)]
                xall.append(jnp.concatenate(xs, axis=1).astype(_BF16))
            halves.append(_dot(jnp.concatenate(xall, axis=0), cblk_ref[hf]))
        y_ref[...] = (jnp.concatenate(halves, axis=1)
                      + dskip_ref[...] * u_ref[...].reshape(nbatch * tt, w).astype(_F32))


def _s5_scan(u, zb, ar_t, ai_t, bblk, cblk, d_skip, w_glu, b_glu, tt=256, slabs_per_pass=8):
    bsz, s, w = u.shape
    nchain = bsz * S5_HALVES
    assert nchain == SUBLANES, "the scan packs batch x state-half chains into the 8 sublanes"
    ntiles = s // tt
    cur_spec = pl.BlockSpec((bsz, tt, w), lambda t: (0, jnp.minimum(t, ntiles - 1), 0))
    prev_spec = pl.BlockSpec((bsz, tt, w), lambda t: (0, jnp.maximum(t - 1, 0), 0))
    const = lambda shape: pl.BlockSpec(shape, lambda t: (0,) * len(shape), pipeline_mode=pl.Buffered(1))
    return pl.pallas_call(
        functools.partial(_s5_kernel, tt=tt, nbatch=bsz, slabs_per_pass=slabs_per_pass, ntiles=ntiles),
        grid=(ntiles + 1,),
        in_specs=[cur_spec, prev_spec, const(ar_t.shape), const(ai_t.shape),
                  const(bblk.shape), const(cblk.shape), const((1, w)), const(w_glu.shape), const((1, w))],
        out_specs=prev_spec,
        out_shape=jax.ShapeDtypeStruct((bsz, s, w), _BF16),
        scratch_shapes=[pltpu.VMEM((S5_SLABS, tt * nchain, LANES), _F32),
                        pltpu.VMEM((S5_SLABS, tt * nchain, LANES), _F32),
                        pltpu.VMEM((S5_SLABS, nchain, LANES), _F32),
                        pltpu.VMEM((S5_SLABS, nchain, LANES), _F32),
                        pltpu.VMEM((bsz * tt, w), _F32)],
        compiler_params=pltpu.CompilerParams(
            dimension_semantics=("arbitrary",), vmem_limit_bytes=VMEM_LIMIT),
        name="s5_scan",
    )(u, zb, ar_t, ai_t, bblk, cblk, d_skip.reshape(1, w), w_glu, b_glu.reshape(1, w))


def _out_kernel(x_ref, ya_ref, yb_ref, ga_ref, gb_ref, gate_ref, wua_ref, wub_ref, wo_ref, gf_ref, o_ref):
    ma = ga_ref[0].astype(_F32) * _dot(ya_ref[0], wua_ref[...])
    mb = gb_ref[0].astype(_F32) * _dot(yb_ref[0], wub_ref[...])
    merged = (ma + mb).astype(_BF16)
    xn = x_ref[0] + gate_ref[0] * _dot(merged, wo_ref[...])
    ms = jnp.mean(xn * xn, axis=-1, keepdims=True)
    o_ref[0] = xn * lax.rsqrt(ms + EPS) * gf_ref[...]


def _out_proj(x, ya, yb, ga, gb, gate, w_up_a, w_up_b, w_out, g_final, tm=1024):
    bsz, s, d = x.shape
    tok_spec = lambda width: pl.BlockSpec((1, tm, width), lambda b, i: (b, i, 0))
    const = lambda shape: pl.BlockSpec(shape, lambda b, i: (0,) * len(shape), pipeline_mode=pl.Buffered(1))
    return pl.pallas_call(
        _out_kernel,
        grid=(bsz, s // tm),
        in_specs=[tok_spec(d), tok_spec(FOX_WIDTH), tok_spec(S5_WIDTH), tok_spec(d), tok_spec(d),
                  pl.BlockSpec((1, 1, d), lambda b, i: (b, 0, 0)),
                  const(w_up_a.shape), const(w_up_b.shape), const(w_out.shape), const((1, d))],
        out_specs=tok_spec(d),
        out_shape=jax.ShapeDtypeStruct((bsz, s, d), x.dtype),
        compiler_params=pltpu.CompilerParams(
            dimension_semantics=("arbitrary", "arbitrary"), vmem_limit_bytes=VMEM_LIMIT),
        name="out_proj",
    )(x, ya, yb, ga, gb, gate, w_up_a, w_up_b, w_out, g_final.reshape(1, d))


def _block_diag_in(bb):
    gh = S5_GROUPS // S5_HALVES
    bb = bb.reshape(S5_HALVES, gh, S5_GROUP, S5_STATE)
    eye = jnp.eye(gh, dtype=bb.dtype)
    out = bb[:, :, :, None, :] * eye[None, :, None, :, None]
    return out.reshape(S5_HALVES, gh * S5_GROUP, gh * S5_STATE)


def _block_diag_out(cc):
    gh = S5_GROUPS // S5_HALVES
    cc = cc.reshape(S5_HALVES, gh, S5_GROUP, S5_STATE)
    eye = jnp.eye(gh, dtype=cc.dtype)
    out = jnp.swapaxes(cc, 2, 3)[:, :, :, None, :] * eye[None, :, None, :, None]
    return out.reshape(S5_HALVES, gh * S5_STATE, gh * S5_GROUP)


def _chain_tiles(a, nbatch):
    a = a.reshape(S5_HALVES, S5_SLABS, LANES)
    a = jnp.swapaxes(a, 0, 1)
    return jnp.tile(a, (1, nbatch, 1))


def kernel(x, c, w_ada, b_ada, g_norm, w_in, b_f, a_re, a_im, log_dt, b_re, b_im, c_re, c_im,
           d_skip, w_glu, b_glu, w_up_a, w_up_b, w_out, g_final):
    bsz, s, d = x.shape
    depth = w_ada.shape[0]
    assert depth == 1 and d == D_MODEL

    mod = _adaln_mod(c, w_ada[0], b_ada[0])
    shift = mod[:, 0:d].reshape(bsz, 1, d)
    scale = mod[:, d:2 * d].reshape(bsz, 1, d)
    gate = mod[:, 2 * d:3 * d].reshape(bsz, 1, d)

    wt = jnp.swapaxes(w_in[0], 0, 1)
    o_z = 3 * FOX_WIDTH + HEADS
    wt_al = jnp.concatenate([wt[0:o_z], jnp.zeros((LANES - HEADS, d), wt.dtype), wt[o_z:]],
                            axis=0).astype(_BF16)

    q, kt, v, qaug, kaugt, za, u, zb, ga, gb, stats = _in_proj(
        x, g_norm[0], scale, shift, wt_al, b_f[0])

    base = _skip_plan(stats)
    base_tab = jnp.swapaxes(base[:, :, 0:HEADS], 1, 2).reshape(-1)
    fend = stats[:, ST_FEND:ST_FEND + KB_PER_TILE, :, 0:HEADS]
    fend_tab = jnp.transpose(fend, (0, 3, 2, 1)).reshape(-1)
    ya = _fox_attention(base_tab, fend_tab, q, kt, kaugt, v, qaug, za)

    abr, abi, bbr, bbi = _s5_params(a_re[0], a_im[0], log_dt[0],
                                    jnp.swapaxes(b_re[0], 1, 2), jnp.swapaxes(b_im[0], 1, 2))
    bblk = jnp.concatenate([_block_diag_in(bbr), _block_diag_in(bbi)], axis=2).astype(_BF16)
    cblk = jnp.concatenate([_block_diag_out(c_re[0]), -_block_diag_out(c_im[0])], axis=1).astype(_BF16)
    yb = _s5_scan(u, zb, _chain_tiles(abr, bsz), _chain_tiles(abi, bsz), bblk, cblk,
                  d_skip[0], w_glu[0].astype(_BF16), b_glu[0])

    return _out_proj(x, ya, yb, ga, gb, gate, w_up_a[0].astype(_BF16), w_up_b[0].astype(_BF16),
                     w_out[0].astype(_BF16), g_final)
```

```python
import functools
import math

import jax
import jax.numpy as jnp
from jax import lax
from jax.experimental import pallas as pl
from jax.experimental.pallas import tpu as pltpu

D_MODEL = 1024
EPS = 1e-6
FOX_WIDTH = 512
HEAD_DIM = 64
HEADS = 8
HEAD_PAIRS = HEADS // 2
S5_WIDTH = 512
S5_GROUP = 16
S5_GROUPS = 32
S5_STATE = 64

LANES = 128
SUBLANES = 8
VMEM_LIMIT = 56 * 1024 * 1024

S5_HALVES = 2
S5_HALF_CH = S5_WIDTH // S5_HALVES
S5_HALF_STATE = S5_GROUPS * S5_STATE // S5_HALVES
S5_SLABS = S5_HALF_STATE // LANES

NEG_BIG = -1e30
LOG2E = math.log2(math.e)

ATTN_TILE = 512
ATTN_KBLOCK = 512
KB_PER_TILE = ATTN_TILE // ATTN_KBLOCK
ST_QN2, ST_FSTART, ST_KMAX, ST_FEND = 0, 1, 2, 3
ST_PLANES = ST_FEND + KB_PER_TILE
SKIP_LOG2 = 40.0
NORM_SAFETY = 1.02

_F32 = jnp.float32
_BF16 = jnp.bfloat16


def _dot(a, b):
    return jnp.dot(a, b, preferred_element_type=_F32)


def _dot_nt(a, b):
    return lax.dot_general(a, b, (((1,), (1,)), ((), ())), preferred_element_type=_F32)


def _split3(x):
    hi = x.astype(_BF16).astype(_F32)
    r1 = x - hi
    mid = r1.astype(_BF16).astype(_F32)
    lo = (r1 - mid).astype(_BF16).astype(_F32)
    return hi, mid, lo


def _sigmoid(x):
    return 1.0 / (1.0 + jnp.exp(-x))


def _silu(x):
    return x * _sigmoid(x)


def _adaln_kernel(ct_ref, w_ref, b_ref, o_ref):
    d, bn = w_ref.shape
    w3 = w_ref[...].reshape(d // SUBLANES, SUBLANES, bn)
    rows = []
    for b in range(ct_ref.shape[1]):
        cb = ct_ref[:, b:b + 1].reshape(d // SUBLANES, SUBLANES, 1)
        part = jnp.sum(w3 * cb, axis=0)
        rows.append(jnp.sum(part, axis=0, keepdims=True))
    o_ref[...] = jnp.concatenate(rows, axis=0) + b_ref[...]


def _adaln_mod(c, w, b):
    bsz, d = c.shape
    n = w.shape[1]
    bn = 1024
    return pl.pallas_call(
        _adaln_kernel,
        grid=(n // bn,),
        in_specs=[pl.BlockSpec((d, bsz), lambda j: (0, 0)),
                  pl.BlockSpec((d, bn), lambda j: (0, j)),
                  pl.BlockSpec((1, bn), lambda j: (0, j))],
        out_specs=pl.BlockSpec((bsz, bn), lambda j: (0, j)),
        out_shape=jax.ShapeDtypeStruct((bsz, n), _F32),
        name="adaln_mod",
    )(c.T, w, b.reshape(1, n))


def _s5_param_kernel(are_ref, aim_ref, ldt_ref, bre_ref, bim_ref,
                     abr_ref, abi_ref, bbr_ref, bbi_ref):
    lr = are_ref[...]
    li = aim_ref[...]
    dt = jnp.exp(ldt_ref[...])
    mag = jnp.exp(lr * dt)
    ang = li * dt
    abr = mag * jnp.cos(ang)
    abi = mag * jnp.sin(ang)
    abr_ref[...] = abr
    abi_ref[...] = abi
    nr = abr - 1.0
    ni = abi
    den = lr * lr + li * li
    cr = (nr * lr + ni * li) / den
    ci = (ni * lr - nr * li) / den
    cr3 = cr[:, None, :]
    ci3 = ci[:, None, :]
    br = bre_ref[...]
    bi = bim_ref[...]
    bbr_ref[...] = cr3 * br - ci3 * bi
    bbi_ref[...] = cr3 * bi + ci3 * br


def _s5_params(a_re, a_im, log_dt, b_re_t, b_im_t):
    g, p = a_re.shape
    cg = b_re_t.shape[1]
    return pl.pallas_call(
        _s5_param_kernel,
        out_shape=(jax.ShapeDtypeStruct((g, p), _F32), jax.ShapeDtypeStruct((g, p), _F32),
                   jax.ShapeDtypeStruct((g, cg, p), _F32), jax.ShapeDtypeStruct((g, cg, p), _F32)),
        name="s5_params",
    )(a_re, a_im, log_dt.reshape(g, 1), b_re_t, b_im_t)


def _in_proj_kernel(x_ref, gn_ref, sc_ref, sh_ref, wa_ref, wt_ref, bf_ref,
                    q_ref, kt_ref, v_ref, qaug_ref, kaugt_ref, za_ref, u_ref, zb_ref, ga_ref, gb_ref,
                    stats_ref, carry_ref, kmax_ref, *, tm):
    i = pl.program_id(1)
    fw = FOX_WIDTH
    c_v, c_f = 2 * fw, 3 * fw
    c_za = 0
    c_u, c_zb = c_za + fw, c_za + fw + S5_WIDTH
    c_ga = c_zb + S5_WIDTH
    c_gb = c_ga + D_MODEL

    @pl.when(i == 0)
    def _():
        carry_ref[...] = jnp.zeros_like(carry_ref)
        kmax_ref[...] = jnp.zeros_like(kmax_ref)

    x = x_ref[0]
    ms = jnp.mean(x * x, axis=-1, keepdims=True)
    h = x * lax.rsqrt(ms + EPS) * gn_ref[...]
    h = (h * (1.0 + sc_ref[0]) + sh_ref[0]).astype(_BF16)

    w = FOX_WIDTH
    qb = (_dot_nt(h, wa_ref[0:fw, :]) * (HEAD_DIM ** -0.5 * LOG2E)).astype(_BF16)
    q_ref[0] = qb
    ktb = _dot_nt(wa_ref[fw:c_v, :], h).astype(_BF16)
    kt_ref[0] = ktb
    v_ref[0] = _dot_nt(h, wa_ref[c_v:c_f, :]).astype(_BF16)

    seg = (lax.broadcasted_iota(jnp.int32, (w, LANES), 0) // HEAD_DIM
           == lax.broadcasted_iota(jnp.int32, (w, LANES), 1)).astype(_BF16)
    seg_t = (lax.broadcasted_iota(jnp.int32, (LANES, w), 1) // HEAD_DIM
             == lax.broadcasted_iota(jnp.int32, (LANES, w), 0)).astype(_BF16)
    qf = qb.astype(_F32)
    kf = ktb.astype(_F32)
    qn2 = _dot((qf * qf).astype(_BF16), seg)
    kn2 = _dot(seg_t, (kf * kf).astype(_BF16)).T
    kmax = jnp.maximum(kmax_ref[...], jnp.max(kn2, axis=0, keepdims=True))
    kmax_ref[...] = kmax
    za_ref[0] = _silu(_dot_nt(h, wt_ref[c_za:c_u, :])).astype(_BF16)
    u_ref[0] = _dot_nt(h, wt_ref[c_u:c_zb, :]).astype(_BF16)
    zb_ref[0] = _silu(_dot_nt(h, wt_ref[c_zb:c_ga, :])).astype(_BF16)
    ga_ref[0] = _sigmoid(_dot_nt(h, wt_ref[c_ga:c_gb, :])).astype(_BF16)
    gb_ref[0] = _sigmoid(_dot_nt(h, wt_ref[c_gb:c_gb + D_MODEL, :])).astype(_BF16)

    flt = _dot_nt(wa_ref[c_f:c_f + 2 * HEADS, :], h)[0:HEADS] + bf_ref[...]
    logft = -(jnp.maximum(-flt, 0.0) + jnp.log1p(jnp.exp(-jnp.abs(flt)))) * LOG2E
    lmh_t = jnp.concatenate(_split3(logft), axis=0).astype(_BF16)

    ta = ATTN_TILE
    row = lax.broadcasted_iota(jnp.int32, (ta, ta), 0)
    col = lax.broadcasted_iota(jnp.int32, (ta, ta), 1)
    triu = (row <= col).astype(_BF16)
    ones = jnp.ones((HEADS, ta), _F32)
    zeros = jnp.zeros((HEADS, ta), _F32)
    nt = tm // ta
    carry = carry_ref[...]
    for t in range(nt):
        r0 = t * ta
        parts = _dot(lmh_t[:, r0:r0 + ta], triu)
        cum_t = (parts[0:HEADS] + parts[HEADS:2 * HEADS]) + parts[2 * HEADS:3 * HEADS] + carry[:, 0:1]
        carry = jnp.broadcast_to(cum_t[:, ta - 1:ta], (HEADS, LANES))

        kh, km, kl = _split3(cum_t)
        kaugt_ref[0, :, r0:r0 + ta] = jnp.concatenate(
            [ones, ones, ones, zeros, kh, km, kl] + [zeros] * 9, axis=0).astype(_BF16)
        cum = jnp.concatenate([cum_t] + [zeros] * (LANES // HEADS - 1), axis=0).T
        qh, qm, ql = _split3(cum)
        qaug_ref[0, r0:r0 + ta, :] = (qh + pltpu.roll(qm, HEADS, 1)
                                      + pltpu.roll(ql, 2 * HEADS, 1)).astype(_BF16)

        tile = i * nt + t
        stats_ref[0, ST_QN2, pl.ds(tile, 1), :] = jnp.max(qn2[r0:r0 + ta], axis=0, keepdims=True)
        stats_ref[0, ST_FSTART, pl.ds(tile, 1), :] = cum[0:1, :]
        for c in range(KB_PER_TILE):
            last = (c + 1) * ATTN_KBLOCK - 1
            stats_ref[0, ST_FEND + c, pl.ds(tile, 1), :] = cum[last:last + 1, :]
    carry_ref[...] = carry
    stats_ref[0, ST_KMAX] = jnp.broadcast_to(kmax, stats_ref.shape[2:])


def _in_proj(x, g_norm, scale, shift, wt_a, wt_b, b_f, tm=1024):
    bsz, s, d = x.shape
    ni = s // tm
    tok = lambda width, dt: jax.ShapeDtypeStruct((bsz, s, width), dt)
    tok_spec = lambda width: pl.BlockSpec((1, tm, width), lambda b, i: (b, i, 0))
    tr = lambda rows: jax.ShapeDtypeStruct((bsz, rows, s), _BF16)
    tr_spec = lambda rows: pl.BlockSpec((1, rows, tm), lambda b, i: (b, 0, i))
    const = lambda shape: pl.BlockSpec(shape, lambda b, i: (0,) * len(shape), pipeline_mode=pl.Buffered(1))
    per_b = pl.BlockSpec((1, 1, d), lambda b, i: (b, 0, 0))
    return pl.pallas_call(
        functools.partial(_in_proj_kernel, tm=tm),
        grid=(bsz, ni),
        in_specs=[tok_spec(d), const((1, d)), per_b, per_b,
                  const(wt_a.shape), const(wt_b.shape), const((HEADS, 1))],
        out_specs=[tok_spec(FOX_WIDTH), tr_spec(FOX_WIDTH), tok_spec(FOX_WIDTH),
                   tok_spec(LANES), tr_spec(LANES),
                   tok_spec(FOX_WIDTH), tok_spec(S5_WIDTH), tok_spec(S5_WIDTH),
                   tok_spec(d), tok_spec(d),
                   pl.BlockSpec((1, ST_PLANES, s // ATTN_TILE, LANES), lambda b, i: (b, 0, 0, 0))],
        out_shape=[tok(FOX_WIDTH, _BF16), tr(FOX_WIDTH), tok(FOX_WIDTH, _BF16),
                   tok(LANES, _BF16), tr(LANES),
                   tok(FOX_WIDTH, _BF16), tok(S5_WIDTH, _BF16), tok(S5_WIDTH, _BF16),
                   tok(d, _BF16), tok(d, _BF16),
                   jax.ShapeDtypeStruct((bsz, ST_PLANES, s // ATTN_TILE, LANES), _F32)],
        scratch_shapes=[pltpu.VMEM((HEADS, LANES), _F32), pltpu.VMEM((1, LANES), _F32)],
        compiler_params=pltpu.CompilerParams(
            dimension_semantics=("arbitrary", "arbitrary"), vmem_limit_bytes=VMEM_LIMIT),
        name="in_proj",
    )(x, g_norm.reshape(1, d), scale, shift, wt_a, wt_b, b_f.reshape(HEADS, 1))


def _plan_kernel(st_ref, o_ref):
    qn2 = st_ref[0, ST_QN2]
    fstart = st_ref[0, ST_FSTART]
    kmax2 = st_ref[0, ST_KMAX]
    o_ref[0] = fstart + (NORM_SAFETY * jnp.sqrt(qn2 * kmax2) + SKIP_LOG2)


def _skip_plan(stats):
    bsz, _, nq, _ = stats.shape
    return pl.pallas_call(
        _plan_kernel,
        grid=(bsz,),
        in_specs=[pl.BlockSpec((1, ST_PLANES, nq, LANES), lambda b: (b, 0, 0, 0))],
        out_specs=pl.BlockSpec((1, nq, LANES), lambda b: (b, 0, 0)),
        out_shape=jax.ShapeDtypeStruct((bsz, nq, LANES), _F32),
        name="skip_plan",
    )(stats)


def _attn_kernel(base_ref, fend_ref, q_ref, kt_ref, kaugt_ref, v_ref, qaug_ref, za_ref, o_ref,
                 lhs_ref, s_ref, p_ref, m_ref, alpha_ref, acc_ref, *, tq, tk, rb, nq):
    b = pl.program_id(0)
    hp = pl.program_id(1)
    i = pl.program_id(2)
    lane = lax.broadcasted_iota(jnp.int32, (1, LANES), 1)
    q = q_ref[0]
    qa = qaug_ref[0]

    head_lanes = [(lane >= hh * HEAD_DIM) & (lane < (hh + 1) * HEAD_DIM) for hh in range(2)]
    for hh in range(2):
        h = 2 * hp + hh
        f_lanes = (lane == h) | (lane == HEADS + h) | (lane == 2 * HEADS + h)
        neg_lanes = (lane == 4 * HEADS + h) | (lane == 5 * HEADS + h) | (lane == 6 * HEADS + h)
        qm = jnp.where(head_lanes[hh], q, jnp.zeros_like(q))
        qaug = jnp.where(f_lanes, qa, jnp.where(neg_lanes, -1.0, 0.0).astype(_BF16))
        lhs_ref[hh] = jnp.concatenate([qm, qaug], axis=1)
        m_ref[hh] = jnp.full((tq, LANES), NEG_BIG, _F32)
        acc_ref[hh] = jnp.zeros((tq, LANES), _F32)

    def step(j, masked):
        k0 = pl.multiple_of(j * tk, tk)
        w = jnp.concatenate([kt_ref[0, :, pl.ds(k0, tk)], kaugt_ref[0, :, pl.ds(k0, tk)]], axis=0)
        vb = v_ref[0, pl.ds(k0, tk), :]
        for hh in range(2):
            s = _dot(lhs_ref[hh], w)
            if masked:
                r = lax.broadcasted_iota(jnp.int32, (tq, tk), 0)
                c = lax.broadcasted_iota(jnp.int32, (tq, tk), 1)
                s = jnp.where(c - r <= i * tq - j * tk, s, NEG_BIG)
            s_ref[hh] = s
        for hh in range(2):
            for r0 in range(0, tq, rb):
                chunks = [s_ref[hh, r0:r0 + rb, c0:c0 + LANES] for c0 in range(0, tk, LANES)]
                m_old = m_ref[hh, r0:r0 + rb, :]
                m_new = jnp.maximum(m_old, jnp.max(functools.reduce(jnp.maximum, chunks), axis=1, keepdims=True))
                alpha_ref[hh, r0:r0 + rb, :] = jnp.exp2(m_old - m_new)
                m_ref[hh, r0:r0 + rb, :] = m_new
                for c0, ch in zip(range(0, tk, LANES), chunks):
                    p_ref[hh, r0:r0 + rb, c0:c0 + LANES] = jnp.exp2(ch - m_new).astype(_BF16)
        for hh in range(2):
            sum_lane = (1 - hh) * HEAD_DIM
            vaug = jnp.where(head_lanes[hh], vb, jnp.where(lane == sum_lane, 1.0, 0.0).astype(_BF16))
            acc_ref[hh] = alpha_ref[hh] * acc_ref[hh] + _dot(p_ref[hh], vaug)

    kpt = tq // tk
    j_diag = i * kpt

    @pl.when(i == 0)
    def _():
        for c in range(kpt):
            step(j_diag + c, True)

    @pl.when(i > 0)
    def _():
        for c in range(kpt):
            step(j_diag + c, True)
        step(j_diag - 1, False)

    j_end = jnp.maximum(j_diag - 1, 0)

    def first_needed(hh):
        thr = base_ref[(b * HEADS + 2 * hp + hh) * nq + i] - jnp.min(m_ref[hh])
        row = (b * HEADS + 2 * hp + hh) * (nq * kpt)
        count = jnp.int32(0)
        for jj in range(nq * kpt):
            count += ((fend_ref[row + jj] >= thr) & (jj < j_end)).astype(jnp.int32)
        return count

    def step_one(hd, js):
        lanes_h = (lane >= hd * HEAD_DIM) & (lane < (hd + 1) * HEAD_DIM)
        sum_lane = (1 - hd) * HEAD_DIM
        lhs = lhs_ref[hd]
        vbs = []
        for n, j in enumerate(js):
            k0 = pl.multiple_of(j * tk, tk)
            w = jnp.concatenate([kt_ref[0, :, pl.ds(k0, tk)], kaugt_ref[0, :, pl.ds(k0, tk)]], axis=0)
            vbs.append(v_ref[0, pl.ds(k0, tk), :])
            s_ref[n] = _dot(lhs, w)
        for n in range(len(js)):
            for r0 in range(0, tq, rb):
                chunks = [s_ref[n, r0:r0 + rb, c0:c0 + LANES] for c0 in range(0, tk, LANES)]
                m_old = m_ref[hd, r0:r0 + rb, :]
                m_new = jnp.maximum(m_old, jnp.max(functools.reduce(jnp.maximum, chunks), axis=1, keepdims=True))
                alpha_ref[hd, r0:r0 + rb, :] = jnp.exp2(m_old - m_new)
                m_ref[hd, r0:r0 + rb, :] = m_new
                for c0, ch in zip(range(0, tk, LANES), chunks):
                    p_ref[n, r0:r0 + rb, c0:c0 + LANES] = jnp.exp2(ch - m_new).astype(_BF16)
            vaug = jnp.where(lanes_h, vbs[n], jnp.where(lane == sum_lane, 1.0, 0.0).astype(_BF16))
            acc_ref[hd] = alpha_ref[hd] * acc_ref[hd] + _dot(p_ref[n], vaug)

    jf0 = first_needed(0)
    jf1 = first_needed(1)
    j_lo = jnp.minimum(jf0, jf1)
    j_first = jnp.maximum(jf0, jf1)
    slow = (jf1 < jf0).astype(jnp.int32)
    n_one = j_first - j_lo

    @pl.when((n_one & 1) == 1)
    def _():
        step_one(slow, [j_lo])

    j_lo2 = j_lo + (n_one & 1)

    def pair_one(t, carry):
        step_one(slow, [j_lo2 + 2 * t, j_lo2 + 2 * t + 1])
        return carry

    lax.fori_loop(0, n_one >> 1, pair_one, 0)

    n_old = j_end - j_first

    @pl.when((n_old & 1) == 1)
    def _():
        step(j_first, False)

    j_2 = j_first + (n_old & 1)

    @pl.when((n_old & 2) == 2)
    def _():
        step(j_2, False)
        step(j_2 + 1, False)

    j_4 = j_2 + (n_old & 2)

    def quad_step(t, carry):
        for c in range(4):
            step(j_4 + 4 * t + c, False)
        return carry

    lax.fori_loop(0, n_old >> 2, quad_step, 0)

    a0 = acc_ref[0]
    a1 = acc_ref[1]
    l0 = a0[:, HEAD_DIM:HEAD_DIM + 1]
    l1 = a1[:, 0:1]
    y = jnp.where(lane < HEAD_DIM, a0 / l0, a1 / l1)
    o_ref[0] = (y * za_ref[0].astype(_F32)).astype(_BF16)


def _fox_attention(base, fend, q, kt, kaugt, v, qaug, za, rb=32):
    bsz, s, _ = q.shape
    tq, tk = ATTN_TILE, ATTN_KBLOCK
    nq = s // tq
    qspec = pl.BlockSpec((1, tq, LANES), lambda b, hp, i, *_: (b, i, hp))
    grid_spec = pltpu.PrefetchScalarGridSpec(
        num_scalar_prefetch=2,
        grid=(bsz, HEAD_PAIRS, nq),
        in_specs=[qspec,
                  pl.BlockSpec((1, LANES, s), lambda b, hp, i, *_: (b, hp, 0)),
                  pl.BlockSpec((1, LANES, s), lambda b, hp, i, *_: (b, 0, 0)),
                  pl.BlockSpec((1, s, LANES), lambda b, hp, i, *_: (b, 0, hp)),
                  pl.BlockSpec((1, tq, LANES), lambda b, hp, i, *_: (b, i, 0)),
                  qspec],
        out_specs=qspec,
        scratch_shapes=[pltpu.VMEM((2, tq, 2 * LANES), _BF16),
                        pltpu.VMEM((2, tq, tk), _F32), pltpu.VMEM((2, tq, tk), _BF16),
                        pltpu.VMEM((2, tq, LANES), _F32), pltpu.VMEM((2, tq, LANES), _F32),
                        pltpu.VMEM((2, tq, LANES), _F32)])
    return pl.pallas_call(
        functools.partial(_attn_kernel, tq=tq, tk=tk, rb=rb, nq=nq),
        grid_spec=grid_spec,
        out_shape=jax.ShapeDtypeStruct((bsz, s, FOX_WIDTH), _BF16),
        compiler_params=pltpu.CompilerParams(
            dimension_semantics=("arbitrary", "arbitrary", "arbitrary"), vmem_limit_bytes=VMEM_LIMIT),
        name="fox_attn",
    )(base, fend, q, kt, kaugt, v, qaug, za)


def _gelu_tanh(x):
    return 0.5 * x * (1.0 + jnp.tanh(math.sqrt(2.0 / math.pi) * (x + 0.044715 * (x * x * x))))


def _s5_kernel(u_ref, zb_ref, ar_ref, ai_ref, bblk_ref, cblk_ref, dskip_ref, wglu_ref, bglu_ref,
               o_ref, xr_ref, xi_ref, sr_ref, si_ref, *, tt, nbatch, slabs_per_pass):
    t = pl.program_id(0)
    nchain = nbatch * S5_HALVES

    @pl.when(t == 0)
    def _():
        sr_ref[...] = jnp.zeros_like(sr_ref)
        si_ref[...] = jnp.zeros_like(si_ref)

    for hf in range(S5_HALVES):
        ub = jnp.concatenate([u_ref[b, :, hf * S5_HALF_CH:(hf + 1) * S5_HALF_CH] for b in range(nbatch)],
                             axis=0)
        bu = _dot(ub, bblk_ref[hf])
        for b in range(nbatch):
            kc = b * S5_HALVES + hf
            rows = slice(b * tt, (b + 1) * tt)
            for j in range(S5_SLABS):
                xr_ref[j, pl.ds(kc, tt, stride=nchain), :] = bu[rows, j * LANES:(j + 1) * LANES]
                xi_ref[j, pl.ds(kc, tt, stride=nchain), :] = bu[rows, (S5_SLABS + j) * LANES:
                                                                (S5_SLABS + j + 1) * LANES]

    for j0 in range(0, S5_SLABS, slabs_per_pass):
        js = list(range(j0, j0 + slabs_per_pass))
        ars = [ar_ref[j] for j in js]
        ais = [ai_ref[j] for j in js]

        def body(s, carry, js=js, ars=ars, ais=ais):
            r0 = pl.multiple_of(s * nchain, nchain)
            new = []
            for n, j in enumerate(js):
                sr, si = carry[2 * n], carry[2 * n + 1]
                br = xr_ref[j, pl.ds(r0, nchain), :]
                bi = xi_ref[j, pl.ds(r0, nchain), :]
                nr = (ars[n] * sr - ais[n] * si) + br
                ni = (ars[n] * si + ais[n] * sr) + bi
                xr_ref[j, pl.ds(r0, nchain), :] = nr
                xi_ref[j, pl.ds(r0, nchain), :] = ni
                new += [nr, ni]
            return tuple(new)

        init = []
        for j in js:
            init += [sr_ref[j], si_ref[j]]
        fin = lax.fori_loop(0, tt, body, tuple(init), unroll=4)
        for n, j in enumerate(js):
            sr_ref[j] = fin[2 * n]
            si_ref[j] = fin[2 * n + 1]

    halves = []
    for hf in range(S5_HALVES):
        xall = []
        for b in range(nbatch):
            kc = b * S5_HALVES + hf
            xs = [xr_ref[j, pl.ds(kc, tt, stride=nchain), :] for j in range(S5_SLABS)]
            xs += [xi_ref[j, pl.ds(kc, tt, stride=nchain), :] for j in range(S5_SLABS)]
            xall.append(jnp.concatenate(xs, axis=1).astype(_BF16))
        halves.append(_dot(jnp.concatenate(xall, axis=0), cblk_ref[hf]))
    w = u_ref.shape[-1]
    y = jnp.concatenate(halves, axis=1)
    y = y + dskip_ref[...] * u_ref[...].reshape(nbatch * tt, w).astype(_F32)
    y = _gelu_tanh(y)
    y = y * _sigmoid(_dot(y.astype(_BF16), wglu_ref[...]) + bglu_ref[...])
    y = y * zb_ref[...].reshape(nbatch * tt, w).astype(_F32)
    o_ref[...] = y.astype(_BF16).reshape(nbatch, tt, w)


def _s5_scan(u, zb, ar_t, ai_t, bblk, cblk, d_skip, w_glu, b_glu, tt=256, slabs_per_pass=8):
    bsz, s, w = u.shape
    nchain = bsz * S5_HALVES
    assert nchain == SUBLANES, "the scan packs batch x state-half chains into the 8 sublanes"
    tok_spec = pl.BlockSpec((bsz, tt, w), lambda t: (0, t, 0))
    const = lambda shape: pl.BlockSpec(shape, lambda t: (0,) * len(shape), pipeline_mode=pl.Buffered(1))
    return pl.pallas_call(
        functools.partial(_s5_kernel, tt=tt, nbatch=bsz, slabs_per_pass=slabs_per_pass),
        grid=(s // tt,),
        in_specs=[tok_spec, tok_spec, const(ar_t.shape), const(ai_t.shape),
                  const(bblk.shape), const(cblk.shape), const((1, w)), const(w_glu.shape), const((1, w))],
        out_specs=tok_spec,
        out_shape=jax.ShapeDtypeStruct((bsz, s, w), _BF16),
        scratch_shapes=[pltpu.VMEM((S5_SLABS, tt * nchain, LANES), _F32),
                        pltpu.VMEM((S5_SLABS, tt * nchain, LANES), _F32),
                        pltpu.VMEM((S5_SLABS, nchain, LANES), _F32),
                        pltpu.VMEM((S5_SLABS, nchain, LANES), _F32)],
        compiler_params=pltpu.CompilerParams(
            dimension_semantics=("arbitrary",), vmem_limit_bytes=VMEM_LIMIT),
        name="s5_scan",
    )(u, zb, ar_t, ai_t, bblk, cblk, d_skip.reshape(1, w), w_glu, b_glu.reshape(1, w))


def _out_kernel(x_ref, ya_ref, yb_ref, ga_ref, gb_ref, gate_ref, wua_ref, wub_ref, wo_ref, gf_ref, o_ref):
    ma = ga_ref[0].astype(_F32) * _dot(ya_ref[0], wua_ref[...])
    mb = gb_ref[0].astype(_F32) * _dot(yb_ref[0], wub_ref[...])
    merged = (ma + mb).astype(_BF16)
    xn = x_ref[0] + gate_ref[0] * _dot(merged, wo_ref[...])
    ms = jnp.mean(xn * xn, axis=-1, keepdims=True)
    o_ref[0] = xn * lax.rsqrt(ms + EPS) * gf_ref[...]


def _out_proj(x, ya, yb, ga, gb, gate, w_up_a, w_up_b, w_out, g_final, tm=1024):
    bsz, s, d = x.shape
    tok_spec = lambda width: pl.BlockSpec((1, tm, width), lambda b, i: (b, i, 0))
    const = lambda shape: pl.BlockSpec(shape, lambda b, i: (0,) * len(shape), pipeline_mode=pl.Buffered(1))
    return pl.pallas_call(
        _out_kernel,
        grid=(bsz, s // tm),
        in_specs=[tok_spec(d), tok_spec(FOX_WIDTH), tok_spec(S5_WIDTH), tok_spec(d), tok_spec(d),
                  pl.BlockSpec((1, 1, d), lambda b, i: (b, 0, 0)),
                  const(w_up_a.shape), const(w_up_b.shape), const(w_out.shape), const((1, d))],
        out_specs=tok_spec(d),
        out_shape=jax.ShapeDtypeStruct((bsz, s, d), x.dtype),
        compiler_params=pltpu.CompilerParams(
            dimension_semantics=("arbitrary", "arbitrary"), vmem_limit_bytes=VMEM_LIMIT),
        name="out_proj",
    )(x, ya, yb, ga, gb, gate, w_up_a, w_up_b, w_out, g_final.reshape(1, d))


def _block_diag_in(bb):
    gh = S5_GROUPS // S5_HALVES
    bb = bb.reshape(S5_HALVES, gh, S5_GROUP, S5_STATE)
    eye = jnp.eye(gh, dtype=bb.dtype)
    out = bb[:, :, :, None, :] * eye[None, :, None, :, None]
    return out.reshape(S5_HALVES, gh * S5_GROUP, gh * S5_STATE)


def _block_diag_out(cc):
    gh = S5_GROUPS // S5_HALVES
    cc = cc.reshape(S5_HALVES, gh, S5_GROUP, S5_STATE)
    eye = jnp.eye(gh, dtype=cc.dtype)
    out = jnp.swapaxes(cc, 2, 3)[:, :, :, None, :] * eye[None, :, None, :, None]
    return out.reshape(S5_HALVES, gh * S5_STATE, gh * S5_GROUP)


def _chain_tiles(a, nbatch):
    a = a.reshape(S5_HALVES, S5_SLABS, LANES)
    a = jnp.swapaxes(a, 0, 1)
    return jnp.tile(a, (1, nbatch, 1))


def kernel(x, c, w_ada, b_ada, g_norm, w_in, b_f, a_re, a_im, log_dt, b_re, b_im, c_re, c_im,
           d_skip, w_glu, b_glu, w_up_a, w_up_b, w_out, g_final):
    bsz, s, d = x.shape
    depth = w_ada.shape[0]
    assert depth == 1 and d == D_MODEL

    mod = _adaln_mod(c, w_ada[0], b_ada[0])
    shift = mod[:, 0:d].reshape(bsz, 1, d)
    scale = mod[:, d:2 * d].reshape(bsz, 1, d)
    gate = mod[:, 2 * d:3 * d].reshape(bsz, 1, d)

    wt = jnp.swapaxes(w_in[0], 0, 1)
    o_z = 3 * FOX_WIDTH + HEADS
    wt_a = jnp.pad(wt[0:o_z], ((0, HEADS), (0, 0))).astype(_BF16)
    wt_b = wt[o_z:].astype(_BF16)

    q, kt, v, qaug, kaugt, za, u, zb, ga, gb, stats = _in_proj(
        x, g_norm[0], scale, shift, wt_a, wt_b, b_f[0])

    base = _skip_plan(stats)
    base_tab = jnp.swapaxes(base[:, :, 0:HEADS], 1, 2).reshape(-1)
    fend = stats[:, ST_FEND:ST_FEND + KB_PER_TILE, :, 0:HEADS]
    fend_tab = jnp.transpose(fend, (0, 3, 2, 1)).reshape(-1)
    ya = _fox_attention(base_tab, fend_tab, q, kt, kaugt, v, qaug, za)

    abr, abi, bbr, bbi = _s5_params(a_re[0], a_im[0], log_dt[0],
                                    jnp.swapaxes(b_re[0], 1, 2), jnp.swapaxes(b_im[0], 1, 2))
    bblk = jnp.concatenate([_block_diag_in(bbr), _block_diag_in(bbi)], axis=2).astype(_BF16)
    cblk = jnp.concatenate([_block_diag_out(c_re[0]), -_block_diag_out(c_im[0])], axis=1).astype(_BF16)
    yb = _s5_scan(u, zb, _chain_tiles(abr, bsz), _chain_tiles(abi, bsz), bblk, cblk,
                  d_skip[0], w_glu[0].astype(_BF16), b_glu[0])

    return _out_proj(x, ya, yb, ga, gb, gate, w_up_a[0].astype(_BF16), w_up_b[0].astype(_BF16),
                     w_out[0].astype(_BF16), g_final)
```

```python
import functools
import math

import jax
import jax.numpy as jnp
from jax import lax
from jax.experimental import pallas as pl
from jax.experimental.pallas import tpu as pltpu

D_MODEL = 1024
EPS = 1e-6
FOX_WIDTH = 512
HEAD_DIM = 64
HEADS = 8
HEAD_PAIRS = HEADS // 2
S5_WIDTH = 512
S5_GROUP = 16
S5_GROUPS = 32
S5_STATE = 64

LANES = 128
SUBLANES = 8
VMEM_LIMIT = 56 * 1024 * 1024

S5_HALVES = 2
S5_HALF_CH = S5_WIDTH // S5_HALVES
S5_HALF_STATE = S5_GROUPS * S5_STATE // S5_HALVES
S5_SLABS = S5_HALF_STATE // LANES

NEG_BIG = -1e30
LOG2E = math.log2(math.e)

ATTN_TILE = 512
ATTN_KBLOCK = 512
KB_PER_TILE = ATTN_TILE // ATTN_KBLOCK
ST_QN2, ST_FSTART, ST_KMAX, ST_FEND = 0, 1, 2, 3
ST_PLANES = ST_FEND + KB_PER_TILE
SKIP_LOG2 = 40.0
NORM_SAFETY = 1.02

_F32 = jnp.float32
_BF16 = jnp.bfloat16


def _dot(a, b):
    return jnp.dot(a, b, preferred_element_type=_F32)


def _dot_nt(a, b):
    return lax.dot_general(a, b, (((1,), (1,)), ((), ())), preferred_element_type=_F32)


def _split3(x):
    hi = x.astype(_BF16).astype(_F32)
    r1 = x - hi
    mid = r1.astype(_BF16).astype(_F32)
    lo = (r1 - mid).astype(_BF16).astype(_F32)
    return hi, mid, lo


def _sigmoid(x):
    return 1.0 / (1.0 + jnp.exp(-x))


def _silu(x):
    return x * _sigmoid(x)


def _adaln_kernel(ct_ref, w_ref, b_ref, o_ref):
    d, bn = w_ref.shape
    w3 = w_ref[...].reshape(d // SUBLANES, SUBLANES, bn)
    rows = []
    for b in range(ct_ref.shape[1]):
        cb = ct_ref[:, b:b + 1].reshape(d // SUBLANES, SUBLANES, 1)
        part = jnp.sum(w3 * cb, axis=0)
        rows.append(jnp.sum(part, axis=0, keepdims=True))
    o_ref[...] = jnp.concatenate(rows, axis=0) + b_ref[...]


def _adaln_mod(c, w, b):
    bsz, d = c.shape
    n = w.shape[1]
    bn = 1024
    return pl.pallas_call(
        _adaln_kernel,
        grid=(n // bn,),
        in_specs=[pl.BlockSpec((d, bsz), lambda j: (0, 0)),
                  pl.BlockSpec((d, bn), lambda j: (0, j)),
                  pl.BlockSpec((1, bn), lambda j: (0, j))],
        out_specs=pl.BlockSpec((bsz, bn), lambda j: (0, j)),
        out_shape=jax.ShapeDtypeStruct((bsz, n), _F32),
        name="adaln_mod",
    )(c.T, w, b.reshape(1, n))


def _s5_param_kernel(are_ref, aim_ref, ldt_ref, bre_ref, bim_ref,
                     abr_ref, abi_ref, bbr_ref, bbi_ref):
    lr = are_ref[...]
    li = aim_ref[...]
    dt = jnp.exp(ldt_ref[...])
    mag = jnp.exp(lr * dt)
    ang = li * dt
    abr = mag * jnp.cos(ang)
    abi = mag * jnp.sin(ang)
    abr_ref[...] = abr
    abi_ref[...] = abi
    nr = abr - 1.0
    ni = abi
    den = lr * lr + li * li
    cr = (nr * lr + ni * li) / den
    ci = (ni * lr - nr * li) / den
    cr3 = cr[:, None, :]
    ci3 = ci[:, None, :]
    br = bre_ref[...]
    bi = bim_ref[...]
    bbr_ref[...] = cr3 * br - ci3 * bi
    bbi_ref[...] = cr3 * bi + ci3 * br


def _s5_params(a_re, a_im, log_dt, b_re_t, b_im_t):
    g, p = a_re.shape
    cg = b_re_t.shape[1]
    return pl.pallas_call(
        _s5_param_kernel,
        out_shape=(jax.ShapeDtypeStruct((g, p), _F32), jax.ShapeDtypeStruct((g, p), _F32),
                   jax.ShapeDtypeStruct((g, cg, p), _F32), jax.ShapeDtypeStruct((g, cg, p), _F32)),
        name="s5_params",
    )(a_re, a_im, log_dt.reshape(g, 1), b_re_t, b_im_t)


def _in_proj_kernel(x_ref, gn_ref, sc_ref, sh_ref, wa_ref, wt_ref, bf_ref,
                    q_ref, kt_ref, v_ref, qaug_ref, kaugt_ref, za_ref, u_ref, zb_ref, ga_ref, gb_ref,
                    stats_ref, carry_ref, kmax_ref, *, tm):
    i = pl.program_id(1)
    fw = FOX_WIDTH
    c_v, c_f = 2 * fw, 3 * fw
    c_za = 0
    c_u, c_zb = c_za + fw, c_za + fw + S5_WIDTH
    c_ga = c_zb + S5_WIDTH
    c_gb = c_ga + D_MODEL

    @pl.when(i == 0)
    def _():
        carry_ref[...] = jnp.zeros_like(carry_ref)
        kmax_ref[...] = jnp.zeros_like(kmax_ref)

    x = x_ref[0]
    ms = jnp.mean(x * x, axis=-1, keepdims=True)
    h = x * lax.rsqrt(ms + EPS) * gn_ref[...]
    h = (h * (1.0 + sc_ref[0]) + sh_ref[0]).astype(_BF16)

    w = FOX_WIDTH
    qb = (_dot_nt(h, wa_ref[0:fw, :]) * (HEAD_DIM ** -0.5 * LOG2E)).astype(_BF16)
    q_ref[0] = qb
    ktb = _dot_nt(wa_ref[fw:c_v, :], h).astype(_BF16)
    kt_ref[0] = ktb
    v_ref[0] = _dot_nt(h, wa_ref[c_v:c_f, :]).astype(_BF16)

    seg = (lax.broadcasted_iota(jnp.int32, (w, LANES), 0) // HEAD_DIM
           == lax.broadcasted_iota(jnp.int32, (w, LANES), 1)).astype(_BF16)
    seg_t = (lax.broadcasted_iota(jnp.int32, (LANES, w), 1) // HEAD_DIM
             == lax.broadcasted_iota(jnp.int32, (LANES, w), 0)).astype(_BF16)
    qf = qb.astype(_F32)
    kf = ktb.astype(_F32)
    qn2 = _dot((qf * qf).astype(_BF16), seg)
    kn2 = _dot(seg_t, (kf * kf).astype(_BF16)).T
    kmax = jnp.maximum(kmax_ref[...], jnp.max(kn2, axis=0, keepdims=True))
    kmax_ref[...] = kmax
    za_ref[0] = _silu(_dot_nt(h, wt_ref[c_za:c_u, :])).astype(_BF16)
    u_ref[0] = _dot_nt(h, wt_ref[c_u:c_zb, :]).astype(_BF16)
    zb_ref[0] = _silu(_dot_nt(h, wt_ref[c_zb:c_ga, :])).astype(_BF16)
    ga_ref[0] = _sigmoid(_dot_nt(h, wt_ref[c_ga:c_gb, :])).astype(_BF16)
    gb_ref[0] = _sigmoid(_dot_nt(h, wt_ref[c_gb:c_gb + D_MODEL, :])).astype(_BF16)

    flt = _dot_nt(wa_ref[c_f:c_f + 2 * HEADS, :], h)[0:HEADS] + bf_ref[...]
    logft = -(jnp.maximum(-flt, 0.0) + jnp.log1p(jnp.exp(-jnp.abs(flt)))) * LOG2E
    lmh_t = jnp.concatenate(_split3(logft), axis=0).astype(_BF16)

    ta = ATTN_TILE
    row = lax.broadcasted_iota(jnp.int32, (ta, ta), 0)
    col = lax.broadcasted_iota(jnp.int32, (ta, ta), 1)
    triu = (row <= col).astype(_BF16)
    ones = jnp.ones((HEADS, ta), _F32)
    zeros = jnp.zeros((HEADS, ta), _F32)
    nt = tm // ta
    carry = carry_ref[...]
    for t in range(nt):
        r0 = t * ta
        parts = _dot(lmh_t[:, r0:r0 + ta], triu)
        cum_t = (parts[0:HEADS] + parts[HEADS:2 * HEADS]) + parts[2 * HEADS:3 * HEADS] + carry[:, 0:1]
        carry = jnp.broadcast_to(cum_t[:, ta - 1:ta], (HEADS, LANES))

        kh, km, kl = _split3(cum_t)
        kaugt_ref[0, :, r0:r0 + ta] = jnp.concatenate(
            [ones, ones, ones, zeros, kh, km, kl] + [zeros] * 9, axis=0).astype(_BF16)
        cum = jnp.concatenate([cum_t] + [zeros] * (LANES // HEADS - 1), axis=0).T
        qh, qm, ql = _split3(cum)
        qaug_ref[0, r0:r0 + ta, :] = (qh + pltpu.roll(qm, HEADS, 1)
                                      + pltpu.roll(ql, 2 * HEADS, 1)).astype(_BF16)

        tile = i * nt + t
        stats_ref[0, ST_QN2, pl.ds(tile, 1), :] = jnp.max(qn2[r0:r0 + ta], axis=0, keepdims=True)
        stats_ref[0, ST_FSTART, pl.ds(tile, 1), :] = cum[0:1, :]
        for c in range(KB_PER_TILE):
            last = (c + 1) * ATTN_KBLOCK - 1
            stats_ref[0, ST_FEND + c, pl.ds(tile, 1), :] = cum[last:last + 1, :]
    carry_ref[...] = carry
    stats_ref[0, ST_KMAX] = jnp.broadcast_to(kmax, stats_ref.shape[2:])


def _in_proj(x, g_norm, scale, shift, wt_a, wt_b, b_f, tm=1024):
    bsz, s, d = x.shape
    ni = s // tm
    tok = lambda width, dt: jax.ShapeDtypeStruct((bsz, s, width), dt)
    tok_spec = lambda width: pl.BlockSpec((1, tm, width), lambda b, i: (b, i, 0))
    tr = lambda rows: jax.ShapeDtypeStruct((bsz, rows, s), _BF16)
    tr_spec = lambda rows: pl.BlockSpec((1, rows, tm), lambda b, i: (b, 0, i))
    const = lambda shape: pl.BlockSpec(shape, lambda b, i: (0,) * len(shape), pipeline_mode=pl.Buffered(1))
    per_b = pl.BlockSpec((1, 1, d), lambda b, i: (b, 0, 0))
    return pl.pallas_call(
        functools.partial(_in_proj_kernel, tm=tm),
        grid=(bsz, ni),
        in_specs=[tok_spec(d), const((1, d)), per_b, per_b,
                  const(wt_a.shape), const(wt_b.shape), const((HEADS, 1))],
        out_specs=[tok_spec(FOX_WIDTH), tr_spec(FOX_WIDTH), tok_spec(FOX_WIDTH),
                   tok_spec(LANES), tr_spec(LANES),
                   tok_spec(FOX_WIDTH), tok_spec(S5_WIDTH), tok_spec(S5_WIDTH),
                   tok_spec(d), tok_spec(d),
                   pl.BlockSpec((1, ST_PLANES, s // ATTN_TILE, LANES), lambda b, i: (b, 0, 0, 0))],
        out_shape=[tok(FOX_WIDTH, _BF16), tr(FOX_WIDTH), tok(FOX_WIDTH, _BF16),
                   tok(LANES, _BF16), tr(LANES),
                   tok(FOX_WIDTH, _BF16), tok(S5_WIDTH, _BF16), tok(S5_WIDTH, _BF16),
                   tok(d, _BF16), tok(d, _BF16),
                   jax.ShapeDtypeStruct((bsz, ST_PLANES, s // ATTN_TILE, LANES), _F32)],
        scratch_shapes=[pltpu.VMEM((HEADS, LANES), _F32), pltpu.VMEM((1, LANES), _F32)],
        compiler_params=pltpu.CompilerParams(
            dimension_semantics=("arbitrary", "arbitrary"), vmem_limit_bytes=VMEM_LIMIT),
        name="in_proj",
    )(x, g_norm.reshape(1, d), scale, shift, wt_a, wt_b, b_f.reshape(HEADS, 1))


def _plan_kernel(st_ref, o_ref):
    qn2 = st_ref[0, ST_QN2]
    fstart = st_ref[0, ST_FSTART]
    kmax2 = st_ref[0, ST_KMAX]
    o_ref[0] = fstart + (NORM_SAFETY * jnp.sqrt(qn2 * kmax2) + SKIP_LOG2)


def _skip_plan(stats):
    bsz, _, nq, _ = stats.shape
    return pl.pallas_call(
        _plan_kernel,
        grid=(bsz,),
        in_specs=[pl.BlockSpec((1, ST_PLANES, nq, LANES), lambda b: (b, 0, 0, 0))],
        out_specs=pl.BlockSpec((1, nq, LANES), lambda b: (b, 0, 0)),
        out_shape=jax.ShapeDtypeStruct((bsz, nq, LANES), _F32),
        name="skip_plan",
    )(stats)


def _attn_kernel(base_ref, fend_ref, q_ref, kt_ref, kaugt_ref, v_ref, qaug_ref, za_ref, o_ref,
                 lhs_ref, s_ref, p_ref, m_ref, alpha_ref, acc_ref, *, tq, tk, rb, nq):
    b = pl.program_id(0)
    hp = pl.program_id(1)
    i = pl.program_id(2)
    lane = lax.broadcasted_iota(jnp.int32, (1, LANES), 1)
    q = q_ref[0]
    qa = qaug_ref[0]

    head_lanes = [(lane >= hh * HEAD_DIM) & (lane < (hh + 1) * HEAD_DIM) for hh in range(2)]
    for hh in range(2):
        h = 2 * hp + hh
        f_lanes = (lane == h) | (lane == HEADS + h) | (lane == 2 * HEADS + h)
        neg_lanes = (lane == 4 * HEADS + h) | (lane == 5 * HEADS + h) | (lane == 6 * HEADS + h)
        qm = jnp.where(head_lanes[hh], q, jnp.zeros_like(q))
        qaug = jnp.where(f_lanes, qa, jnp.where(neg_lanes, -1.0, 0.0).astype(_BF16))
        lhs_ref[hh] = jnp.concatenate([qm, qaug], axis=1)
        m_ref[hh] = jnp.full((tq, LANES), NEG_BIG, _F32)
        acc_ref[hh] = jnp.zeros((tq, LANES), _F32)

    def step(j, masked):
        k0 = pl.multiple_of(j * tk, tk)
        w = jnp.concatenate([kt_ref[0, :, pl.ds(k0, tk)], kaugt_ref[0, :, pl.ds(k0, tk)]], axis=0)
        vb = v_ref[0, pl.ds(k0, tk), :]
        for hh in range(2):
            s = _dot(lhs_ref[hh], w)
            if masked:
                r = lax.broadcasted_iota(jnp.int32, (tq, tk), 0)
                c = lax.broadcasted_iota(jnp.int32, (tq, tk), 1)
                s = jnp.where(c - r <= i * tq - j * tk, s, NEG_BIG)
            s_ref[hh] = s
        for hh in range(2):
            for r0 in range(0, tq, rb):
                chunks = [s_ref[hh, r0:r0 + rb, c0:c0 + LANES] for c0 in range(0, tk, LANES)]
                m_old = m_ref[hh, r0:r0 + rb, :]
                m_new = jnp.maximum(m_old, jnp.max(functools.reduce(jnp.maximum, chunks), axis=1, keepdims=True))
                alpha_ref[hh, r0:r0 + rb, :] = jnp.exp2(m_old - m_new)
                m_ref[hh, r0:r0 + rb, :] = m_new
                for c0, ch in zip(range(0, tk, LANES), chunks):
                    p_ref[hh, r0:r0 + rb, c0:c0 + LANES] = jnp.exp2(ch - m_new).astype(_BF16)
        for hh in range(2):
            sum_lane = (1 - hh) * HEAD_DIM
            vaug = jnp.where(head_lanes[hh], vb, jnp.where(lane == sum_lane, 1.0, 0.0).astype(_BF16))
            acc_ref[hh] = alpha_ref[hh] * acc_ref[hh] + _dot(p_ref[hh], vaug)

    kpt = tq // tk
    j_diag = i * kpt

    for c in range(kpt):
        step(j_diag + c, True)

    def first_needed(hh):
        thr = base_ref[(b * HEADS + 2 * hp + hh) * nq + i] - jnp.min(m_ref[hh])
        row = (b * HEADS + 2 * hp + hh) * (nq * kpt)
        count = jnp.int32(0)
        for jj in range(nq * kpt):
            count += ((fend_ref[row + jj] >= thr) & (jj < j_diag)).astype(jnp.int32)
        return count

    def step_one(hd, js):
        lanes_h = (lane >= hd * HEAD_DIM) & (lane < (hd + 1) * HEAD_DIM)
        sum_lane = (1 - hd) * HEAD_DIM
        lhs = lhs_ref[hd]
        vbs = []
        for n, j in enumerate(js):
            k0 = pl.multiple_of(j * tk, tk)
            w = jnp.concatenate([kt_ref[0, :, pl.ds(k0, tk)], kaugt_ref[0, :, pl.ds(k0, tk)]], axis=0)
            vbs.append(v_ref[0, pl.ds(k0, tk), :])
            s_ref[n] = _dot(lhs, w)
        for n in range(len(js)):
            for r0 in range(0, tq, rb):
                chunks = [s_ref[n, r0:r0 + rb, c0:c0 + LANES] for c0 in range(0, tk, LANES)]
                m_old = m_ref[hd, r0:r0 + rb, :]
                m_new = jnp.maximum(m_old, jnp.max(functools.reduce(jnp.maximum, chunks), axis=1, keepdims=True))
                alpha_ref[hd, r0:r0 + rb, :] = jnp.exp2(m_old - m_new)
                m_ref[hd, r0:r0 + rb, :] = m_new
                for c0, ch in zip(range(0, tk, LANES), chunks):
                    p_ref[n, r0:r0 + rb, c0:c0 + LANES] = jnp.exp2(ch - m_new).astype(_BF16)
            vaug = jnp.where(lanes_h, vbs[n], jnp.where(lane == sum_lane, 1.0, 0.0).astype(_BF16))
            acc_ref[hd] = alpha_ref[hd] * acc_ref[hd] + _dot(p_ref[n], vaug)

    jf0 = first_needed(0)
    jf1 = first_needed(1)
    j_lo = jnp.minimum(jf0, jf1)
    j_first = jnp.maximum(jf0, jf1)
    slow = (jf1 < jf0).astype(jnp.int32)
    n_one = j_first - j_lo

    @pl.when((n_one & 1) == 1)
    def _():
        step_one(slow, [j_lo])

    j_lo2 = j_lo + (n_one & 1)

    def pair_one(t, carry):
        step_one(slow, [j_lo2 + 2 * t, j_lo2 + 2 * t + 1])
        return carry

    lax.fori_loop(0, n_one >> 1, pair_one, 0)

    n_old = j_diag - j_first

    @pl.when((n_old & 1) == 1)
    def _():
        step(j_first, False)

    j_2 = j_first + (n_old & 1)

    @pl.when((n_old & 2) == 2)
    def _():
        step(j_2, False)
        step(j_2 + 1, False)

    j_4 = j_2 + (n_old & 2)

    def quad_step(t, carry):
        for c in range(4):
            step(j_4 + 4 * t + c, False)
        return carry

    lax.fori_loop(0, n_old >> 2, quad_step, 0)

    a0 = acc_ref[0]
    a1 = acc_ref[1]
    l0 = a0[:, HEAD_DIM:HEAD_DIM + 1]
    l1 = a1[:, 0:1]
    y = jnp.where(lane < HEAD_DIM, a0 / l0, a1 / l1)
    o_ref[0] = (y * za_ref[0].astype(_F32)).astype(_BF16)


def _fox_attention(base, fend, q, kt, kaugt, v, qaug, za, rb=32):
    bsz, s, _ = q.shape
    tq, tk = ATTN_TILE, ATTN_KBLOCK
    nq = s // tq
    qspec = pl.BlockSpec((1, tq, LANES), lambda b, hp, i, *_: (b, i, hp))
    grid_spec = pltpu.PrefetchScalarGridSpec(
        num_scalar_prefetch=2,
        grid=(bsz, HEAD_PAIRS, nq),
        in_specs=[qspec,
                  pl.BlockSpec((1, LANES, s), lambda b, hp, i, *_: (b, hp, 0)),
                  pl.BlockSpec((1, LANES, s), lambda b, hp, i, *_: (b, 0, 0)),
                  pl.BlockSpec((1, s, LANES), lambda b, hp, i, *_: (b, 0, hp)),
                  pl.BlockSpec((1, tq, LANES), lambda b, hp, i, *_: (b, i, 0)),
                  qspec],
        out_specs=qspec,
        scratch_shapes=[pltpu.VMEM((2, tq, 2 * LANES), _BF16),
                        pltpu.VMEM((2, tq, tk), _F32), pltpu.VMEM((2, tq, tk), _BF16),
                        pltpu.VMEM((2, tq, LANES), _F32), pltpu.VMEM((2, tq, LANES), _F32),
                        pltpu.VMEM((2, tq, LANES), _F32)])
    return pl.pallas_call(
        functools.partial(_attn_kernel, tq=tq, tk=tk, rb=rb, nq=nq),
        grid_spec=grid_spec,
        out_shape=jax.ShapeDtypeStruct((bsz, s, FOX_WIDTH), _BF16),
        compiler_params=pltpu.CompilerParams(
            dimension_semantics=("arbitrary", "arbitrary", "arbitrary"), vmem_limit_bytes=VMEM_LIMIT),
        name="fox_attn",
    )(base, fend, q, kt, kaugt, v, qaug, za)


def _gelu_tanh(x):
    return 0.5 * x * (1.0 + jnp.tanh(math.sqrt(2.0 / math.pi) * (x + 0.044715 * (x * x * x))))


def _s5_kernel(u_ref, zb_ref, ar_ref, ai_ref, bblk_ref, cblk_ref, dskip_ref, wglu_ref, bglu_ref,
               o_ref, xr_ref, xi_ref, sr_ref, si_ref, *, tt, nbatch, slabs_per_pass):
    t = pl.program_id(0)
    nchain = nbatch * S5_HALVES

    @pl.when(t == 0)
    def _():
        sr_ref[...] = jnp.zeros_like(sr_ref)
        si_ref[...] = jnp.zeros_like(si_ref)

    for hf in range(S5_HALVES):
        ub = jnp.concatenate([u_ref[b, :, hf * S5_HALF_CH:(hf + 1) * S5_HALF_CH] for b in range(nbatch)],
                             axis=0)
        bu = _dot(ub, bblk_ref[hf])
        for b in range(nbatch):
            kc = b * S5_HALVES + hf
            rows = slice(b * tt, (b + 1) * tt)
            for j in range(S5_SLABS):
                xr_ref[j, pl.ds(kc, tt, stride=nchain), :] = bu[rows, j * LANES:(j + 1) * LANES]
                xi_ref[j, pl.ds(kc, tt, stride=nchain), :] = bu[rows, (S5_SLABS + j) * LANES:
                                                                (S5_SLABS + j + 1) * LANES]

    for j0 in range(0, S5_SLABS, slabs_per_pass):
        js = list(range(j0, j0 + slabs_per_pass))
        ars = [ar_ref[j] for j in js]
        ais = [ai_ref[j] for j in js]

        def body(s, carry, js=js, ars=ars, ais=ais):
            r0 = pl.multiple_of(s * nchain, nchain)
            new = []
            for n, j in enumerate(js):
                sr, si = carry[2 * n], carry[2 * n + 1]
                br = xr_ref[j, pl.ds(r0, nchain), :]
                bi = xi_ref[j, pl.ds(r0, nchain), :]
                nr = (ars[n] * sr - ais[n] * si) + br
                ni = (ars[n] * si + ais[n] * sr) + bi
                xr_ref[j, pl.ds(r0, nchain), :] = nr
                xi_ref[j, pl.ds(r0, nchain), :] = ni
                new += [nr, ni]
            return tuple(new)

        init = []
        for j in js:
            init += [sr_ref[j], si_ref[j]]
        fin = lax.fori_loop(0, tt, body, tuple(init), unroll=4)
        for n, j in enumerate(js):
            sr_ref[j] = fin[2 * n]
            si_ref[j] = fin[2 * n + 1]

    halves = []
    for hf in range(S5_HALVES):
        xall = []
        for b in range(nbatch):
            kc = b * S5_HALVES + hf
            xs = [xr_ref[j, pl.ds(kc, tt, stride=nchain), :] for j in range(S5_SLABS)]
            xs += [xi_ref[j, pl.ds(kc, tt, stride=nchain), :] for j in range(S5_SLABS)]
            xall.append(jnp.concatenate(xs, axis=1).astype(_BF16))
        halves.append(_dot(jnp.concatenate(xall, axis=0), cblk_ref[hf]))
    w = u_ref.shape[-1]
    y = jnp.concatenate(halves, axis=1)
    y = y + dskip_ref[...] * u_ref[...].reshape(nbatch * tt, w).astype(_F32)
    y = _gelu_tanh(y)
    y = y * _sigmoid(_dot(y.astype(_BF16), wglu_ref[...]) + bglu_ref[...])
    y = y * zb_ref[...].reshape(nbatch * tt, w).astype(_F32)
    o_ref[...] = y.astype(_BF16).reshape(nbatch, tt, w)


def _s5_scan(u, zb, ar_t, ai_t, bblk, cblk, d_skip, w_glu, b_glu, tt=256, slabs_per_pass=8):
    bsz, s, w = u.shape
    nchain = bsz * S5_HALVES
    assert nchain == SUBLANES, "the scan packs batch x state-half chains into the 8 sublanes"
    tok_spec = pl.BlockSpec((bsz, tt, w), lambda t: (0, t, 0))
    const = lambda shape: pl.BlockSpec(shape, lambda t: (0,) * len(shape), pipeline_mode=pl.Buffered(1))
    return pl.pallas_call(
        functools.partial(_s5_kernel, tt=tt, nbatch=bsz, slabs_per_pass=slabs_per_pass),
        grid=(s // tt,),
        in_specs=[tok_spec, tok_spec, const(ar_t.shape), const(ai_t.shape),
                  const(bblk.shape), const(cblk.shape), const((1, w)), const(w_glu.shape), const((1, w))],
        out_specs=tok_spec,
        out_shape=jax.ShapeDtypeStruct((bsz, s, w), _BF16),
        scratch_shapes=[pltpu.VMEM((S5_SLABS, tt * nchain, LANES), _F32),
                        pltpu.VMEM((S5_SLABS, tt * nchain, LANES), _F32),
                        pltpu.VMEM((S5_SLABS, nchain, LANES), _F32),
                        pltpu.VMEM((S5_SLABS, nchain, LANES), _F32)],
        compiler_params=pltpu.CompilerParams(
            dimension_semantics=("arbitrary",), vmem_limit_bytes=VMEM_LIMIT),
        name="s5_scan",
    )(u, zb, ar_t, ai_t, bblk, cblk, d_skip.reshape(1, w), w_glu, b_glu.reshape(1, w))


def _out_kernel(x_ref, ya_ref, yb_ref, ga_ref, gb_ref, gate_ref, wua_ref, wub_ref, wo_ref, gf_ref, o_ref):
    ma = ga_ref[0].astype(_F32) * _dot(ya_ref[0], wua_ref[...])
    mb = gb_ref[0].astype(_F32) * _dot(yb_ref[0], wub_ref[...])
    merged = (ma + mb).astype(_BF16)
    xn = x_ref[0] + gate_ref[0] * _dot(merged, wo_ref[...])
    ms = jnp.mean(xn * xn, axis=-1, keepdims=True)
    o_ref[0] = xn * lax.rsqrt(ms + EPS) * gf_ref[...]


def _out_proj(x, ya, yb, ga, gb, gate, w_up_a, w_up_b, w_out, g_final, tm=1024):
    bsz, s, d = x.shape
    tok_spec = lambda width: pl.BlockSpec((1, tm, width), lambda b, i: (b, i, 0))
    const = lambda shape: pl.BlockSpec(shape, lambda b, i: (0,) * len(shape), pipeline_mode=pl.Buffered(1))
    return pl.pallas_call(
        _out_kernel,
        grid=(bsz, s // tm),
        in_specs=[tok_spec(d), tok_spec(FOX_WIDTH), tok_spec(S5_WIDTH), tok_spec(d), tok_spec(d),
                  pl.BlockSpec((1, 1, d), lambda b, i: (b, 0, 0)),
                  const(w_up_a.shape), const(w_up_b.shape), const(w_out.shape), const((1, d))],
        out_specs=tok_spec(d),
        out_shape=jax.ShapeDtypeStruct((bsz, s, d), x.dtype),
        compiler_params=pltpu.CompilerParams(
            dimension_semantics=("arbitrary", "arbitrary"), vmem_limit_bytes=VMEM_LIMIT),
        name="out_proj",
    )(x, ya, yb, ga, gb, gate, w_up_a, w_up_b, w_out, g_final.reshape(1, d))


def _block_diag_in(bb):
    gh = S5_GROUPS // S5_HALVES
    bb = bb.reshape(S5_HALVES, gh, S5_GROUP, S5_STATE)
    eye = jnp.eye(gh, dtype=bb.dtype)
    out = bb[:, :, :, None, :] * eye[None, :, None, :, None]
    return out.reshape(S5_HALVES, gh * S5_GROUP, gh * S5_STATE)


def _block_diag_out(cc):
    gh = S5_GROUPS // S5_HALVES
    cc = cc.reshape(S5_HALVES, gh, S5_GROUP, S5_STATE)
    eye = jnp.eye(gh, dtype=cc.dtype)
    out = jnp.swapaxes(cc, 2, 3)[:, :, :, None, :] * eye[None, :, None, :, None]
    return out.reshape(S5_HALVES, gh * S5_STATE, gh * S5_GROUP)


def _chain_tiles(a, nbatch):
    a = a.reshape(S5_HALVES, S5_SLABS, LANES)
    a = jnp.swapaxes(a, 0, 1)
    return jnp.tile(a, (1, nbatch, 1))


def kernel(x, c, w_ada, b_ada, g_norm, w_in, b_f, a_re, a_im, log_dt, b_re, b_im, c_re, c_im,
           d_skip, w_glu, b_glu, w_up_a, w_up_b, w_out, g_final):
    bsz, s, d = x.shape
    depth = w_ada.shape[0]
    assert depth == 1 and d == D_MODEL

    mod = _adaln_mod(c, w_ada[0], b_ada[0])
    shift = mod[:, 0:d].reshape(bsz, 1, d)
    scale = mod[:, d:2 * d].reshape(bsz, 1, d)
    gate = mod[:, 2 * d:3 * d].reshape(bsz, 1, d)

    wt = jnp.swapaxes(w_in[0], 0, 1)
    o_z = 3 * FOX_WIDTH + HEADS
    wt_a = jnp.pad(wt[0:o_z], ((0, HEADS), (0, 0))).astype(_BF16)
    wt_b = wt[o_z:].astype(_BF16)

    q, kt, v, qaug, kaugt, za, u, zb, ga, gb, stats = _in_proj(
        x, g_norm[0], scale, shift, wt_a, wt_b, b_f[0])

    base = _skip_plan(stats)
    base_tab = jnp.swapaxes(base[:, :, 0:HEADS], 1, 2).reshape(-1)
    fend = stats[:, ST_FEND:ST_FEND + KB_PER_TILE, :, 0:HEADS]
    fend_tab = jnp.transpose(fend, (0, 3, 2, 1)).reshape(-1)
    ya = _fox_attention(base_tab, fend_tab, q, kt, kaugt, v, qaug, za)

    abr, abi, bbr, bbi = _s5_params(a_re[0], a_im[0], log_dt[0],
                                    jnp.swapaxes(b_re[0], 1, 2), jnp.swapaxes(b_im[0], 1, 2))
    bblk = jnp.concatenate([_block_diag_in(bbr), _block_diag_in(bbi)], axis=2).astype(_BF16)
    cblk = jnp.concatenate([_block_diag_out(c_re[0]), -_block_diag_out(c_im[0])], axis=1).astype(_BF16)
    yb = _s5_scan(u, zb, _chain_tiles(abr, bsz), _chain_tiles(abi, bsz), bblk, cblk,
                  d_skip[0], w_glu[0].astype(_BF16), b_glu[0])

    return _out_proj(x, ya, yb, ga, gb, gate, w_up_a[0].astype(_BF16), w_up_b[0].astype(_BF16),
                     w_out[0].astype(_BF16), g_final)
```

```python
import functools
import math

import jax
import jax.numpy as jnp
from jax import lax
from jax.experimental import pallas as pl
from jax.experimental.pallas import tpu as pltpu

D_MODEL = 1024
EPS = 1e-6
FOX_WIDTH = 512
HEAD_DIM = 64
HEADS = 8
HEAD_PAIRS = HEADS // 2
S5_WIDTH = 512
S5_GROUP = 16
S5_GROUPS = 32
S5_STATE = 64

LANES = 128
SUBLANES = 8
VMEM_LIMIT = 56 * 1024 * 1024

S5_HALVES = 2
S5_HALF_CH = S5_WIDTH // S5_HALVES
S5_HALF_STATE = S5_GROUPS * S5_STATE // S5_HALVES
S5_SLABS = S5_HALF_STATE // LANES

NEG_BIG = -1e30
LOG2E = math.log2(math.e)

ATTN_TILE = 512
ATTN_KBLOCK = 512
KB_PER_TILE = ATTN_TILE // ATTN_KBLOCK
ST_QN2, ST_FSTART, ST_KMAX, ST_FEND = 0, 1, 2, 3
ST_PLANES = ST_FEND + KB_PER_TILE
SKIP_LOG2 = 40.0
NORM_SAFETY = 1.02

_F32 = jnp.float32
_BF16 = jnp.bfloat16


def _dot(a, b):
    return jnp.dot(a, b, preferred_element_type=_F32)


def _dot_nt(a, b):
    return lax.dot_general(a, b, (((1,), (1,)), ((), ())), preferred_element_type=_F32)


def _split3(x):
    hi = x.astype(_BF16).astype(_F32)
    r1 = x - hi
    mid = r1.astype(_BF16).astype(_F32)
    lo = (r1 - mid).astype(_BF16).astype(_F32)
    return hi, mid, lo


def _sigmoid(x):
    return 1.0 / (1.0 + jnp.exp(-x))


def _silu(x):
    return x * _sigmoid(x)


def _adaln_kernel(ct_ref, w_ref, b_ref, o_ref):
    d, bn = w_ref.shape
    w3 = w_ref[...].reshape(d // SUBLANES, SUBLANES, bn)
    rows = []
    for b in range(ct_ref.shape[1]):
        cb = ct_ref[:, b:b + 1].reshape(d // SUBLANES, SUBLANES, 1)
        part = jnp.sum(w3 * cb, axis=0)
        rows.append(jnp.sum(part, axis=0, keepdims=True))
    o_ref[...] = jnp.concatenate(rows, axis=0) + b_ref[...]


def _adaln_mod(c, w, b):
    bsz, d = c.shape
    n = w.shape[1]
    bn = 1024
    return pl.pallas_call(
        _adaln_kernel,
        grid=(n // bn,),
        in_specs=[pl.BlockSpec((d, bsz), lambda j: (0, 0)),
                  pl.BlockSpec((d, bn), lambda j: (0, j)),
                  pl.BlockSpec((1, bn), lambda j: (0, j))],
        out_specs=pl.BlockSpec((bsz, bn), lambda j: (0, j)),
        out_shape=jax.ShapeDtypeStruct((bsz, n), _F32),
        name="adaln_mod",
    )(c.T, w, b.reshape(1, n))


def _block_diag(x, nblk, reps_per_tile):
    rows, c = x.shape
    r = rows // nblk
    tile = jnp.concatenate([x] * reps_per_tile, axis=1)
    wide = jnp.concatenate([tile] * (nblk // reps_per_tile), axis=1)
    rb = lax.broadcasted_iota(jnp.int32, wide.shape, 0) // r
    cb = lax.broadcasted_iota(jnp.int32, wide.shape, 1) // c
    return jnp.where(rb == cb, wide, 0.0)


def _s5_param_kernel(are_ref, aim_ref, ldt_ref, bre_ref, bim_ref, cre_ref, cim_ref,
                     abr_ref, abi_ref, bblk_ref, cblk_ref):
    lr = are_ref[...]
    li = aim_ref[...]
    dt = jnp.exp(ldt_ref[...])
    mag = jnp.exp(lr * dt)
    ang = li * dt
    abr = mag * jnp.cos(ang)
    abi = mag * jnp.sin(ang)
    abr_ref[...] = abr
    abi_ref[...] = abi
    nr = abr - 1.0
    ni = abi
    den = lr * lr + li * li
    cr = (nr * lr + ni * li) / den
    ci = (ni * lr - nr * li) / den
    cr3 = cr[:, None, :]
    ci3 = ci[:, None, :]
    br = bre_ref[...]
    bi = bim_ref[...]
    bbr = cr3 * br - ci3 * bi
    bbi = cr3 * bi + ci3 * br

    gh = S5_GROUPS // S5_HALVES
    for hf in range(S5_HALVES):
        gsl = slice(hf * gh, (hf + 1) * gh)
        b_in = [_block_diag(t[gsl].reshape(gh * S5_GROUP, S5_STATE), gh, LANES // S5_STATE) for t in (bbr, bbi)]
        bblk_ref[hf] = jnp.concatenate(b_in, axis=1).astype(_BF16)
        c_out = [_block_diag(t[gsl].reshape(gh * S5_STATE, S5_GROUP), gh, LANES // S5_GROUP)
                 for t in (cre_ref[...], -cim_ref[...])]
        cblk_ref[hf] = jnp.concatenate(c_out, axis=0).astype(_BF16)


def _s5_params(a_re, a_im, log_dt, b_re_t, b_im_t, c_re_t, c_im_t):
    g, p = a_re.shape
    cg = b_re_t.shape[1]
    gh = g // S5_HALVES
    return pl.pallas_call(
        _s5_param_kernel,
        out_shape=(jax.ShapeDtypeStruct((g, p), _F32), jax.ShapeDtypeStruct((g, p), _F32),
                   jax.ShapeDtypeStruct((S5_HALVES, gh * cg, 2 * gh * p), _BF16),
                   jax.ShapeDtypeStruct((S5_HALVES, 2 * gh * p, gh * cg), _BF16)),
        name="s5_params",
    )(a_re, a_im, log_dt.reshape(g, 1), b_re_t, b_im_t, c_re_t, c_im_t)


def _in_proj_kernel(x_ref, gn_ref, sc_ref, sh_ref, wa_ref, wt_ref, bf_ref,
                    q_ref, kt_ref, v_ref, qaug_ref, kaugt_ref, za_ref, u_ref, zb_ref, ga_ref, gb_ref,
                    stats_ref, carry_ref, kmax_ref, *, tm):
    i = pl.program_id(1)
    fw = FOX_WIDTH
    c_v, c_f = 2 * fw, 3 * fw
    c_za = 0
    c_u, c_zb = c_za + fw, c_za + fw + S5_WIDTH
    c_ga = c_zb + S5_WIDTH
    c_gb = c_ga + D_MODEL

    @pl.when(i == 0)
    def _():
        carry_ref[...] = jnp.zeros_like(carry_ref)
        kmax_ref[...] = jnp.zeros_like(kmax_ref)

    x = x_ref[0]
    ms = jnp.mean(x * x, axis=-1, keepdims=True)
    h = x * lax.rsqrt(ms + EPS) * gn_ref[...]
    h = (h * (1.0 + sc_ref[0]) + sh_ref[0]).astype(_BF16)

    w = FOX_WIDTH
    qb = (_dot_nt(h, wa_ref[0:fw, :]) * (HEAD_DIM ** -0.5 * LOG2E)).astype(_BF16)
    q_ref[0] = qb
    ktb = _dot_nt(wa_ref[fw:c_v, :], h).astype(_BF16)
    kt_ref[0] = ktb
    v_ref[0] = _dot_nt(h, wa_ref[c_v:c_f, :]).astype(_BF16)

    seg = (lax.broadcasted_iota(jnp.int32, (w, LANES), 0) // HEAD_DIM
           == lax.broadcasted_iota(jnp.int32, (w, LANES), 1)).astype(_BF16)
    seg_t = (lax.broadcasted_iota(jnp.int32, (LANES, w), 1) // HEAD_DIM
             == lax.broadcasted_iota(jnp.int32, (LANES, w), 0)).astype(_BF16)
    qf = qb.astype(_F32)
    kf = ktb.astype(_F32)
    qn2 = _dot((qf * qf).astype(_BF16), seg)
    kn2 = _dot(seg_t, (kf * kf).astype(_BF16)).T
    kmax = jnp.maximum(kmax_ref[...], jnp.max(kn2, axis=0, keepdims=True))
    kmax_ref[...] = kmax
    za_ref[0] = _silu(_dot_nt(h, wt_ref[c_za:c_u, :])).astype(_BF16)
    u_ref[0] = _dot_nt(h, wt_ref[c_u:c_zb, :]).astype(_BF16)
    zb_ref[0] = _silu(_dot_nt(h, wt_ref[c_zb:c_ga, :])).astype(_BF16)
    ga_ref[0] = _sigmoid(_dot_nt(h, wt_ref[c_ga:c_gb, :])).astype(_BF16)
    gb_ref[0] = _sigmoid(_dot_nt(h, wt_ref[c_gb:c_gb + D_MODEL, :])).astype(_BF16)

    flt = _dot_nt(wa_ref[c_f:c_f + 2 * HEADS, :], h)[0:HEADS] + bf_ref[...]
    logft = -(jnp.maximum(-flt, 0.0) + jnp.log1p(jnp.exp(-jnp.abs(flt)))) * LOG2E
    lmh_t = jnp.concatenate(_split3(logft), axis=0).astype(_BF16)

    ta = ATTN_TILE
    row = lax.broadcasted_iota(jnp.int32, (ta, ta), 0)
    col = lax.broadcasted_iota(jnp.int32, (ta, ta), 1)
    triu = (row <= col).astype(_BF16)
    ones = jnp.ones((HEADS, ta), _F32)
    zeros = jnp.zeros((HEADS, ta), _F32)
    nt = tm // ta
    carry = carry_ref[...]
    for t in range(nt):
        r0 = t * ta
        parts = _dot(lmh_t[:, r0:r0 + ta], triu)
        cum_t = (parts[0:HEADS] + parts[HEADS:2 * HEADS]) + parts[2 * HEADS:3 * HEADS] + carry[:, 0:1]
        carry = jnp.broadcast_to(cum_t[:, ta - 1:ta], (HEADS, LANES))

        kh, km, kl = _split3(cum_t)
        kaugt_ref[0, :, r0:r0 + ta] = jnp.concatenate(
            [ones, ones, ones, zeros, kh, km, kl] + [zeros] * 9, axis=0).astype(_BF16)
        cum = jnp.concatenate([cum_t] + [zeros] * (LANES // HEADS - 1), axis=0).T
        qh, qm, ql = _split3(cum)
        qaug_ref[0, r0:r0 + ta, :] = (qh + pltpu.roll(qm, HEADS, 1)
                                      + pltpu.roll(ql, 2 * HEADS, 1)).astype(_BF16)

        tile = i * nt + t
        stats_ref[0, ST_QN2, pl.ds(tile, 1), :] = jnp.max(qn2[r0:r0 + ta], axis=0, keepdims=True)
        stats_ref[0, ST_FSTART, pl.ds(tile, 1), :] = cum[0:1, :]
        for c in range(KB_PER_TILE):
            last = (c + 1) * ATTN_KBLOCK - 1
            stats_ref[0, ST_FEND + c, pl.ds(tile, 1), :] = cum[last:last + 1, :]
    carry_ref[...] = carry
    stats_ref[0, ST_KMAX] = jnp.broadcast_to(kmax, stats_ref.shape[2:])


def _in_proj(x, g_norm, scale, shift, wt_a, wt_b, b_f, tm=1024):
    bsz, s, d = x.shape
    ni = s // tm
    tok = lambda width, dt: jax.ShapeDtypeStruct((bsz, s, width), dt)
    tok_spec = lambda width: pl.BlockSpec((1, tm, width), lambda b, i: (b, i, 0))
    tr = lambda rows: jax.ShapeDtypeStruct((bsz, rows, s), _BF16)
    tr_spec = lambda rows: pl.BlockSpec((1, rows, tm), lambda b, i: (b, 0, i))
    const = lambda shape: pl.BlockSpec(shape, lambda b, i: (0,) * len(shape), pipeline_mode=pl.Buffered(1))
    per_b = pl.BlockSpec((1, 1, d), lambda b, i: (b, 0, 0))
    return pl.pallas_call(
        functools.partial(_in_proj_kernel, tm=tm),
        grid=(bsz, ni),
        in_specs=[tok_spec(d), const((1, d)), per_b, per_b,
                  const(wt_a.shape), const(wt_b.shape), const((HEADS, 1))],
        out_specs=[tok_spec(FOX_WIDTH), tr_spec(FOX_WIDTH), tok_spec(FOX_WIDTH),
                   tok_spec(LANES), tr_spec(LANES),
                   tok_spec(FOX_WIDTH), tok_spec(S5_WIDTH), tok_spec(S5_WIDTH),
                   tok_spec(d), tok_spec(d),
                   pl.BlockSpec((1, ST_PLANES, s // ATTN_TILE, LANES), lambda b, i: (b, 0, 0, 0))],
        out_shape=[tok(FOX_WIDTH, _BF16), tr(FOX_WIDTH), tok(FOX_WIDTH, _BF16),
                   tok(LANES, _BF16), tr(LANES),
                   tok(FOX_WIDTH, _BF16), tok(S5_WIDTH, _BF16), tok(S5_WIDTH, _BF16),
                   tok(d, _BF16), tok(d, _BF16),
                   jax.ShapeDtypeStruct((bsz, ST_PLANES, s // ATTN_TILE, LANES), _F32)],
        scratch_shapes=[pltpu.VMEM((HEADS, LANES), _F32), pltpu.VMEM((1, LANES), _F32)],
        compiler_params=pltpu.CompilerParams(
            dimension_semantics=("arbitrary", "arbitrary"), vmem_limit_bytes=VMEM_LIMIT),
        name="in_proj",
    )(x, g_norm.reshape(1, d), scale, shift, wt_a, wt_b, b_f.reshape(HEADS, 1))


def _plan_kernel(st_ref, o_ref):
    qn2 = st_ref[0, ST_QN2]
    fstart = st_ref[0, ST_FSTART]
    kmax2 = st_ref[0, ST_KMAX]
    o_ref[0] = fstart + (NORM_SAFETY * jnp.sqrt(qn2 * kmax2) + SKIP_LOG2)


def _skip_plan(stats):
    bsz, _, nq, _ = stats.shape
    return pl.pallas_call(
        _plan_kernel,
        grid=(bsz,),
        in_specs=[pl.BlockSpec((1, ST_PLANES, nq, LANES), lambda b: (b, 0, 0, 0))],
        out_specs=pl.BlockSpec((1, nq, LANES), lambda b: (b, 0, 0)),
        out_shape=jax.ShapeDtypeStruct((bsz, nq, LANES), _F32),
        name="skip_plan",
    )(stats)


def _attn_kernel(base_ref, fend_ref, q_ref, kt_ref, kaugt_ref, v_ref, qaug_ref, za_ref, o_ref,
                 lhs_ref, s_ref, p_ref, m_ref, alpha_ref, acc_ref, *, tq, tk, rb, nq):
    b = pl.program_id(0)
    hp = pl.program_id(1)
    i = pl.program_id(2)
    lane = lax.broadcasted_iota(jnp.int32, (1, LANES), 1)
    q = q_ref[0]
    qa = qaug_ref[0]

    head_lanes = [(lane >= hh * HEAD_DIM) & (lane < (hh + 1) * HEAD_DIM) for hh in range(2)]
    for hh in range(2):
        h = 2 * hp + hh
        f_lanes = (lane == h) | (lane == HEADS + h) | (lane == 2 * HEADS + h)
        neg_lanes = (lane == 4 * HEADS + h) | (lane == 5 * HEADS + h) | (lane == 6 * HEADS + h)
        qm = jnp.where(head_lanes[hh], q, jnp.zeros_like(q))
        qaug = jnp.where(f_lanes, qa, jnp.where(neg_lanes, -1.0, 0.0).astype(_BF16))
        lhs_ref[hh] = jnp.concatenate([qm, qaug], axis=1)
        m_ref[hh] = jnp.full((tq, LANES), NEG_BIG, _F32)
        acc_ref[hh] = jnp.zeros((tq, LANES), _F32)

    def step(j, masked):
        k0 = pl.multiple_of(j * tk, tk)
        w = jnp.concatenate([kt_ref[0, :, pl.ds(k0, tk)], kaugt_ref[0, :, pl.ds(k0, tk)]], axis=0)
        vb = v_ref[0, pl.ds(k0, tk), :]
        for hh in range(2):
            s = _dot(lhs_ref[hh], w)
            if masked:
                r = lax.broadcasted_iota(jnp.int32, (tq, tk), 0)
                c = lax.broadcasted_iota(jnp.int32, (tq, tk), 1)
                s = jnp.where(c - r <= i * tq - j * tk, s, NEG_BIG)
            s_ref[hh] = s
        for hh in range(2):
            for r0 in range(0, tq, rb):
                chunks = [s_ref[hh, r0:r0 + rb, c0:c0 + LANES] for c0 in range(0, tk, LANES)]
                m_old = m_ref[hh, r0:r0 + rb, :]
                m_new = jnp.maximum(m_old, jnp.max(functools.reduce(jnp.maximum, chunks), axis=1, keepdims=True))
                alpha_ref[hh, r0:r0 + rb, :] = jnp.exp2(m_old - m_new)
                m_ref[hh, r0:r0 + rb, :] = m_new
                for c0, ch in zip(range(0, tk, LANES), chunks):
                    p_ref[hh, r0:r0 + rb, c0:c0 + LANES] = jnp.exp2(ch - m_new).astype(_BF16)
        for hh in range(2):
            sum_lane = (1 - hh) * HEAD_DIM
            vaug = jnp.where(head_lanes[hh], vb, jnp.where(lane == sum_lane, 1.0, 0.0).astype(_BF16))
            acc_ref[hh] = alpha_ref[hh] * acc_ref[hh] + _dot(p_ref[hh], vaug)

    kpt = tq // tk
    j_diag = i * kpt

    for c in range(kpt):
        step(j_diag + c, True)

    def first_needed(hh):
        thr = base_ref[(b * HEADS + 2 * hp + hh) * nq + i] - jnp.min(m_ref[hh])
        row = (b * HEADS + 2 * hp + hh) * (nq * kpt)
        count = jnp.int32(0)
        for jj in range(nq * kpt):
            count += ((fend_ref[row + jj] >= thr) & (jj < j_diag)).astype(jnp.int32)
        return count

    def step_one(hd, js):
        lanes_h = (lane >= hd * HEAD_DIM) & (lane < (hd + 1) * HEAD_DIM)
        sum_lane = (1 - hd) * HEAD_DIM
        lhs = lhs_ref[hd]
        vbs = []
        for n, j in enumerate(js):
            k0 = pl.multiple_of(j * tk, tk)
            w = jnp.concatenate([kt_ref[0, :, pl.ds(k0, tk)], kaugt_ref[0, :, pl.ds(k0, tk)]], axis=0)
            vbs.append(v_ref[0, pl.ds(k0, tk), :])
            s_ref[n] = _dot(lhs, w)
        for n in range(len(js)):
            for r0 in range(0, tq, rb):
                chunks = [s_ref[n, r0:r0 + rb, c0:c0 + LANES] for c0 in range(0, tk, LANES)]
                m_old = m_ref[hd, r0:r0 + rb, :]
                m_new = jnp.maximum(m_old, jnp.max(functools.reduce(jnp.maximum, chunks), axis=1, keepdims=True))
                alpha_ref[hd, r0:r0 + rb, :] = jnp.exp2(m_old - m_new)
                m_ref[hd, r0:r0 + rb, :] = m_new
                for c0, ch in zip(range(0, tk, LANES), chunks):
                    p_ref[n, r0:r0 + rb, c0:c0 + LANES] = jnp.exp2(ch - m_new).astype(_BF16)
            vaug = jnp.where(lanes_h, vbs[n], jnp.where(lane == sum_lane, 1.0, 0.0).astype(_BF16))
            acc_ref[hd] = alpha_ref[hd] * acc_ref[hd] + _dot(p_ref[n], vaug)

    jf0 = first_needed(0)
    jf1 = first_needed(1)
    j_lo = jnp.minimum(jf0, jf1)
    j_first = jnp.maximum(jf0, jf1)
    slow = (jf1 < jf0).astype(jnp.int32)
    n_one = j_first - j_lo

    @pl.when((n_one & 1) == 1)
    def _():
        step_one(slow, [j_lo])

    j_lo2 = j_lo + (n_one & 1)

    def pair_one(t, carry):
        step_one(slow, [j_lo2 + 2 * t, j_lo2 + 2 * t + 1])
        return carry

    lax.fori_loop(0, n_one >> 1, pair_one, 0)

    n_old = j_diag - j_first

    @pl.when((n_old & 1) == 1)
    def _():
        step(j_first, False)

    j_2 = j_first + (n_old & 1)

    @pl.when((n_old & 2) == 2)
    def _():
        step(j_2, False)
        step(j_2 + 1, False)

    j_4 = j_2 + (n_old & 2)

    def quad_step(t, carry):
        for c in range(4):
            step(j_4 + 4 * t + c, False)
        return carry

    lax.fori_loop(0, n_old >> 2, quad_step, 0)

    a0 = acc_ref[0]
    a1 = acc_ref[1]
    l0 = a0[:, HEAD_DIM:HEAD_DIM + 1]
    l1 = a1[:, 0:1]
    y = jnp.where(lane < HEAD_DIM, a0 / l0, a1 / l1)
    o_ref[0] = (y * za_ref[0].astype(_F32)).astype(_BF16)


def _fox_attention(base, fend, q, kt, kaugt, v, qaug, za, rb=32):
    bsz, s, _ = q.shape
    tq, tk = ATTN_TILE, ATTN_KBLOCK
    nq = s // tq
    qspec = pl.BlockSpec((1, tq, LANES), lambda b, hp, i, *_: (b, i, hp))
    grid_spec = pltpu.PrefetchScalarGridSpec(
        num_scalar_prefetch=2,
        grid=(bsz, HEAD_PAIRS, nq),
        in_specs=[qspec,
                  pl.BlockSpec((1, LANES, s), lambda b, hp, i, *_: (b, hp, 0)),
                  pl.BlockSpec((1, LANES, s), lambda b, hp, i, *_: (b, 0, 0)),
                  pl.BlockSpec((1, s, LANES), lambda b, hp, i, *_: (b, 0, hp)),
                  pl.BlockSpec((1, tq, LANES), lambda b, hp, i, *_: (b, i, 0)),
                  qspec],
        out_specs=qspec,
        scratch_shapes=[pltpu.VMEM((2, tq, 2 * LANES), _BF16),
                        pltpu.VMEM((2, tq, tk), _F32), pltpu.VMEM((2, tq, tk), _BF16),
                        pltpu.VMEM((2, tq, LANES), _F32), pltpu.VMEM((2, tq, LANES), _F32),
                        pltpu.VMEM((2, tq, LANES), _F32)])
    return pl.pallas_call(
        functools.partial(_attn_kernel, tq=tq, tk=tk, rb=rb, nq=nq),
        grid_spec=grid_spec,
        out_shape=jax.ShapeDtypeStruct((bsz, s, FOX_WIDTH), _BF16),
        compiler_params=pltpu.CompilerParams(
            dimension_semantics=("arbitrary", "arbitrary", "arbitrary"), vmem_limit_bytes=VMEM_LIMIT),
        name="fox_attn",
    )(base, fend, q, kt, kaugt, v, qaug, za)


def _gelu_tanh(x):
    return 0.5 * x * (1.0 + jnp.tanh(math.sqrt(2.0 / math.pi) * (x + 0.044715 * (x * x * x))))


def _s5_kernel(u_ref, zb_ref, ar_ref, ai_ref, bblk_ref, cblk_ref, dskip_ref, wglu_ref, bglu_ref,
               o_ref, xr_ref, xi_ref, sr_ref, si_ref, *, tt, nbatch, slabs_per_pass):
    t = pl.program_id(0)
    nchain = nbatch * S5_HALVES

    @pl.when(t == 0)
    def _():
        sr_ref[...] = jnp.zeros_like(sr_ref)
        si_ref[...] = jnp.zeros_like(si_ref)

    for hf in range(S5_HALVES):
        ub = jnp.concatenate([u_ref[b, :, hf * S5_HALF_CH:(hf + 1) * S5_HALF_CH] for b in range(nbatch)],
                             axis=0)
        bu = _dot(ub, bblk_ref[hf])
        for b in range(nbatch):
            kc = b * S5_HALVES + hf
            rows = slice(b * tt, (b + 1) * tt)
            for j in range(S5_SLABS):
                xr_ref[j, pl.ds(kc, tt, stride=nchain), :] = bu[rows, j * LANES:(j + 1) * LANES]
                xi_ref[j, pl.ds(kc, tt, stride=nchain), :] = bu[rows, (S5_SLABS + j) * LANES:
                                                                (S5_SLABS + j + 1) * LANES]

    for j0 in range(0, S5_SLABS, slabs_per_pass):
        js = list(range(j0, j0 + slabs_per_pass))
        ars = [ar_ref[j] for j in js]
        ais = [ai_ref[j] for j in js]

        def body(s, carry, js=js, ars=ars, ais=ais):
            r0 = pl.multiple_of(s * nchain, nchain)
            new = []
            for n, j in enumerate(js):
                sr, si = carry[2 * n], carry[2 * n + 1]
                br = xr_ref[j, pl.ds(r0, nchain), :]
                bi = xi_ref[j, pl.ds(r0, nchain), :]
                nr = (ars[n] * sr - ais[n] * si) + br
                ni = (ars[n] * si + ais[n] * sr) + bi
                xr_ref[j, pl.ds(r0, nchain), :] = nr
                xi_ref[j, pl.ds(r0, nchain), :] = ni
                new += [nr, ni]
            return tuple(new)

        init = []
        for j in js:
            init += [sr_ref[j], si_ref[j]]
        fin = lax.fori_loop(0, tt, body, tuple(init), unroll=4)
        for n, j in enumerate(js):
            sr_ref[j] = fin[2 * n]
            si_ref[j] = fin[2 * n + 1]

    halves = []
    for hf in range(S5_HALVES):
        xall = []
        for b in range(nbatch):
            kc = b * S5_HALVES + hf
            xs = [xr_ref[j, pl.ds(kc, tt, stride=nchain), :] for j in range(S5_SLABS)]
            xs += [xi_ref[j, pl.ds(kc, tt, stride=nchain), :] for j in range(S5_SLABS)]
            xall.append(jnp.concatenate(xs, axis=1).astype(_BF16))
        halves.append(_dot(jnp.concatenate(xall, axis=0), cblk_ref[hf]))
    w = u_ref.shape[-1]
    y = jnp.concatenate(halves, axis=1)
    y = y + dskip_ref[...] * u_ref[...].reshape(nbatch * tt, w).astype(_F32)
    y = _gelu_tanh(y)
    y = y * _sigmoid(_dot(y.astype(_BF16), wglu_ref[...]) + bglu_ref[...])
    y = y * zb_ref[...].reshape(nbatch * tt, w).astype(_F32)
    o_ref[...] = y.astype(_BF16).reshape(nbatch, tt, w)


def _s5_scan(u, zb, ar_t, ai_t, bblk, cblk, d_skip, w_glu, b_glu, tt=256, slabs_per_pass=8):
    bsz, s, w = u.shape
    nchain = bsz * S5_HALVES
    assert nchain == SUBLANES, "the scan packs batch x state-half chains into the 8 sublanes"
    tok_spec = pl.BlockSpec((bsz, tt, w), lambda t: (0, t, 0))
    const = lambda shape: pl.BlockSpec(shape, lambda t: (0,) * len(shape), pipeline_mode=pl.Buffered(1))
    return pl.pallas_call(
        functools.partial(_s5_kernel, tt=tt, nbatch=bsz, slabs_per_pass=slabs_per_pass),
        grid=(s // tt,),
        in_specs=[tok_spec, tok_spec, const(ar_t.shape), const(ai_t.shape),
                  const(bblk.shape), const(cblk.shape), const((1, w)), const(w_glu.shape), const((1, w))],
        out_specs=tok_spec,
        out_shape=jax.ShapeDtypeStruct((bsz, s, w), _BF16),
        scratch_shapes=[pltpu.VMEM((S5_SLABS, tt * nchain, LANES), _F32),
                        pltpu.VMEM((S5_SLABS, tt * nchain, LANES), _F32),
                        pltpu.VMEM((S5_SLABS, nchain, LANES), _F32),
                        pltpu.VMEM((S5_SLABS, nchain, LANES), _F32)],
        compiler_params=pltpu.CompilerParams(
            dimension_semantics=("arbitrary",), vmem_limit_bytes=VMEM_LIMIT),
        name="s5_scan",
    )(u, zb, ar_t, ai_t, bblk, cblk, d_skip.reshape(1, w), w_glu, b_glu.reshape(1, w))


def _out_kernel(x_ref, ya_ref, yb_ref, ga_ref, gb_ref, gate_ref, wua_ref, wub_ref, wo_ref, gf_ref, o_ref):
    ma = ga_ref[0].astype(_F32) * _dot(ya_ref[0], wua_ref[...])
    mb = gb_ref[0].astype(_F32) * _dot(yb_ref[0], wub_ref[...])
    merged = (ma + mb).astype(_BF16)
    xn = x_ref[0] + gate_ref[0] * _dot(merged, wo_ref[...])
    ms = jnp.mean(xn * xn, axis=-1, keepdims=True)
    o_ref[0] = xn * lax.rsqrt(ms + EPS) * gf_ref[...]


def _out_proj(x, ya, yb, ga, gb, gate, w_up_a, w_up_b, w_out, g_final, tm=1024):
    bsz, s, d = x.shape
    tok_spec = lambda width: pl.BlockSpec((1, tm, width), lambda b, i: (b, i, 0))
    const = lambda shape: pl.BlockSpec(shape, lambda b, i: (0,) * len(shape), pipeline_mode=pl.Buffered(1))
    return pl.pallas_call(
        _out_kernel,
        grid=(bsz, s // tm),
        in_specs=[tok_spec(d), tok_spec(FOX_WIDTH), tok_spec(S5_WIDTH), tok_spec(d), tok_spec(d),
                  pl.BlockSpec((1, 1, d), lambda b, i: (b, 0, 0)),
                  const(w_up_a.shape), const(w_up_b.shape), const(w_out.shape), const((1, d))],
        out_specs=tok_spec(d),
        out_shape=jax.ShapeDtypeStruct((bsz, s, d), x.dtype),
        compiler_params=pltpu.CompilerParams(
            dimension_semantics=("arbitrary", "arbitrary"), vmem_limit_bytes=VMEM_LIMIT),
        name="out_proj",
    )(x, ya, yb, ga, gb, gate, w_up_a, w_up_b, w_out, g_final.reshape(1, d))


def _chain_tiles(a, nbatch):
    a = a.reshape(S5_HALVES, S5_SLABS, LANES)
    a = jnp.swapaxes(a, 0, 1)
    return jnp.tile(a, (1, nbatch, 1))


def kernel(x, c, w_ada, b_ada, g_norm, w_in, b_f, a_re, a_im, log_dt, b_re, b_im, c_re, c_im,
           d_skip, w_glu, b_glu, w_up_a, w_up_b, w_out, g_final):
    bsz, s, d = x.shape
    depth = w_ada.shape[0]
    assert depth == 1 and d == D_MODEL

    mod = _adaln_mod(c, w_ada[0], b_ada[0])
    shift = mod[:, 0:d].reshape(bsz, 1, d)
    scale = mod[:, d:2 * d].reshape(bsz, 1, d)
    gate = mod[:, 2 * d:3 * d].reshape(bsz, 1, d)

    wt = jnp.swapaxes(w_in[0], 0, 1)
    o_z = 3 * FOX_WIDTH + HEADS
    wt_a = jnp.pad(wt[0:o_z], ((0, HEADS), (0, 0))).astype(_BF16)
    wt_b = wt[o_z:].astype(_BF16)

    q, kt, v, qaug, kaugt, za, u, zb, ga, gb, stats = _in_proj(
        x, g_norm[0], scale, shift, wt_a, wt_b, b_f[0])

    base = _skip_plan(stats)
    base_tab = jnp.swapaxes(base[:, :, 0:HEADS], 1, 2).reshape(-1)
    fend = stats[:, ST_FEND:ST_FEND + KB_PER_TILE, :, 0:HEADS]
    fend_tab = jnp.transpose(fend, (0, 3, 2, 1)).reshape(-1)
    ya = _fox_attention(base_tab, fend_tab, q, kt, kaugt, v, qaug, za)

    abr, abi, bblk, cblk = _s5_params(a_re[0], a_im[0], log_dt[0],
                                      jnp.swapaxes(b_re[0], 1, 2), jnp.swapaxes(b_im[0], 1, 2),
                                      jnp.swapaxes(c_re[0], 1, 2), jnp.swapaxes(c_im[0], 1, 2))
    yb = _s5_scan(u, zb, _chain_tiles(abr, bsz), _chain_tiles(abi, bsz), bblk, cblk,
                  d_skip[0], w_glu[0].astype(_BF16), b_glu[0])

    return _out_proj(x, ya, yb, ga, gb, gate, w_up_a[0].astype(_BF16), w_up_b[0].astype(_BF16),
                     w_out[0].astype(_BF16), g_final)
```

```python
import functools
import math

import jax
import jax.numpy as jnp
from jax import lax
from jax.experimental import pallas as pl
from jax.experimental.pallas import tpu as pltpu

D_MODEL = 1024
EPS = 1e-6
FOX_WIDTH = 512
HEAD_DIM = 64
HEADS = 8
HEAD_PAIRS = HEADS // 2
S5_WIDTH = 512
S5_GROUP = 16
S5_GROUPS = 32
S5_STATE = 64

LANES = 128
SUBLANES = 8
VMEM_LIMIT = 56 * 1024 * 1024

S5_HALVES = 2
S5_HALF_CH = S5_WIDTH // S5_HALVES
S5_HALF_STATE = S5_GROUPS * S5_STATE // S5_HALVES
S5_SLABS = S5_HALF_STATE // LANES

NEG_BIG = -1e30
LOG2E = math.log2(math.e)

ATTN_TILE = 512
ATTN_KBLOCK = 512
KB_PER_TILE = ATTN_TILE // ATTN_KBLOCK
ST_QN2, ST_FSTART, ST_KMAX, ST_FEND = 0, 1, 2, 3
ST_PLANES = ST_FEND + KB_PER_TILE
SKIP_LOG2 = 40.0
NORM_SAFETY = 1.02

_F32 = jnp.float32
_BF16 = jnp.bfloat16


def _dot(a, b):
    return jnp.dot(a, b, preferred_element_type=_F32)


def _dot_nt(a, b):
    return lax.dot_general(a, b, (((1,), (1,)), ((), ())), preferred_element_type=_F32)


def _split3(x):
    hi = x.astype(_BF16).astype(_F32)
    r1 = x - hi
    mid = r1.astype(_BF16).astype(_F32)
    lo = (r1 - mid).astype(_BF16).astype(_F32)
    return hi, mid, lo


def _sigmoid(x):
    return 1.0 / (1.0 + jnp.exp(-x))


def _silu(x):
    return x * _sigmoid(x)


def _adaln_kernel(ct_ref, w_ref, b_ref, o_ref):
    d, bn = w_ref.shape
    w3 = w_ref[...].reshape(d // SUBLANES, SUBLANES, bn)
    rows = []
    for b in range(ct_ref.shape[1]):
        cb = ct_ref[:, b:b + 1].reshape(d // SUBLANES, SUBLANES, 1)
        part = jnp.sum(w3 * cb, axis=0)
        rows.append(jnp.sum(part, axis=0, keepdims=True))
    o_ref[...] = jnp.concatenate(rows, axis=0) + b_ref[...]


def _adaln_mod(c, w, b):
    bsz, d = c.shape
    n = w.shape[1]
    bn = 512
    return pl.pallas_call(
        _adaln_kernel,
        grid=(n // bn,),
        in_specs=[pl.BlockSpec((d, bsz), lambda j: (0, 0)),
                  pl.BlockSpec((d, bn), lambda j: (0, j)),
                  pl.BlockSpec((1, bn), lambda j: (0, j))],
        out_specs=pl.BlockSpec((bsz, bn), lambda j: (0, j)),
        out_shape=jax.ShapeDtypeStruct((bsz, n), _F32),
        name="adaln_mod",
    )(c.T, w, b.reshape(1, n))


def _block_diag(x, nblk, reps_per_tile):
    rows, c = x.shape
    r = rows // nblk
    tile = jnp.concatenate([x] * reps_per_tile, axis=1)
    wide = jnp.concatenate([tile] * (nblk // reps_per_tile), axis=1)
    rb = lax.broadcasted_iota(jnp.int32, wide.shape, 0) // r
    cb = lax.broadcasted_iota(jnp.int32, wide.shape, 1) // c
    return jnp.where(rb == cb, wide, 0.0)


def _s5_param_kernel(are_ref, aim_ref, ldt_ref, bre_ref, bim_ref, cre_ref, cim_ref,
                     abr_ref, abi_ref, bblk_ref, cblk_ref):
    lr = are_ref[...]
    li = aim_ref[...]
    dt = jnp.exp(ldt_ref[...])
    mag = jnp.exp(lr * dt)
    ang = li * dt
    abr = mag * jnp.cos(ang)
    abi = mag * jnp.sin(ang)
    abr_ref[...] = abr
    abi_ref[...] = abi
    nr = abr - 1.0
    ni = abi
    den = lr * lr + li * li
    cr = (nr * lr + ni * li) / den
    ci = (ni * lr - nr * li) / den
    cr3 = cr[:, None, :]
    ci3 = ci[:, None, :]
    br = bre_ref[...]
    bi = bim_ref[...]
    bbr = cr3 * br - ci3 * bi
    bbi = cr3 * bi + ci3 * br

    gh = S5_GROUPS // S5_HALVES
    for hf in range(S5_HALVES):
        gsl = slice(hf * gh, (hf + 1) * gh)
        b_in = [_block_diag(t[gsl].reshape(gh * S5_GROUP, S5_STATE), gh, LANES // S5_STATE) for t in (bbr, bbi)]
        bblk_ref[hf] = jnp.concatenate(b_in, axis=1).astype(_BF16)
        c_out = [_block_diag(t[gsl].reshape(gh * S5_STATE, S5_GROUP), gh, LANES // S5_GROUP)
                 for t in (cre_ref[...], -cim_ref[...])]
        cblk_ref[hf] = jnp.concatenate(c_out, axis=0).astype(_BF16)


def _s5_params(a_re, a_im, log_dt, b_re_t, b_im_t, c_re_t, c_im_t):
    g, p = a_re.shape
    cg = b_re_t.shape[1]
    gh = g // S5_HALVES
    return pl.pallas_call(
        _s5_param_kernel,
        out_shape=(jax.ShapeDtypeStruct((g, p), _F32), jax.ShapeDtypeStruct((g, p), _F32),
                   jax.ShapeDtypeStruct((S5_HALVES, gh * cg, 2 * gh * p), _BF16),
                   jax.ShapeDtypeStruct((S5_HALVES, 2 * gh * p, gh * cg), _BF16)),
        name="s5_params",
    )(a_re, a_im, log_dt.reshape(g, 1), b_re_t, b_im_t, c_re_t, c_im_t)


def _in_proj_kernel(x_ref, gn_ref, sc_ref, sh_ref, wa_ref, wt_ref, bf_ref,
                    q_ref, kt_ref, v_ref, qaug_ref, kaugt_ref, za_ref, u_ref, zb_ref, ga_ref, gb_ref,
                    stats_ref, carry_ref, kmax_ref, *, tm):
    i = pl.program_id(1)
    fw = FOX_WIDTH
    c_v, c_f = 2 * fw, 3 * fw
    c_za = 0
    c_u, c_zb = c_za + fw, c_za + fw + S5_WIDTH
    c_ga = c_zb + S5_WIDTH
    c_gb = c_ga + D_MODEL

    @pl.when(i == 0)
    def _():
        carry_ref[...] = jnp.zeros_like(carry_ref)
        kmax_ref[...] = jnp.zeros_like(kmax_ref)

    x = x_ref[0]
    ms = jnp.mean(x * x, axis=-1, keepdims=True)
    h = x * lax.rsqrt(ms + EPS) * gn_ref[...]
    h = (h * (1.0 + sc_ref[0]) + sh_ref[0]).astype(_BF16)

    w = FOX_WIDTH
    qb = (_dot_nt(h, wa_ref[0:fw, :]) * (HEAD_DIM ** -0.5 * LOG2E)).astype(_BF16)
    q_ref[0] = qb
    ktb = _dot_nt(wa_ref[fw:c_v, :], h).astype(_BF16)
    kt_ref[0] = ktb
    v_ref[0] = _dot_nt(h, wa_ref[c_v:c_f, :]).astype(_BF16)

    seg = (lax.broadcasted_iota(jnp.int32, (w, LANES), 0) // HEAD_DIM
           == lax.broadcasted_iota(jnp.int32, (w, LANES), 1)).astype(_BF16)
    seg_t = (lax.broadcasted_iota(jnp.int32, (LANES, w), 1) // HEAD_DIM
             == lax.broadcasted_iota(jnp.int32, (LANES, w), 0)).astype(_BF16)
    qf = qb.astype(_F32)
    kf = ktb.astype(_F32)
    qn2 = _dot((qf * qf).astype(_BF16), seg)
    kn2 = _dot(seg_t, (kf * kf).astype(_BF16)).T
    kmax = jnp.maximum(kmax_ref[...], jnp.max(kn2, axis=0, keepdims=True))
    kmax_ref[...] = kmax
    za_ref[0] = _silu(_dot_nt(h, wt_ref[c_za:c_u, :])).astype(_BF16)
    u_ref[0] = _dot_nt(h, wt_ref[c_u:c_zb, :]).astype(_BF16)
    zb_ref[0] = _silu(_dot_nt(h, wt_ref[c_zb:c_ga, :])).astype(_BF16)
    ga_ref[0] = _sigmoid(_dot_nt(h, wt_ref[c_ga:c_gb, :])).astype(_BF16)
    gb_ref[0] = _sigmoid(_dot_nt(h, wt_ref[c_gb:c_gb + D_MODEL, :])).astype(_BF16)

    flt = _dot_nt(wa_ref[c_f:c_f + 2 * HEADS, :], h)[0:HEADS] + bf_ref[...]
    logft = -(jnp.maximum(-flt, 0.0) + jnp.log1p(jnp.exp(-jnp.abs(flt)))) * LOG2E
    lmh_t = jnp.concatenate(_split3(logft), axis=0).astype(_BF16)

    ta = ATTN_TILE
    row = lax.broadcasted_iota(jnp.int32, (ta, ta), 0)
    col = lax.broadcasted_iota(jnp.int32, (ta, ta), 1)
    triu = (row <= col).astype(_BF16)
    ones = jnp.ones((HEADS, ta), _F32)
    zeros = jnp.zeros((HEADS, ta), _F32)
    nt = tm // ta
    carry = carry_ref[...]
    for t in range(nt):
        r0 = t * ta
        parts = _dot(lmh_t[:, r0:r0 + ta], triu)
        cum_t = (parts[0:HEADS] + parts[HEADS:2 * HEADS]) + parts[2 * HEADS:3 * HEADS] + carry[:, 0:1]
        carry = jnp.broadcast_to(cum_t[:, ta - 1:ta], (HEADS, LANES))

        kh, km, kl = _split3(cum_t)
        kaugt_ref[0, :, r0:r0 + ta] = jnp.concatenate(
            [ones, ones, ones, zeros, kh, km, kl] + [zeros] * 9, axis=0).astype(_BF16)
        cum = jnp.concatenate([cum_t] + [zeros] * (LANES // HEADS - 1), axis=0).T
        qh, qm, ql = _split3(cum)
        qaug_ref[0, r0:r0 + ta, :] = (qh + pltpu.roll(qm, HEADS, 1)
                                      + pltpu.roll(ql, 2 * HEADS, 1)).astype(_BF16)

        tile = i * nt + t
        stats_ref[0, ST_QN2, pl.ds(tile, 1), :] = jnp.max(qn2[r0:r0 + ta], axis=0, keepdims=True)
        stats_ref[0, ST_FSTART, pl.ds(tile, 1), :] = cum[0:1, :]
        for c in range(KB_PER_TILE):
            last = (c + 1) * ATTN_KBLOCK - 1
            stats_ref[0, ST_FEND + c, pl.ds(tile, 1), :] = cum[last:last + 1, :]
    carry_ref[...] = carry
    stats_ref[0, ST_KMAX] = jnp.broadcast_to(kmax, stats_ref.shape[2:])


def _in_proj(x, g_norm, scale, shift, wt_a, wt_b, b_f, tm=1024):
    bsz, s, d = x.shape
    ni = s // tm
    tok = lambda width, dt: jax.ShapeDtypeStruct((bsz, s, width), dt)
    tok_spec = lambda width: pl.BlockSpec((1, tm, width), lambda b, i: (b, i, 0))
    tr = lambda rows: jax.ShapeDtypeStruct((bsz, rows, s), _BF16)
    tr_spec = lambda rows: pl.BlockSpec((1, rows, tm), lambda b, i: (b, 0, i))
    const = lambda shape: pl.BlockSpec(shape, lambda b, i: (0,) * len(shape), pipeline_mode=pl.Buffered(1))
    per_b = pl.BlockSpec((1, 1, d), lambda b, i: (b, 0, 0))
    return pl.pallas_call(
        functools.partial(_in_proj_kernel, tm=tm),
        grid=(bsz, ni),
        in_specs=[tok_spec(d), const((1, d)), per_b, per_b,
                  const(wt_a.shape), const(wt_b.shape), const((HEADS, 1))],
        out_specs=[tok_spec(FOX_WIDTH), tr_spec(FOX_WIDTH), tok_spec(FOX_WIDTH),
                   tok_spec(LANES), tr_spec(LANES),
                   tok_spec(FOX_WIDTH), tok_spec(S5_WIDTH), tok_spec(S5_WIDTH),
                   tok_spec(d), tok_spec(d),
                   pl.BlockSpec((1, ST_PLANES, s // ATTN_TILE, LANES), lambda b, i: (b, 0, 0, 0))],
        out_shape=[tok(FOX_WIDTH, _BF16), tr(FOX_WIDTH), tok(FOX_WIDTH, _BF16),
                   tok(LANES, _BF16), tr(LANES),
                   tok(FOX_WIDTH, _BF16), tok(S5_WIDTH, _BF16), tok(S5_WIDTH, _BF16),
                   tok(d, _BF16), tok(d, _BF16),
                   jax.ShapeDtypeStruct((bsz, ST_PLANES, s // ATTN_TILE, LANES), _F32)],
        scratch_shapes=[pltpu.VMEM((HEADS, LANES), _F32), pltpu.VMEM((1, LANES), _F32)],
        compiler_params=pltpu.CompilerParams(
            dimension_semantics=("arbitrary", "arbitrary"), vmem_limit_bytes=VMEM_LIMIT),
        name="in_proj",
    )(x, g_norm.reshape(1, d), scale, shift, wt_a, wt_b, b_f.reshape(HEADS, 1))


def _plan_kernel(st_ref, o_ref):
    qn2 = st_ref[0, ST_QN2]
    fstart = st_ref[0, ST_FSTART]
    kmax2 = st_ref[0, ST_KMAX]
    o_ref[0] = fstart + (NORM_SAFETY * jnp.sqrt(qn2 * kmax2) + SKIP_LOG2)


def _skip_plan(stats):
    bsz, _, nq, _ = stats.shape
    return pl.pallas_call(
        _plan_kernel,
        grid=(bsz,),
        in_specs=[pl.BlockSpec((1, ST_PLANES, nq, LANES), lambda b: (b, 0, 0, 0))],
        out_specs=pl.BlockSpec((1, nq, LANES), lambda b: (b, 0, 0)),
        out_shape=jax.ShapeDtypeStruct((bsz, nq, LANES), _F32),
        name="skip_plan",
    )(stats)


def _attn_kernel(base_ref, fend_ref, q_ref, kt_ref, kaugt_ref, v_ref, qaug_ref, za_ref, o_ref,
                 lhs_ref, s_ref, p_ref, m_ref, alpha_ref, acc_ref, *, tq, tk, rb, nq):
    b = pl.program_id(0)
    hp = pl.program_id(1)
    i = pl.program_id(2)
    lane = lax.broadcasted_iota(jnp.int32, (1, LANES), 1)
    q = q_ref[0]
    qa = qaug_ref[0]

    head_lanes = [(lane >= hh * HEAD_DIM) & (lane < (hh + 1) * HEAD_DIM) for hh in range(2)]
    for hh in range(2):
        h = 2 * hp + hh
        f_lanes = (lane == h) | (lane == HEADS + h) | (lane == 2 * HEADS + h)
        neg_lanes = (lane == 4 * HEADS + h) | (lane == 5 * HEADS + h) | (lane == 6 * HEADS + h)
        qm = jnp.where(head_lanes[hh], q, jnp.zeros_like(q))
        qaug = jnp.where(f_lanes, qa, jnp.where(neg_lanes, -1.0, 0.0).astype(_BF16))
        lhs_ref[hh] = jnp.concatenate([qm, qaug], axis=1)
        m_ref[hh] = jnp.full((tq, LANES), NEG_BIG, _F32)
        acc_ref[hh] = jnp.zeros((tq, LANES), _F32)

    def step(j, masked):
        k0 = pl.multiple_of(j * tk, tk)
        w = jnp.concatenate([kt_ref[0, :, pl.ds(k0, tk)], kaugt_ref[0, :, pl.ds(k0, tk)]], axis=0)
        vb = v_ref[0, pl.ds(k0, tk), :]
        for hh in range(2):
            s = _dot(lhs_ref[hh], w)
            if masked:
                r = lax.broadcasted_iota(jnp.int32, (tq, tk), 0)
                c = lax.broadcasted_iota(jnp.int32, (tq, tk), 1)
                s = jnp.where(c - r <= i * tq - j * tk, s, NEG_BIG)
            s_ref[hh] = s
        for hh in range(2):
            for r0 in range(0, tq, rb):
                chunks = [s_ref[hh, r0:r0 + rb, c0:c0 + LANES] for c0 in range(0, tk, LANES)]
                m_old = m_ref[hh, r0:r0 + rb, :]
                m_new = jnp.maximum(m_old, jnp.max(functools.reduce(jnp.maximum, chunks), axis=1, keepdims=True))
                alpha_ref[hh, r0:r0 + rb, :] = jnp.exp2(m_old - m_new)
                m_ref[hh, r0:r0 + rb, :] = m_new
                for c0, ch in zip(range(0, tk, LANES), chunks):
                    p_ref[hh, r0:r0 + rb, c0:c0 + LANES] = jnp.exp2(ch - m_new).astype(_BF16)
        for hh in range(2):
            sum_lane = (1 - hh) * HEAD_DIM
            vaug = jnp.where(head_lanes[hh], vb, jnp.where(lane == sum_lane, 1.0, 0.0).astype(_BF16))
            acc_ref[hh] = alpha_ref[hh] * acc_ref[hh] + _dot(p_ref[hh], vaug)

    kpt = tq // tk
    j_diag = i * kpt

    for c in range(kpt):
        step(j_diag + c, True)

    def first_needed(hh):
        thr = base_ref[(b * HEADS + 2 * hp + hh) * nq + i] - jnp.min(m_ref[hh])
        row = (b * HEADS + 2 * hp + hh) * (nq * kpt)
        count = jnp.int32(0)
        for jj in range(nq * kpt):
            count += ((fend_ref[row + jj] >= thr) & (jj < j_diag)).astype(jnp.int32)
        return count

    def step_one(hd, js):
        lanes_h = (lane >= hd * HEAD_DIM) & (lane < (hd + 1) * HEAD_DIM)
        sum_lane = (1 - hd) * HEAD_DIM
        lhs = lhs_ref[hd]
        vbs = []
        for n, j in enumerate(js):
            k0 = pl.multiple_of(j * tk, tk)
            w = jnp.concatenate([kt_ref[0, :, pl.ds(k0, tk)], kaugt_ref[0, :, pl.ds(k0, tk)]], axis=0)
            vbs.append(v_ref[0, pl.ds(k0, tk), :])
            s_ref[n] = _dot(lhs, w)
        for n in range(len(js)):
            for r0 in range(0, tq, rb):
                chunks = [s_ref[n, r0:r0 + rb, c0:c0 + LANES] for c0 in range(0, tk, LANES)]
                m_old = m_ref[hd, r0:r0 + rb, :]
                m_new = jnp.maximum(m_old, jnp.max(functools.reduce(jnp.maximum, chunks), axis=1, keepdims=True))
                alpha_ref[hd, r0:r0 + rb, :] = jnp.exp2(m_old - m_new)
                m_ref[hd, r0:r0 + rb, :] = m_new
                for c0, ch in zip(range(0, tk, LANES), chunks):
                    p_ref[n, r0:r0 + rb, c0:c0 + LANES] = jnp.exp2(ch - m_new).astype(_BF16)
            vaug = jnp.where(lanes_h, vbs[n], jnp.where(lane == sum_lane, 1.0, 0.0).astype(_BF16))
            acc_ref[hd] = alpha_ref[hd] * acc_ref[hd] + _dot(p_ref[n], vaug)

    jf0 = first_needed(0)
    jf1 = first_needed(1)
    j_lo = jnp.minimum(jf0, jf1)
    j_first = jnp.maximum(jf0, jf1)
    slow = (jf1 < jf0).astype(jnp.int32)
    n_one = j_first - j_lo

    @pl.when((n_one & 1) == 1)
    def _():
        step_one(slow, [j_lo])

    j_lo2 = j_lo + (n_one & 1)

    def pair_one(t, carry):
        step_one(slow, [j_lo2 + 2 * t, j_lo2 + 2 * t + 1])
        return carry

    lax.fori_loop(0, n_one >> 1, pair_one, 0)

    n_old = j_diag - j_first

    @pl.when((n_old & 1) == 1)
    def _():
        step(j_first, False)

    j_2 = j_first + (n_old & 1)

    @pl.when((n_old & 2) == 2)
    def _():
        step(j_2, False)
        step(j_2 + 1, False)

    j_4 = j_2 + (n_old & 2)

    def quad_step(t, carry):
        for c in range(4):
            step(j_4 + 4 * t + c, False)
        return carry

    lax.fori_loop(0, n_old >> 2, quad_step, 0)

    a0 = acc_ref[0]
    a1 = acc_ref[1]
    l0 = a0[:, HEAD_DIM:HEAD_DIM + 1]
    l1 = a1[:, 0:1]
    y = jnp.where(lane < HEAD_DIM, a0 / l0, a1 / l1)
    o_ref[0] = (y * za_ref[0].astype(_F32)).astype(_BF16)


def _fox_attention(base, fend, q, kt, kaugt, v, qaug, za, rb=32):
    bsz, s, _ = q.shape
    tq, tk = ATTN_TILE, ATTN_KBLOCK
    nq = s // tq
    qspec = pl.BlockSpec((1, tq, LANES), lambda b, hp, i, *_: (b, i, hp))
    grid_spec = pltpu.PrefetchScalarGridSpec(
        num_scalar_prefetch=2,
        grid=(bsz, HEAD_PAIRS, nq),
        in_specs=[qspec,
                  pl.BlockSpec((1, LANES, s), lambda b, hp, i, *_: (b, hp, 0)),
                  pl.BlockSpec((1, LANES, s), lambda b, hp, i, *_: (b, 0, 0)),
                  pl.BlockSpec((1, s, LANES), lambda b, hp, i, *_: (b, 0, hp)),
                  pl.BlockSpec((1, tq, LANES), lambda b, hp, i, *_: (b, i, 0)),
                  qspec],
        out_specs=qspec,
        scratch_shapes=[pltpu.VMEM((2, tq, 2 * LANES), _BF16),
                        pltpu.VMEM((2, tq, tk), _F32), pltpu.VMEM((2, tq, tk), _BF16),
                        pltpu.VMEM((2, tq, LANES), _F32), pltpu.VMEM((2, tq, LANES), _F32),
                        pltpu.VMEM((2, tq, LANES), _F32)])
    return pl.pallas_call(
        functools.partial(_attn_kernel, tq=tq, tk=tk, rb=rb, nq=nq),
        grid_spec=grid_spec,
        out_shape=jax.ShapeDtypeStruct((bsz, s, FOX_WIDTH), _BF16),
        compiler_params=pltpu.CompilerParams(
            dimension_semantics=("arbitrary", "arbitrary", "arbitrary"), vmem_limit_bytes=VMEM_LIMIT),
        name="fox_attn",
    )(base, fend, q, kt, kaugt, v, qaug, za)


def _gelu_tanh(x):
    return 0.5 * x * (1.0 + jnp.tanh(math.sqrt(2.0 / math.pi) * (x + 0.044715 * (x * x * x))))


def _s5_kernel(u_ref, zb_ref, ar_ref, ai_ref, bblk_ref, cblk_ref, dskip_ref, wglu_ref, bglu_ref,
               o_ref, xr_ref, xi_ref, sr_ref, si_ref, *, tt, nbatch, slabs_per_pass):
    t = pl.program_id(0)
    nchain = nbatch * S5_HALVES

    @pl.when(t == 0)
    def _():
        sr_ref[...] = jnp.zeros_like(sr_ref)
        si_ref[...] = jnp.zeros_like(si_ref)

    for hf in range(S5_HALVES):
        ub = jnp.concatenate([u_ref[b, :, hf * S5_HALF_CH:(hf + 1) * S5_HALF_CH] for b in range(nbatch)],
                             axis=0)
        bu = _dot(ub, bblk_ref[hf])
        for b in range(nbatch):
            kc = b * S5_HALVES + hf
            rows = slice(b * tt, (b + 1) * tt)
            for j in range(S5_SLABS):
                xr_ref[j, pl.ds(kc, tt, stride=nchain), :] = bu[rows, j * LANES:(j + 1) * LANES]
                xi_ref[j, pl.ds(kc, tt, stride=nchain), :] = bu[rows, (S5_SLABS + j) * LANES:
                                                                (S5_SLABS + j + 1) * LANES]

    for j0 in range(0, S5_SLABS, slabs_per_pass):
        js = list(range(j0, j0 + slabs_per_pass))
        ars = [ar_ref[j] for j in js]
        ais = [ai_ref[j] for j in js]

        def body(s, carry, js=js, ars=ars, ais=ais):
            r0 = pl.multiple_of(s * nchain, nchain)
            new = []
            for n, j in enumerate(js):
                sr, si = carry[2 * n], carry[2 * n + 1]
                br = xr_ref[j, pl.ds(r0, nchain), :]
                bi = xi_ref[j, pl.ds(r0, nchain), :]
                nr = (ars[n] * sr - ais[n] * si) + br
                ni = (ars[n] * si + ais[n] * sr) + bi
                xr_ref[j, pl.ds(r0, nchain), :] = nr
                xi_ref[j, pl.ds(r0, nchain), :] = ni
                new += [nr, ni]
            return tuple(new)

        init = []
        for j in js:
            init += [sr_ref[j], si_ref[j]]
        fin = lax.fori_loop(0, tt, body, tuple(init), unroll=4)
        for n, j in enumerate(js):
            sr_ref[j] = fin[2 * n]
            si_ref[j] = fin[2 * n + 1]

    halves = []
    for hf in range(S5_HALVES):
        xall = []
        for b in range(nbatch):
            kc = b * S5_HALVES + hf
            xs = [xr_ref[j, pl.ds(kc, tt, stride=nchain), :] for j in range(S5_SLABS)]
            xs += [xi_ref[j, pl.ds(kc, tt, stride=nchain), :] for j in range(S5_SLABS)]
            xall.append(jnp.concatenate(xs, axis=1).astype(_BF16))
        halves.append(_dot(jnp.concatenate(xall, axis=0), cblk_ref[hf]))
    w = u_ref.shape[-1]
    y = jnp.concatenate(halves, axis=1)
    y = y + dskip_ref[...] * u_ref[...].reshape(nbatch * tt, w).astype(_F32)
    y = _gelu_tanh(y)
    y = y * _sigmoid(_dot(y.astype(_BF16), wglu_ref[...]) + bglu_ref[...])
    y = y * zb_ref[...].reshape(nbatch * tt, w).astype(_F32)
    o_ref[...] = y.astype(_BF16).reshape(nbatch, tt, w)


def _s5_scan(u, zb, ar_t, ai_t, bblk, cblk, d_skip, w_glu, b_glu, tt=256, slabs_per_pass=8):
    bsz, s, w = u.shape
    nchain = bsz * S5_HALVES
    assert nchain == SUBLANES, "the scan packs batch x state-half chains into the 8 sublanes"
    tok_spec = pl.BlockSpec((bsz, tt, w), lambda t: (0, t, 0))
    const = lambda shape: pl.BlockSpec(shape, lambda t: (0,) * len(shape), pipeline_mode=pl.Buffered(1))
    return pl.pallas_call(
        functools.partial(_s5_kernel, tt=tt, nbatch=bsz, slabs_per_pass=slabs_per_pass),
        grid=(s // tt,),
        in_specs=[tok_spec, tok_spec, const(ar_t.shape), const(ai_t.shape),
                  const(bblk.shape), const(cblk.shape), const((1, w)), const(w_glu.shape), const((1, w))],
        out_specs=tok_spec,
        out_shape=jax.ShapeDtypeStruct((bsz, s, w), _BF16),
        scratch_shapes=[pltpu.VMEM((S5_SLABS, tt * nchain, LANES), _F32),
                        pltpu.VMEM((S5_SLABS, tt * nchain, LANES), _F32),
                        pltpu.VMEM((S5_SLABS, nchain, LANES), _F32),
                        pltpu.VMEM((S5_SLABS, nchain, LANES), _F32)],
        compiler_params=pltpu.CompilerParams(
            dimension_semantics=("arbitrary",), vmem_limit_bytes=VMEM_LIMIT),
        name="s5_scan",
    )(u, zb, ar_t, ai_t, bblk, cblk, d_skip.reshape(1, w), w_glu, b_glu.reshape(1, w))


def _out_kernel(x_ref, ya_ref, yb_ref, ga_ref, gb_ref, gate_ref, wua_ref, wub_ref, wo_ref, gf_ref, o_ref):
    ma = ga_ref[0].astype(_F32) * _dot(ya_ref[0], wua_ref[...])
    mb = gb_ref[0].astype(_F32) * _dot(yb_ref[0], wub_ref[...])
    merged = (ma + mb).astype(_BF16)
    xn = x_ref[0] + gate_ref[0] * _dot(merged, wo_ref[...])
    ms = jnp.mean(xn * xn, axis=-1, keepdims=True)
    o_ref[0] = xn * lax.rsqrt(ms + EPS) * gf_ref[...]


def _out_proj(x, ya, yb, ga, gb, gate, w_up_a, w_up_b, w_out, g_final, tm=1024):
    bsz, s, d = x.shape
    tok_spec = lambda width: pl.BlockSpec((1, tm, width), lambda b, i: (b, i, 0))
    const = lambda shape: pl.BlockSpec(shape, lambda b, i: (0,) * len(shape), pipeline_mode=pl.Buffered(1))
    return pl.pallas_call(
        _out_kernel,
        grid=(bsz, s // tm),
        in_specs=[tok_spec(d), tok_spec(FOX_WIDTH), tok_spec(S5_WIDTH), tok_spec(d), tok_spec(d),
                  pl.BlockSpec((1, 1, d), lambda b, i: (b, 0, 0)),
                  const(w_up_a.shape), const(w_up_b.shape), const(w_out.shape), const((1, d))],
        out_specs=tok_spec(d),
        out_shape=jax.ShapeDtypeStruct((bsz, s, d), x.dtype),
        compiler_params=pltpu.CompilerParams(
            dimension_semantics=("arbitrary", "arbitrary"), vmem_limit_bytes=VMEM_LIMIT),
        name="out_proj",
    )(x, ya, yb, ga, gb, gate, w_up_a, w_up_b, w_out, g_final.reshape(1, d))


def _chain_tiles(a, nbatch):
    a = a.reshape(S5_HALVES, S5_SLABS, LANES)
    a = jnp.swapaxes(a, 0, 1)
    return jnp.tile(a, (1, nbatch, 1))


def kernel(x, c, w_ada, b_ada, g_norm, w_in, b_f, a_re, a_im, log_dt, b_re, b_im, c_re, c_im,
           d_skip, w_glu, b_glu, w_up_a, w_up_b, w_out, g_final):
    bsz, s, d = x.shape
    depth = w_ada.shape[0]
    assert depth == 1 and d == D_MODEL

    mod = _adaln_mod(c, w_ada[0], b_ada[0])
    shift = mod[:, 0:d].reshape(bsz, 1, d)
    scale = mod[:, d:2 * d].reshape(bsz, 1, d)
    gate = mod[:, 2 * d:3 * d].reshape(bsz, 1, d)

    wt = jnp.swapaxes(w_in[0], 0, 1)
    o_z = 3 * FOX_WIDTH + HEADS
    wt_a = jnp.pad(wt[0:o_z], ((0, HEADS), (0, 0))).astype(_BF16)
    wt_b = wt[o_z:].astype(_BF16)

    q, kt, v, qaug, kaugt, za, u, zb, ga, gb, stats = _in_proj(
        x, g_norm[0], scale, shift, wt_a, wt_b, b_f[0])

    base = _skip_plan(stats)
    base_tab = jnp.swapaxes(base[:, :, 0:HEADS], 1, 2).reshape(-1)
    fend = stats[:, ST_FEND:ST_FEND + KB_PER_TILE, :, 0:HEADS]
    fend_tab = jnp.transpose(fend, (0, 3, 2, 1)).reshape(-1)
    ya = _fox_attention(base_tab, fend_tab, q, kt, kaugt, v, qaug, za)

    abr, abi, bblk, cblk = _s5_params(a_re[0], a_im[0], log_dt[0],
                                      jnp.swapaxes(b_re[0], 1, 2), jnp.swapaxes(b_im[0], 1, 2),
                                      jnp.swapaxes(c_re[0], 1, 2), jnp.swapaxes(c_im[0], 1, 2))
    yb = _s5_scan(u, zb, _chain_tiles(abr, bsz), _chain_tiles(abi, bsz), bblk, cblk,
                  d_skip[0], w_glu[0].astype(_BF16), b_glu[0])

    return _out_proj(x, ya, yb, ga, gb, gate, w_up_a[0].astype(_BF16), w_up_b[0].astype(_BF16),
                     w_out[0].astype(_BF16), g_final)
```

```python
import functools
import math

import jax
import jax.numpy as jnp
from jax import lax
from jax.experimental import pallas as pl
from jax.experimental.pallas import tpu as pltpu

D_MODEL = 1024
EPS = 1e-6
FOX_WIDTH = 512
HEAD_DIM = 64
HEADS = 8
HEAD_PAIRS = HEADS // 2
S5_WIDTH = 512
S5_GROUP = 16
S5_GROUPS = 32
S5_STATE = 64

LANES = 128
SUBLANES = 8
VMEM_LIMIT = 56 * 1024 * 1024

S5_HALVES = 2
S5_HALF_CH = S5_WIDTH // S5_HALVES
S5_HALF_STATE = S5_GROUPS * S5_STATE // S5_HALVES
S5_SLABS = S5_HALF_STATE // LANES

NEG_BIG = -1e30
LOG2E = math.log2(math.e)

ATTN_TILE = 512
ATTN_KBLOCK = 512
KB_PER_TILE = ATTN_TILE // ATTN_KBLOCK
ST_QN2, ST_FSTART, ST_KMAX, ST_FEND = 0, 1, 2, 3
ST_PLANES = ST_FEND + KB_PER_TILE
SKIP_LOG2 = 40.0
NORM_SAFETY = 1.02

_F32 = jnp.float32
_BF16 = jnp.bfloat16


def _dot(a, b):
    return jnp.dot(a, b, preferred_element_type=_F32)


def _dot_nt(a, b):
    return lax.dot_general(a, b, (((1,), (1,)), ((), ())), preferred_element_type=_F32)


def _split3(x):
    hi = x.astype(_BF16).astype(_F32)
    r1 = x - hi
    mid = r1.astype(_BF16).astype(_F32)
    lo = (r1 - mid).astype(_BF16).astype(_F32)
    return hi, mid, lo


def _sigmoid(x):
    return 1.0 / (1.0 + jnp.exp(-x))


def _silu(x):
    return x * _sigmoid(x)


def _adaln_kernel(ct_ref, w_ref, b_ref, o_ref):
    d, bn = w_ref.shape
    w3 = w_ref[...].reshape(d // SUBLANES, SUBLANES, bn)
    rows = []
    for b in range(ct_ref.shape[1]):
        cb = ct_ref[:, b:b + 1].reshape(d // SUBLANES, SUBLANES, 1)
        part = jnp.sum(w3 * cb, axis=0)
        rows.append(jnp.sum(part, axis=0, keepdims=True))
    o_ref[...] = jnp.concatenate(rows, axis=0) + b_ref[...]


def _adaln_mod(c, w, b):
    bsz, d = c.shape
    n = w.shape[1]
    bn = 512
    return pl.pallas_call(
        _adaln_kernel,
        grid=(n // bn,),
        in_specs=[pl.BlockSpec((d, bsz), lambda j: (0, 0)),
                  pl.BlockSpec((d, bn), lambda j: (0, j)),
                  pl.BlockSpec((1, bn), lambda j: (0, j))],
        out_specs=pl.BlockSpec((bsz, bn), lambda j: (0, j)),
        out_shape=jax.ShapeDtypeStruct((bsz, n), _F32),
        name="adaln_mod",
    )(c.T, w, b.reshape(1, n))


def _block_diag(x, nblk, reps_per_tile):
    rows, c = x.shape
    r = rows // nblk
    tile = jnp.concatenate([x] * reps_per_tile, axis=1)
    wide = jnp.concatenate([tile] * (nblk // reps_per_tile), axis=1)
    rb = lax.broadcasted_iota(jnp.int32, wide.shape, 0) // r
    cb = lax.broadcasted_iota(jnp.int32, wide.shape, 1) // c
    return jnp.where(rb == cb, wide, 0.0)


def _s5_param_kernel(are_ref, aim_ref, ldt_ref, bre_ref, bim_ref, cre_ref, cim_ref,
                     abr_ref, abi_ref, bblk_ref, cblk_ref):
    lr = are_ref[...]
    li = aim_ref[...]
    dt = jnp.exp(ldt_ref[...])
    mag = jnp.exp(lr * dt)
    ang = li * dt
    abr = mag * jnp.cos(ang)
    abi = mag * jnp.sin(ang)
    abr_ref[...] = abr
    abi_ref[...] = abi
    nr = abr - 1.0
    ni = abi
    den = lr * lr + li * li
    cr = (nr * lr + ni * li) / den
    ci = (ni * lr - nr * li) / den
    cr3 = cr[:, None, :]
    ci3 = ci[:, None, :]
    br = bre_ref[...]
    bi = bim_ref[...]
    bbr = cr3 * br - ci3 * bi
    bbi = cr3 * bi + ci3 * br

    gh = S5_GROUPS // S5_HALVES
    for hf in range(S5_HALVES):
        gsl = slice(hf * gh, (hf + 1) * gh)
        b_in = [_block_diag(t[gsl].reshape(gh * S5_GROUP, S5_STATE), gh, LANES // S5_STATE) for t in (bbr, bbi)]
        bblk_ref[hf] = jnp.concatenate(b_in, axis=1).astype(_BF16)
        c_out = [_block_diag(t[gsl].reshape(gh * S5_STATE, S5_GROUP), gh, LANES // S5_GROUP)
                 for t in (cre_ref[...], -cim_ref[...])]
        cblk_ref[hf] = jnp.concatenate(c_out, axis=0).astype(_BF16)


def _s5_params(a_re, a_im, log_dt, b_re_t, b_im_t, c_re_t, c_im_t):
    g, p = a_re.shape
    cg = b_re_t.shape[1]
    gh = g // S5_HALVES
    return pl.pallas_call(
        _s5_param_kernel,
        out_shape=(jax.ShapeDtypeStruct((g, p), _F32), jax.ShapeDtypeStruct((g, p), _F32),
                   jax.ShapeDtypeStruct((S5_HALVES, gh * cg, 2 * gh * p), _BF16),
                   jax.ShapeDtypeStruct((S5_HALVES, 2 * gh * p, gh * cg), _BF16)),
        name="s5_params",
    )(a_re, a_im, log_dt.reshape(g, 1), b_re_t, b_im_t, c_re_t, c_im_t)


def _in_proj_kernel(x_ref, gn_ref, sc_ref, sh_ref, wa_ref, wt_ref, bf_ref,
                    q_ref, kt_ref, v_ref, qaug_ref, kaugt_ref, za_ref, u_ref, zb_ref, ga_ref, gb_ref,
                    stats_ref, carry_ref, kmax_ref, *, tm):
    i = pl.program_id(1)
    fw = FOX_WIDTH
    c_v, c_f = 2 * fw, 3 * fw
    c_za = 0
    c_u, c_zb = c_za + fw, c_za + fw + S5_WIDTH
    c_ga = c_zb + S5_WIDTH
    c_gb = c_ga + D_MODEL

    @pl.when(i == 0)
    def _():
        carry_ref[...] = jnp.zeros_like(carry_ref)
        kmax_ref[...] = jnp.zeros_like(kmax_ref)

    x = x_ref[0]
    ms = jnp.mean(x * x, axis=-1, keepdims=True)
    h = x * lax.rsqrt(ms + EPS) * gn_ref[...]
    h = (h * (1.0 + sc_ref[0]) + sh_ref[0]).astype(_BF16)

    w = FOX_WIDTH
    qb = (_dot_nt(h, wa_ref[0:fw, :]) * (HEAD_DIM ** -0.5 * LOG2E)).astype(_BF16)
    q_ref[0] = qb
    ktb = _dot_nt(wa_ref[fw:c_v, :], h).astype(_BF16)
    kt_ref[0] = ktb
    v_ref[0] = _dot_nt(h, wa_ref[c_v:c_f, :]).astype(_BF16)

    seg = (lax.broadcasted_iota(jnp.int32, (w, LANES), 0) // HEAD_DIM
           == lax.broadcasted_iota(jnp.int32, (w, LANES), 1)).astype(_BF16)
    seg_t = (lax.broadcasted_iota(jnp.int32, (LANES, w), 1) // HEAD_DIM
             == lax.broadcasted_iota(jnp.int32, (LANES, w), 0)).astype(_BF16)
    qf = qb.astype(_F32)
    kf = ktb.astype(_F32)
    qn2 = _dot((qf * qf).astype(_BF16), seg)
    kn2 = _dot(seg_t, (kf * kf).astype(_BF16)).T
    kmax = jnp.maximum(kmax_ref[...], jnp.max(kn2, axis=0, keepdims=True))
    kmax_ref[...] = kmax
    za_ref[0] = _silu(_dot_nt(h, wt_ref[c_za:c_u, :])).astype(_BF16)
    u_ref[0] = _dot_nt(h, wt_ref[c_u:c_zb, :]).astype(_BF16)
    zb_ref[0] = _silu(_dot_nt(h, wt_ref[c_zb:c_ga, :])).astype(_BF16)
    ga_ref[0] = _sigmoid(_dot_nt(h, wt_ref[c_ga:c_gb, :])).astype(_BF16)
    gb_ref[0] = _sigmoid(_dot_nt(h, wt_ref[c_gb:c_gb + D_MODEL, :])).astype(_BF16)

    flt = _dot_nt(wa_ref[c_f:c_f + 2 * HEADS, :], h)[0:HEADS] + bf_ref[...]
    logft = -(jnp.maximum(-flt, 0.0) + jnp.log1p(jnp.exp(-jnp.abs(flt)))) * LOG2E
    lmh_t = jnp.concatenate(_split3(logft), axis=0).astype(_BF16)

    ta = ATTN_TILE
    row = lax.broadcasted_iota(jnp.int32, (ta, ta), 0)
    col = lax.broadcasted_iota(jnp.int32, (ta, ta), 1)
    triu = (row <= col).astype(_BF16)
    ones = jnp.ones((HEADS, ta), _F32)
    zeros = jnp.zeros((HEADS, ta), _F32)
    nt = tm // ta
    carry = carry_ref[...]
    for t in range(nt):
        r0 = t * ta
        parts = _dot(lmh_t[:, r0:r0 + ta], triu)
        cum_t = (parts[0:HEADS] + parts[HEADS:2 * HEADS]) + parts[2 * HEADS:3 * HEADS] + carry[:, 0:1]
        carry = jnp.broadcast_to(cum_t[:, ta - 1:ta], (HEADS, LANES))

        kh, km, kl = _split3(cum_t)
        kaugt_ref[0, :, r0:r0 + ta] = jnp.concatenate(
            [ones, ones, ones, zeros, kh, km, kl] + [zeros] * 9, axis=0).astype(_BF16)
        cum = jnp.concatenate([cum_t] + [zeros] * (LANES // HEADS - 1), axis=0).T
        qh, qm, ql = _split3(cum)
        qaug_ref[0, r0:r0 + ta, :] = (qh + pltpu.roll(qm, HEADS, 1)
                                      + pltpu.roll(ql, 2 * HEADS, 1)).astype(_BF16)

        tile = i * nt + t
        stats_ref[0, ST_QN2, pl.ds(tile, 1), :] = jnp.max(qn2[r0:r0 + ta], axis=0, keepdims=True)
        stats_ref[0, ST_FSTART, pl.ds(tile, 1), :] = cum[0:1, :]
        for c in range(KB_PER_TILE):
            last = (c + 1) * ATTN_KBLOCK - 1
            stats_ref[0, ST_FEND + c, pl.ds(tile, 1), :] = cum[last:last + 1, :]
    carry_ref[...] = carry
    stats_ref[0, ST_KMAX] = jnp.broadcast_to(kmax, stats_ref.shape[2:])


def _in_proj(x, g_norm, scale, shift, wt_a, wt_b, b_f, tm=1024):
    bsz, s, d = x.shape
    ni = s // tm
    tok = lambda width, dt: jax.ShapeDtypeStruct((bsz, s, width), dt)
    tok_spec = lambda width: pl.BlockSpec((1, tm, width), lambda b, i: (b, i, 0))
    tr = lambda rows: jax.ShapeDtypeStruct((bsz, rows, s), _BF16)
    tr_spec = lambda rows: pl.BlockSpec((1, rows, tm), lambda b, i: (b, 0, i))
    const = lambda shape: pl.BlockSpec(shape, lambda b, i: (0,) * len(shape), pipeline_mode=pl.Buffered(1))
    per_b = pl.BlockSpec((1, 1, d), lambda b, i: (b, 0, 0))
    return pl.pallas_call(
        functools.partial(_in_proj_kernel, tm=tm),
        grid=(bsz, ni),
        in_specs=[tok_spec(d), const((1, d)), per_b, per_b,
                  const(wt_a.shape), const(wt_b.shape), const((HEADS, 1))],
        out_specs=[tok_spec(FOX_WIDTH), tr_spec(FOX_WIDTH), tok_spec(FOX_WIDTH),
                   tok_spec(LANES), tr_spec(LANES),
                   tok_spec(FOX_WIDTH), tok_spec(S5_WIDTH), tok_spec(S5_WIDTH),
                   tok_spec(d), tok_spec(d),
                   pl.BlockSpec((1, ST_PLANES, s // ATTN_TILE, LANES), lambda b, i: (b, 0, 0, 0))],
        out_shape=[tok(FOX_WIDTH, _BF16), tr(FOX_WIDTH), tok(FOX_WIDTH, _BF16),
                   tok(LANES, _BF16), tr(LANES),
                   tok(FOX_WIDTH, _BF16), tok(S5_WIDTH, _BF16), tok(S5_WIDTH, _BF16),
                   tok(d, _BF16), tok(d, _BF16),
                   jax.ShapeDtypeStruct((bsz, ST_PLANES, s // ATTN_TILE, LANES), _F32)],
        scratch_shapes=[pltpu.VMEM((HEADS, LANES), _F32), pltpu.VMEM((1, LANES), _F32)],
        compiler_params=pltpu.CompilerParams(
            dimension_semantics=("arbitrary", "arbitrary"), vmem_limit_bytes=VMEM_LIMIT),
        name="in_proj",
    )(x, g_norm.reshape(1, d), scale, shift, wt_a, wt_b, b_f.reshape(HEADS, 1))


def _plan_kernel(st_ref, o_ref):
    qn2 = st_ref[0, ST_QN2]
    fstart = st_ref[0, ST_FSTART]
    kmax2 = st_ref[0, ST_KMAX]
    o_ref[0] = fstart + (NORM_SAFETY * jnp.sqrt(qn2 * kmax2) + SKIP_LOG2)


def _skip_plan(stats):
    bsz, _, nq, _ = stats.shape
    return pl.pallas_call(
        _plan_kernel,
        grid=(bsz,),
        in_specs=[pl.BlockSpec((1, ST_PLANES, nq, LANES), lambda b: (b, 0, 0, 0))],
        out_specs=pl.BlockSpec((1, nq, LANES), lambda b: (b, 0, 0)),
        out_shape=jax.ShapeDtypeStruct((bsz, nq, LANES), _F32),
        name="skip_plan",
    )(stats)


def _attn_kernel(base_ref, fend_ref, q_ref, kt_ref, kaugt_ref, v_ref, qaug_ref, za_ref, o_ref,
                 lhs_ref, s_ref, p_ref, m_ref, alpha_ref, acc_ref, *, tq, tk, rb, nq):
    b = pl.program_id(0)
    hp = pl.program_id(1)
    i = pl.program_id(2)
    lane = lax.broadcasted_iota(jnp.int32, (1, LANES), 1)
    q = q_ref[0]
    qa = qaug_ref[0]

    head_lanes = [(lane >= hh * HEAD_DIM) & (lane < (hh + 1) * HEAD_DIM) for hh in range(2)]
    for hh in range(2):
        h = 2 * hp + hh
        f_lanes = (lane == h) | (lane == HEADS + h) | (lane == 2 * HEADS + h)
        neg_lanes = (lane == 4 * HEADS + h) | (lane == 5 * HEADS + h) | (lane == 6 * HEADS + h)
        qm = jnp.where(head_lanes[hh], q, jnp.zeros_like(q))
        qaug = jnp.where(f_lanes, qa, jnp.where(neg_lanes, -1.0, 0.0).astype(_BF16))
        lhs_ref[hh] = jnp.concatenate([qm, qaug], axis=1)
        m_ref[hh] = jnp.full((tq, LANES), NEG_BIG, _F32)
        acc_ref[hh] = jnp.zeros((tq, LANES), _F32)

    def step(j, masked):
        k0 = pl.multiple_of(j * tk, tk)
        w = jnp.concatenate([kt_ref[0, :, pl.ds(k0, tk)], kaugt_ref[0, :, pl.ds(k0, tk)]], axis=0)
        vb = v_ref[0, pl.ds(k0, tk), :]
        for hh in range(2):
            s = _dot(lhs_ref[hh], w)
            if masked:
                r = lax.broadcasted_iota(jnp.int32, (tq, tk), 0)
                c = lax.broadcasted_iota(jnp.int32, (tq, tk), 1)
                s = jnp.where(c - r <= i * tq - j * tk, s, NEG_BIG)
            s_ref[hh] = s
        for hh in range(2):
            for r0 in range(0, tq, rb):
                chunks = [s_ref[hh, r0:r0 + rb, c0:c0 + LANES] for c0 in range(0, tk, LANES)]
                m_old = m_ref[hh, r0:r0 + rb, :]
                m_new = jnp.maximum(m_old, jnp.max(functools.reduce(jnp.maximum, chunks), axis=1, keepdims=True))
                alpha_ref[hh, r0:r0 + rb, :] = jnp.exp2(m_old - m_new)
                m_ref[hh, r0:r0 + rb, :] = m_new
                for c0, ch in zip(range(0, tk, LANES), chunks):
                    p_ref[hh, r0:r0 + rb, c0:c0 + LANES] = jnp.exp2(ch - m_new).astype(_BF16)
        for hh in range(2):
            sum_lane = (1 - hh) * HEAD_DIM
            vaug = jnp.where(head_lanes[hh], vb, jnp.where(lane == sum_lane, 1.0, 0.0).astype(_BF16))
            acc_ref[hh] = alpha_ref[hh] * acc_ref[hh] + _dot(p_ref[hh], vaug)

    kpt = tq // tk
    j_diag = i * kpt

    for c in range(kpt):
        step(j_diag + c, True)

    def first_needed(hh):
        thr = base_ref[(b * HEADS + 2 * hp + hh) * nq + i] - jnp.min(m_ref[hh])
        row = (b * HEADS + 2 * hp + hh) * (nq * kpt)
        count = jnp.int32(0)
        for jj in range(nq * kpt):
            count += ((fend_ref[row + jj] >= thr) & (jj < j_diag)).astype(jnp.int32)
        return count

    def step_one(hd, js):
        lanes_h = (lane >= hd * HEAD_DIM) & (lane < (hd + 1) * HEAD_DIM)
        sum_lane = (1 - hd) * HEAD_DIM
        lhs = lhs_ref[hd]
        vbs = []
        for n, j in enumerate(js):
            k0 = pl.multiple_of(j * tk, tk)
            w = jnp.concatenate([kt_ref[0, :, pl.ds(k0, tk)], kaugt_ref[0, :, pl.ds(k0, tk)]], axis=0)
            vbs.append(v_ref[0, pl.ds(k0, tk), :])
            s_ref[n] = _dot(lhs, w)
        for n in range(len(js)):
            for r0 in range(0, tq, rb):
                chunks = [s_ref[n, r0:r0 + rb, c0:c0 + LANES] for c0 in range(0, tk, LANES)]
                m_old = m_ref[hd, r0:r0 + rb, :]
                m_new = jnp.maximum(m_old, jnp.max(functools.reduce(jnp.maximum, chunks), axis=1, keepdims=True))
                alpha_ref[hd, r0:r0 + rb, :] = jnp.exp2(m_old - m_new)
                m_ref[hd, r0:r0 + rb, :] = m_new
                for c0, ch in zip(range(0, tk, LANES), chunks):
                    p_ref[n, r0:r0 + rb, c0:c0 + LANES] = jnp.exp2(ch - m_new).astype(_BF16)
            vaug = jnp.where(lanes_h, vbs[n], jnp.where(lane == sum_lane, 1.0, 0.0).astype(_BF16))
            acc_ref[hd] = alpha_ref[hd] * acc_ref[hd] + _dot(p_ref[n], vaug)

    jf0 = first_needed(0)
    jf1 = first_needed(1)
    j_lo = jnp.minimum(jf0, jf1)
    j_first = jnp.maximum(jf0, jf1)
    slow = (jf1 < jf0).astype(jnp.int32)
    n_one = j_first - j_lo

    @pl.when((n_one & 1) == 1)
    def _():
        step_one(slow, [j_lo])

    j_lo2 = j_lo + (n_one & 1)

    def pair_one(t, carry):
        step_one(slow, [j_lo2 + 2 * t, j_lo2 + 2 * t + 1])
        return carry

    lax.fori_loop(0, n_one >> 1, pair_one, 0)

    n_old = j_diag - j_first

    @pl.when((n_old & 1) == 1)
    def _():
        step(j_first, False)

    j_2 = j_first + (n_old & 1)

    @pl.when((n_old & 2) == 2)
    def _():
        step(j_2, False)
        step(j_2 + 1, False)

    j_4 = j_2 + (n_old & 2)

    def quad_step(t, carry):
        for c in range(4):
            step(j_4 + 4 * t + c, False)
        return carry

    lax.fori_loop(0, n_old >> 2, quad_step, 0)

    a0 = acc_ref[0]
    a1 = acc_ref[1]
    l0 = a0[:, HEAD_DIM:HEAD_DIM + 1]
    l1 = a1[:, 0:1]
    y = jnp.where(lane < HEAD_DIM, a0 / l0, a1 / l1)
    o_ref[0] = (y * za_ref[0].astype(_F32)).astype(_BF16)


def _fox_attention(base, fend, q, kt, kaugt, v, qaug, za, rb=32):
    bsz, s, _ = q.shape
    tq, tk = ATTN_TILE, ATTN_KBLOCK
    nq = s // tq
    qspec = pl.BlockSpec((1, tq, LANES), lambda b, hp, i, *_: (b, i, hp))
    grid_spec = pltpu.PrefetchScalarGridSpec(
        num_scalar_prefetch=2,
        grid=(bsz, HEAD_PAIRS, nq),
        in_specs=[qspec,
                  pl.BlockSpec((1, LANES, s), lambda b, hp, i, *_: (b, hp, 0)),
                  pl.BlockSpec((1, LANES, s), lambda b, hp, i, *_: (b, 0, 0)),
                  pl.BlockSpec((1, s, LANES), lambda b, hp, i, *_: (b, 0, hp)),
                  pl.BlockSpec((1, tq, LANES), lambda b, hp, i, *_: (b, i, 0)),
                  qspec],
        out_specs=qspec,
        scratch_shapes=[pltpu.VMEM((2, tq, 2 * LANES), _BF16),
                        pltpu.VMEM((2, tq, tk), _F32), pltpu.VMEM((2, tq, tk), _BF16),
                        pltpu.VMEM((2, tq, LANES), _F32), pltpu.VMEM((2, tq, LANES), _F32),
                        pltpu.VMEM((2, tq, LANES), _F32)])
    return pl.pallas_call(
        functools.partial(_attn_kernel, tq=tq, tk=tk, rb=rb, nq=nq),
        grid_spec=grid_spec,
        out_shape=jax.ShapeDtypeStruct((bsz, s, FOX_WIDTH), _BF16),
        compiler_params=pltpu.CompilerParams(
            dimension_semantics=("arbitrary", "arbitrary", "arbitrary"), vmem_limit_bytes=VMEM_LIMIT),
        name="fox_attn",
    )(base, fend, q, kt, kaugt, v, qaug, za)


def _gelu_tanh(x):
    return 0.5 * x * (1.0 + jnp.tanh(math.sqrt(2.0 / math.pi) * (x + 0.044715 * (x * x * x))))


def _s5_kernel(u_ref, zb_ref, ar_ref, ai_ref, bblk_ref, cblk_ref, dskip_ref, wglu_ref, bglu_ref,
               o_ref, xr_ref, xi_ref, sr_ref, si_ref, usb_ref, ysb_ref, *, tt, nbatch, slabs_per_pass):
    t = pl.program_id(0)
    nchain = nbatch * S5_HALVES
    w = u_ref.shape[-1]
    wslabs = w // LANES
    nrow = nbatch * tt

    @pl.when(t == 0)
    def _():
        sr_ref[...] = jnp.zeros_like(sr_ref)
        si_ref[...] = jnp.zeros_like(si_ref)

    for b in range(nbatch):
        ub = u_ref[b].astype(_F32)
        for jl in range(wslabs):
            usb_ref[jl, pl.ds(b, tt, stride=nbatch), :] = ub[:, jl * LANES:(jl + 1) * LANES]
    u_sb = jnp.concatenate([usb_ref[jl] for jl in range(wslabs)], axis=1)

    for hf in range(S5_HALVES):
        ub = u_sb[:, hf * S5_HALF_CH:(hf + 1) * S5_HALF_CH].astype(_BF16)
        bu = _dot(ub, bblk_ref[hf])
        for j in range(S5_SLABS):
            xr_ref[j, pl.ds(hf, nrow, stride=S5_HALVES), :] = bu[:, j * LANES:(j + 1) * LANES]
            xi_ref[j, pl.ds(hf, nrow, stride=S5_HALVES), :] = bu[:, (S5_SLABS + j) * LANES:
                                                                 (S5_SLABS + j + 1) * LANES]

    for j0 in range(0, S5_SLABS, slabs_per_pass):
        js = list(range(j0, j0 + slabs_per_pass))
        ars = [ar_ref[j] for j in js]
        ais = [ai_ref[j] for j in js]

        def body(s, carry, js=js, ars=ars, ais=ais):
            r0 = pl.multiple_of(s * nchain, nchain)
            new = []
            for n, j in enumerate(js):
                sr, si = carry[2 * n], carry[2 * n + 1]
                br = xr_ref[j, pl.ds(r0, nchain), :]
                bi = xi_ref[j, pl.ds(r0, nchain), :]
                nr = (ars[n] * sr - ais[n] * si) + br
                ni = (ars[n] * si + ais[n] * sr) + bi
                xr_ref[j, pl.ds(r0, nchain), :] = nr
                xi_ref[j, pl.ds(r0, nchain), :] = ni
                new += [nr, ni]
            return tuple(new)

        init = []
        for j in js:
            init += [sr_ref[j], si_ref[j]]
        fin = lax.fori_loop(0, tt, body, tuple(init), unroll=4)
        for n, j in enumerate(js):
            sr_ref[j] = fin[2 * n]
            si_ref[j] = fin[2 * n + 1]

    halves = []
    for hf in range(S5_HALVES):
        xs = [xr_ref[j, pl.ds(hf, nrow, stride=S5_HALVES), :] for j in range(S5_SLABS)]
        xs += [xi_ref[j, pl.ds(hf, nrow, stride=S5_HALVES), :] for j in range(S5_SLABS)]
        halves.append(_dot(jnp.concatenate(xs, axis=1).astype(_BF16), cblk_ref[hf]))
    y = jnp.concatenate(halves, axis=1)
    y = y + dskip_ref[...] * u_sb
    y = _gelu_tanh(y)
    y = y * _sigmoid(_dot(y.astype(_BF16), wglu_ref[...]) + bglu_ref[...])
    for jl in range(wslabs):
        ysb_ref[jl] = y[:, jl * LANES:(jl + 1) * LANES]
    for b in range(nbatch):
        yb = jnp.concatenate([ysb_ref[jl, pl.ds(b, tt, stride=nbatch), :] for jl in range(wslabs)], axis=1)
        o_ref[b] = (yb * zb_ref[b].astype(_F32)).astype(_BF16)


def _s5_scan(u, zb, ar_t, ai_t, bblk, cblk, d_skip, w_glu, b_glu, tt=256, slabs_per_pass=8):
    bsz, s, w = u.shape
    nchain = bsz * S5_HALVES
    assert nchain == SUBLANES, "the scan packs batch x state-half chains into the 8 sublanes"
    tok_spec = pl.BlockSpec((bsz, tt, w), lambda t: (0, t, 0))
    const = lambda shape: pl.BlockSpec(shape, lambda t: (0,) * len(shape), pipeline_mode=pl.Buffered(1))
    return pl.pallas_call(
        functools.partial(_s5_kernel, tt=tt, nbatch=bsz, slabs_per_pass=slabs_per_pass),
        grid=(s // tt,),
        in_specs=[tok_spec, tok_spec, const(ar_t.shape), const(ai_t.shape),
                  const(bblk.shape), const(cblk.shape), const((1, w)), const(w_glu.shape), const((1, w))],
        out_specs=tok_spec,
        out_shape=jax.ShapeDtypeStruct((bsz, s, w), _BF16),
        scratch_shapes=[pltpu.VMEM((S5_SLABS, tt * nchain, LANES), _F32),
                        pltpu.VMEM((S5_SLABS, tt * nchain, LANES), _F32),
                        pltpu.VMEM((S5_SLABS, nchain, LANES), _F32),
                        pltpu.VMEM((S5_SLABS, nchain, LANES), _F32),
                        pltpu.VMEM((w // LANES, bsz * tt, LANES), _F32),
                        pltpu.VMEM((w // LANES, bsz * tt, LANES), _F32)],
        compiler_params=pltpu.CompilerParams(
            dimension_semantics=("arbitrary",), vmem_limit_bytes=VMEM_LIMIT),
        name="s5_scan",
    )(u, zb, ar_t, ai_t, bblk, cblk, d_skip.reshape(1, w), w_glu, b_glu.reshape(1, w))


def _out_kernel(x_ref, ya_ref, yb_ref, ga_ref, gb_ref, gate_ref, wua_ref, wub_ref, wo_ref, gf_ref, o_ref):
    ma = ga_ref[0].astype(_F32) * _dot(ya_ref[0], wua_ref[...])
    mb = gb_ref[0].astype(_F32) * _dot(yb_ref[0], wub_ref[...])
    merged = (ma + mb).astype(_BF16)
    xn = x_ref[0] + gate_ref[0] * _dot(merged, wo_ref[...])
    ms = jnp.mean(xn * xn, axis=-1, keepdims=True)
    o_ref[0] = xn * lax.rsqrt(ms + EPS) * gf_ref[...]


def _out_proj(x, ya, yb, ga, gb, gate, w_up_a, w_up_b, w_out, g_final, tm=1024):
    bsz, s, d = x.shape
    tok_spec = lambda width: pl.BlockSpec((1, tm, width), lambda b, i: (b, i, 0))
    const = lambda shape: pl.BlockSpec(shape, lambda b, i: (0,) * len(shape), pipeline_mode=pl.Buffered(1))
    return pl.pallas_call(
        _out_kernel,
        grid=(bsz, s // tm),
        in_specs=[tok_spec(d), tok_spec(FOX_WIDTH), tok_spec(S5_WIDTH), tok_spec(d), tok_spec(d),
                  pl.BlockSpec((1, 1, d), lambda b, i: (b, 0, 0)),
                  const(w_up_a.shape), const(w_up_b.shape), const(w_out.shape), const((1, d))],
        out_specs=tok_spec(d),
        out_shape=jax.ShapeDtypeStruct((bsz, s, d), x.dtype),
        compiler_params=pltpu.CompilerParams(
            dimension_semantics=("arbitrary", "arbitrary"), vmem_limit_bytes=VMEM_LIMIT),
        name="out_proj",
    )(x, ya, yb, ga, gb, gate, w_up_a, w_up_b, w_out, g_final.reshape(1, d))


def _chain_tiles(a, nbatch):
    a = a.reshape(S5_HALVES, S5_SLABS, LANES)
    a = jnp.swapaxes(a, 0, 1)
    return jnp.tile(a, (1, nbatch, 1))


def kernel(x, c, w_ada, b_ada, g_norm, w_in, b_f, a_re, a_im, log_dt, b_re, b_im, c_re, c_im,
           d_skip, w_glu, b_glu, w_up_a, w_up_b, w_out, g_final):
    bsz, s, d = x.shape
    depth = w_ada.shape[0]
    assert depth == 1 and d == D_MODEL

    mod = _adaln_mod(c, w_ada[0], b_ada[0])
    shift = mod[:, 0:d].reshape(bsz, 1, d)
    scale = mod[:, d:2 * d].reshape(bsz, 1, d)
    gate = mod[:, 2 * d:3 * d].reshape(bsz, 1, d)

    wt = jnp.swapaxes(w_in[0], 0, 1)
    o_z = 3 * FOX_WIDTH + HEADS
    wt_a = jnp.pad(wt[0:o_z], ((0, HEADS), (0, 0))).astype(_BF16)
    wt_b = wt[o_z:].astype(_BF16)

    q, kt, v, qaug, kaugt, za, u, zb, ga, gb, stats = _in_proj(
        x, g_norm[0], scale, shift, wt_a, wt_b, b_f[0])

    base = _skip_plan(stats)
    base_tab = jnp.swapaxes(base[:, :, 0:HEADS], 1, 2).reshape(-1)
    fend = stats[:, ST_FEND:ST_FEND + KB_PER_TILE, :, 0:HEADS]
    fend_tab = jnp.transpose(fend, (0, 3, 2, 1)).reshape(-1)
    ya = _fox_attention(base_tab, fend_tab, q, kt, kaugt, v, qaug, za)

    abr, abi, bblk, cblk = _s5_params(a_re[0], a_im[0], log_dt[0],
                                      jnp.swapaxes(b_re[0], 1, 2), jnp.swapaxes(b_im[0], 1, 2),
                                      jnp.swapaxes(c_re[0], 1, 2), jnp.swapaxes(c_im[0], 1, 2))
    yb = _s5_scan(u, zb, _chain_tiles(abr, bsz), _chain_tiles(abi, bsz), bblk, cblk,
                  d_skip[0], w_glu[0].astype(_BF16), b_glu[0])

    return _out_proj(x, ya, yb, ga, gb, gate, w_up_a[0].astype(_BF16), w_up_b[0].astype(_BF16),
                     w_out[0].astype(_BF16), g_final)
```

```python
import functools
import math

import jax
import jax.numpy as jnp
from jax import lax
from jax.experimental import pallas as pl
from jax.experimental.pallas import tpu as pltpu

D_MODEL = 1024
EPS = 1e-6
FOX_WIDTH = 512
HEAD_DIM = 64
HEADS = 8
HEAD_PAIRS = HEADS // 2
S5_WIDTH = 512
S5_GROUP = 16
S5_GROUPS = 32
S5_STATE = 64

LANES = 128
SUBLANES = 8
VMEM_LIMIT = 56 * 1024 * 1024

S5_HALVES = 2
S5_HALF_CH = S5_WIDTH // S5_HALVES
S5_HALF_STATE = S5_GROUPS * S5_STATE // S5_HALVES
S5_SLABS = S5_HALF_STATE // LANES

NEG_BIG = -1e30
LOG2E = math.log2(math.e)

ATTN_TILE = 512
ATTN_KBLOCK = 512
KB_PER_TILE = ATTN_TILE // ATTN_KBLOCK
ST_QN2, ST_FSTART, ST_KMAX, ST_FEND = 0, 1, 2, 3
ST_PLANES = ST_FEND + KB_PER_TILE
SKIP_LOG2 = 40.0
NORM_SAFETY = 1.02

_F32 = jnp.float32
_BF16 = jnp.bfloat16


def _dot(a, b):
    return jnp.dot(a, b, preferred_element_type=_F32)


def _dot_nt(a, b):
    return lax.dot_general(a, b, (((1,), (1,)), ((), ())), preferred_element_type=_F32)


def _split3(x):
    hi = x.astype(_BF16).astype(_F32)
    r1 = x - hi
    mid = r1.astype(_BF16).astype(_F32)
    lo = (r1 - mid).astype(_BF16).astype(_F32)
    return hi, mid, lo


def _sigmoid(x):
    return 1.0 / (1.0 + jnp.exp(-x))


def _silu(x):
    return x * _sigmoid(x)


def _adaln_kernel(ct_ref, w_ref, b_ref, o_ref):
    d, bn = w_ref.shape
    w3 = w_ref[...].reshape(d // SUBLANES, SUBLANES, bn)
    rows = []
    for b in range(ct_ref.shape[1]):
        cb = ct_ref[:, b:b + 1].reshape(d // SUBLANES, SUBLANES, 1)
        part = jnp.sum(w3 * cb, axis=0)
        rows.append(jnp.sum(part, axis=0, keepdims=True))
    o_ref[...] = jnp.concatenate(rows, axis=0) + b_ref[...]


def _adaln_mod(c, w, b):
    bsz, d = c.shape
    n = w.shape[1]
    bn = 512
    return pl.pallas_call(
        _adaln_kernel,
        grid=(n // bn,),
        in_specs=[pl.BlockSpec((d, bsz), lambda j: (0, 0)),
                  pl.BlockSpec((d, bn), lambda j: (0, j)),
                  pl.BlockSpec((1, bn), lambda j: (0, j))],
        out_specs=pl.BlockSpec((bsz, bn), lambda j: (0, j)),
        out_shape=jax.ShapeDtypeStruct((bsz, n), _F32),
        name="adaln_mod",
    )(c.T, w, b.reshape(1, n))


def _block_diag(x, nblk, reps_per_tile):
    rows, c = x.shape
    r = rows // nblk
    tile = jnp.concatenate([x] * reps_per_tile, axis=1)
    wide = jnp.concatenate([tile] * (nblk // reps_per_tile), axis=1)
    rb = lax.broadcasted_iota(jnp.int32, wide.shape, 0) // r
    cb = lax.broadcasted_iota(jnp.int32, wide.shape, 1) // c
    return jnp.where(rb == cb, wide, 0.0)


def _s5_param_kernel(are_ref, aim_ref, ldt_ref, bre_ref, bim_ref, cre_ref, cim_ref,
                     abr_ref, abi_ref, bblk_ref, cblk_ref):
    lr = are_ref[...]
    li = aim_ref[...]
    dt = jnp.exp(ldt_ref[...])
    mag = jnp.exp(lr * dt)
    ang = li * dt
    abr = mag * jnp.cos(ang)
    abi = mag * jnp.sin(ang)
    abr_ref[...] = abr
    abi_ref[...] = abi
    nr = abr - 1.0
    ni = abi
    den = lr * lr + li * li
    cr = (nr * lr + ni * li) / den
    ci = (ni * lr - nr * li) / den
    cr3 = cr[:, None, :]
    ci3 = ci[:, None, :]
    br = bre_ref[...]
    bi = bim_ref[...]
    bbr = cr3 * br - ci3 * bi
    bbi = cr3 * bi + ci3 * br

    gh = S5_GROUPS // S5_HALVES
    for hf in range(S5_HALVES):
        gsl = slice(hf * gh, (hf + 1) * gh)
        b_in = [_block_diag(t[gsl].reshape(gh * S5_GROUP, S5_STATE), gh, LANES // S5_STATE) for t in (bbr, bbi)]
        bblk_ref[hf] = jnp.concatenate(b_in, axis=1).astype(_BF16)
        c_out = [_block_diag(t[gsl].reshape(gh * S5_STATE, S5_GROUP), gh, LANES // S5_GROUP)
                 for t in (cre_ref[...], -cim_ref[...])]
        cblk_ref[hf] = jnp.concatenate(c_out, axis=0).astype(_BF16)


def _s5_params(a_re, a_im, log_dt, b_re_t, b_im_t, c_re_t, c_im_t):
    g, p = a_re.shape
    cg = b_re_t.shape[1]
    gh = g // S5_HALVES
    return pl.pallas_call(
        _s5_param_kernel,
        out_shape=(jax.ShapeDtypeStruct((g, p), _F32), jax.ShapeDtypeStruct((g, p), _F32),
                   jax.ShapeDtypeStruct((S5_HALVES, gh * cg, 2 * gh * p), _BF16),
                   jax.ShapeDtypeStruct((S5_HALVES, 2 * gh * p, gh * cg), _BF16)),
        name="s5_params",
    )(a_re, a_im, log_dt.reshape(g, 1), b_re_t, b_im_t, c_re_t, c_im_t)


def _in_proj_kernel(x_ref, gn_ref, sc_ref, sh_ref, wa_ref, wt_ref, bf_ref,
                    q_ref, kt_ref, v_ref, qaug_ref, kaugt_ref, za_ref, u_ref, zb_ref, ga_ref, gb_ref,
                    stats_ref, carry_ref, kmax_ref, *, tm):
    i = pl.program_id(1)
    fw = FOX_WIDTH
    c_v, c_f = 2 * fw, 3 * fw
    c_za = 0
    c_u, c_zb = c_za + fw, c_za + fw + S5_WIDTH
    c_ga = c_zb + S5_WIDTH
    c_gb = c_ga + D_MODEL

    @pl.when(i == 0)
    def _():
        carry_ref[...] = jnp.zeros_like(carry_ref)
        kmax_ref[...] = jnp.zeros_like(kmax_ref)

    x = x_ref[0]
    ms = jnp.mean(x * x, axis=-1, keepdims=True)
    h = x * lax.rsqrt(ms + EPS) * gn_ref[...]
    h = (h * (1.0 + sc_ref[0]) + sh_ref[0]).astype(_BF16)

    w = FOX_WIDTH
    qb = (_dot_nt(h, wa_ref[0:fw, :]) * (HEAD_DIM ** -0.5 * LOG2E)).astype(_BF16)
    q_ref[0] = qb
    ktb = _dot_nt(wa_ref[fw:c_v, :], h).astype(_BF16)
    kt_ref[0] = ktb
    v_ref[0] = _dot_nt(h, wa_ref[c_v:c_f, :]).astype(_BF16)

    seg = (lax.broadcasted_iota(jnp.int32, (w, LANES), 0) // HEAD_DIM
           == lax.broadcasted_iota(jnp.int32, (w, LANES), 1)).astype(_BF16)
    seg_t = (lax.broadcasted_iota(jnp.int32, (LANES, w), 1) // HEAD_DIM
             == lax.broadcasted_iota(jnp.int32, (LANES, w), 0)).astype(_BF16)
    qf = qb.astype(_F32)
    kf = ktb.astype(_F32)
    qn2 = _dot((qf * qf).astype(_BF16), seg)
    kn2 = _dot(seg_t, (kf * kf).astype(_BF16)).T
    kmax = jnp.maximum(kmax_ref[...], jnp.max(kn2, axis=0, keepdims=True))
    kmax_ref[...] = kmax
    za_ref[0] = _silu(_dot_nt(h, wt_ref[c_za:c_u, :])).astype(_BF16)
    u_ref[0] = _dot_nt(h, wt_ref[c_u:c_zb, :]).astype(_BF16)
    zb_ref[0] = _silu(_dot_nt(h, wt_ref[c_zb:c_ga, :])).astype(_BF16)
    ga_ref[0] = _sigmoid(_dot_nt(h, wt_ref[c_ga:c_gb, :])).astype(_BF16)
    gb_ref[0] = _sigmoid(_dot_nt(h, wt_ref[c_gb:c_gb + D_MODEL, :])).astype(_BF16)

    flt = _dot_nt(wa_ref[c_f:c_f + 2 * HEADS, :], h)[0:HEADS] + bf_ref[...]
    logft = -(jnp.maximum(-flt, 0.0) + jnp.log1p(jnp.exp(-jnp.abs(flt)))) * LOG2E
    lmh_t = jnp.concatenate(_split3(logft), axis=0).astype(_BF16)

    ta = ATTN_TILE
    row = lax.broadcasted_iota(jnp.int32, (ta, ta), 0)
    col = lax.broadcasted_iota(jnp.int32, (ta, ta), 1)
    triu = (row <= col).astype(_BF16)
    ones = jnp.ones((HEADS, ta), _F32)
    zeros = jnp.zeros((HEADS, ta), _F32)
    nt = tm // ta
    carry = carry_ref[...]
    for t in range(nt):
        r0 = t * ta
        parts = _dot(lmh_t[:, r0:r0 + ta], triu)
        cum_t = (parts[0:HEADS] + parts[HEADS:2 * HEADS]) + parts[2 * HEADS:3 * HEADS] + carry[:, 0:1]
        carry = jnp.broadcast_to(cum_t[:, ta - 1:ta], (HEADS, LANES))

        kh, km, kl = _split3(cum_t)
        kaugt_ref[0, :, r0:r0 + ta] = jnp.concatenate(
            [ones, ones, ones, zeros, kh, km, kl] + [zeros] * 9, axis=0).astype(_BF16)
        cum = jnp.concatenate([cum_t] + [zeros] * (LANES // HEADS - 1), axis=0).T
        qh, qm, ql = _split3(cum)
        qaug_ref[0, r0:r0 + ta, :] = (qh + pltpu.roll(qm, HEADS, 1)
                                      + pltpu.roll(ql, 2 * HEADS, 1)).astype(_BF16)

        tile = i * nt + t
        stats_ref[0, ST_QN2, pl.ds(tile, 1), :] = jnp.max(qn2[r0:r0 + ta], axis=0, keepdims=True)
        stats_ref[0, ST_FSTART, pl.ds(tile, 1), :] = cum[0:1, :]
        for c in range(KB_PER_TILE):
            last = (c + 1) * ATTN_KBLOCK - 1
            stats_ref[0, ST_FEND + c, pl.ds(tile, 1), :] = cum[last:last + 1, :]
    carry_ref[...] = carry
    stats_ref[0, ST_KMAX] = jnp.broadcast_to(kmax, stats_ref.shape[2:])


def _in_proj(x, g_norm, scale, shift, wt_a, wt_b, b_f, tm=1024):
    bsz, s, d = x.shape
    ni = s // tm
    tok = lambda width, dt: jax.ShapeDtypeStruct((bsz, s, width), dt)
    tok_spec = lambda width: pl.BlockSpec((1, tm, width), lambda b, i: (b, i, 0))
    tr = lambda rows: jax.ShapeDtypeStruct((bsz, rows, s), _BF16)
    tr_spec = lambda rows: pl.BlockSpec((1, rows, tm), lambda b, i: (b, 0, i))
    const = lambda shape: pl.BlockSpec(shape, lambda b, i: (0,) * len(shape), pipeline_mode=pl.Buffered(1))
    per_b = pl.BlockSpec((1, 1, d), lambda b, i: (b, 0, 0))
    return pl.pallas_call(
        functools.partial(_in_proj_kernel, tm=tm),
        grid=(bsz, ni),
        in_specs=[tok_spec(d), const((1, d)), per_b, per_b,
                  const(wt_a.shape), const(wt_b.shape), const((HEADS, 1))],
        out_specs=[tok_spec(FOX_WIDTH), tr_spec(FOX_WIDTH), tok_spec(FOX_WIDTH),
                   tok_spec(LANES), tr_spec(LANES),
                   tok_spec(FOX_WIDTH), tok_spec(S5_WIDTH), tok_spec(S5_WIDTH),
                   tok_spec(d), tok_spec(d),
                   pl.BlockSpec((1, ST_PLANES, s // ATTN_TILE, LANES), lambda b, i: (b, 0, 0, 0))],
        out_shape=[tok(FOX_WIDTH, _BF16), tr(FOX_WIDTH), tok(FOX_WIDTH, _BF16),
                   tok(LANES, _BF16), tr(LANES),
                   tok(FOX_WIDTH, _BF16), tok(S5_WIDTH, _BF16), tok(S5_WIDTH, _BF16),
                   tok(d, _BF16), tok(d, _BF16),
                   jax.ShapeDtypeStruct((bsz, ST_PLANES, s // ATTN_TILE, LANES), _F32)],
        scratch_shapes=[pltpu.VMEM((HEADS, LANES), _F32), pltpu.VMEM((1, LANES), _F32)],
        compiler_params=pltpu.CompilerParams(
            dimension_semantics=("arbitrary", "arbitrary"), vmem_limit_bytes=VMEM_LIMIT),
        name="in_proj",
    )(x, g_norm.reshape(1, d), scale, shift, wt_a, wt_b, b_f.reshape(HEADS, 1))


def _plan_kernel(st_ref, o_ref):
    qn2 = st_ref[0, ST_QN2]
    fstart = st_ref[0, ST_FSTART]
    kmax2 = st_ref[0, ST_KMAX]
    o_ref[0] = fstart + (NORM_SAFETY * jnp.sqrt(qn2 * kmax2) + SKIP_LOG2)


def _skip_plan(stats):
    bsz, _, nq, _ = stats.shape
    return pl.pallas_call(
        _plan_kernel,
        grid=(bsz,),
        in_specs=[pl.BlockSpec((1, ST_PLANES, nq, LANES), lambda b: (b, 0, 0, 0))],
        out_specs=pl.BlockSpec((1, nq, LANES), lambda b: (b, 0, 0)),
        out_shape=jax.ShapeDtypeStruct((bsz, nq, LANES), _F32),
        name="skip_plan",
    )(stats)


def _attn_kernel(base_ref, fend_ref, q_ref, kt_ref, kaugt_ref, v_ref, qaug_ref, za_ref, o_ref,
                 lhs_ref, s_ref, p_ref, m_ref, alpha_ref, acc_ref, *, tq, tk, rb, nq):
    b = pl.program_id(0)
    hp = pl.program_id(1)
    i = pl.program_id(2)
    lane = lax.broadcasted_iota(jnp.int32, (1, LANES), 1)
    q = q_ref[0]
    qa = qaug_ref[0]

    head_lanes = [(lane >= hh * HEAD_DIM) & (lane < (hh + 1) * HEAD_DIM) for hh in range(2)]
    for hh in range(2):
        h = 2 * hp + hh
        f_lanes = (lane == h) | (lane == HEADS + h) | (lane == 2 * HEADS + h)
        neg_lanes = (lane == 4 * HEADS + h) | (lane == 5 * HEADS + h) | (lane == 6 * HEADS + h)
        qm = jnp.where(head_lanes[hh], q, jnp.zeros_like(q))
        qaug = jnp.where(f_lanes, qa, jnp.where(neg_lanes, -1.0, 0.0).astype(_BF16))
        lhs_ref[hh] = jnp.concatenate([qm, qaug], axis=1)
        m_ref[hh] = jnp.full((tq, LANES), NEG_BIG, _F32)
        acc_ref[hh] = jnp.zeros((tq, LANES), _F32)

    def step(j, masked):
        k0 = pl.multiple_of(j * tk, tk)
        w = jnp.concatenate([kt_ref[0, :, pl.ds(k0, tk)], kaugt_ref[0, :, pl.ds(k0, tk)]], axis=0)
        vb = v_ref[0, pl.ds(k0, tk), :]
        for hh in range(2):
            s = _dot(lhs_ref[hh], w)
            if masked:
                r = lax.broadcasted_iota(jnp.int32, (tq, tk), 0)
                c = lax.broadcasted_iota(jnp.int32, (tq, tk), 1)
                s = jnp.where(c - r <= i * tq - j * tk, s, NEG_BIG)
            s_ref[hh] = s
        for hh in range(2):
            for r0 in range(0, tq, rb):
                chunks = [s_ref[hh, r0:r0 + rb, c0:c0 + LANES] for c0 in range(0, tk, LANES)]
                m_old = m_ref[hh, r0:r0 + rb, :]
                m_new = jnp.maximum(m_old, jnp.max(functools.reduce(jnp.maximum, chunks), axis=1, keepdims=True))
                alpha_ref[hh, r0:r0 + rb, :] = jnp.exp2(m_old - m_new)
                m_ref[hh, r0:r0 + rb, :] = m_new
                for c0, ch in zip(range(0, tk, LANES), chunks):
                    p_ref[hh, r0:r0 + rb, c0:c0 + LANES] = jnp.exp2(ch - m_new).astype(_BF16)
        for hh in range(2):
            sum_lane = (1 - hh) * HEAD_DIM
            vaug = jnp.where(head_lanes[hh], vb, jnp.where(lane == sum_lane, 1.0, 0.0).astype(_BF16))
            acc_ref[hh] = alpha_ref[hh] * acc_ref[hh] + _dot(p_ref[hh], vaug)

    kpt = tq // tk
    j_diag = i * kpt

    for c in range(kpt):
        step(j_diag + c, True)

    def first_needed(hh):
        thr = base_ref[(b * HEADS + 2 * hp + hh) * nq + i] - jnp.min(m_ref[hh])
        row = (b * HEADS + 2 * hp + hh) * (nq * kpt)
        count = jnp.int32(0)
        for jj in range(nq * kpt):
            count += ((fend_ref[row + jj] >= thr) & (jj < j_diag)).astype(jnp.int32)
        return count

    def step_one(hd, js):
        lanes_h = (lane >= hd * HEAD_DIM) & (lane < (hd + 1) * HEAD_DIM)
        sum_lane = (1 - hd) * HEAD_DIM
        lhs = lhs_ref[hd]
        vbs = []
        for n, j in enumerate(js):
            k0 = pl.multiple_of(j * tk, tk)
            w = jnp.concatenate([kt_ref[0, :, pl.ds(k0, tk)], kaugt_ref[0, :, pl.ds(k0, tk)]], axis=0)
            vbs.append(v_ref[0, pl.ds(k0, tk), :])
            s_ref[n] = _dot(lhs, w)
        for n in range(len(js)):
            for r0 in range(0, tq, rb):
                chunks = [s_ref[n, r0:r0 + rb, c0:c0 + LANES] for c0 in range(0, tk, LANES)]
                m_old = m_ref[hd, r0:r0 + rb, :]
                m_new = jnp.maximum(m_old, jnp.max(functools.reduce(jnp.maximum, chunks), axis=1, keepdims=True))
                alpha_ref[hd, r0:r0 + rb, :] = jnp.exp2(m_old - m_new)
                m_ref[hd, r0:r0 + rb, :] = m_new
                for c0, ch in zip(range(0, tk, LANES), chunks):
                    p_ref[n, r0:r0 + rb, c0:c0 + LANES] = jnp.exp2(ch - m_new).astype(_BF16)
            vaug = jnp.where(lanes_h, vbs[n], jnp.where(lane == sum_lane, 1.0, 0.0).astype(_BF16))
            acc_ref[hd] = alpha_ref[hd] * acc_ref[hd] + _dot(p_ref[n], vaug)

    jf0 = first_needed(0)
    jf1 = first_needed(1)
    j_lo = jnp.minimum(jf0, jf1)
    j_first = jnp.maximum(jf0, jf1)
    slow = (jf1 < jf0).astype(jnp.int32)
    n_one = j_first - j_lo

    @pl.when((n_one & 1) == 1)
    def _():
        step_one(slow, [j_lo])

    j_lo2 = j_lo + (n_one & 1)

    def pair_one(t, carry):
        step_one(slow, [j_lo2 + 2 * t, j_lo2 + 2 * t + 1])
        return carry

    lax.fori_loop(0, n_one >> 1, pair_one, 0)

    n_old = j_diag - j_first

    @pl.when((n_old & 1) == 1)
    def _():
        step(j_first, False)

    j_2 = j_first + (n_old & 1)

    def pair_step(t, carry):
        step(j_2 + 2 * t, False)
        step(j_2 + 2 * t + 1, False)
        return carry

    lax.fori_loop(0, n_old >> 1, pair_step, 0)

    a0 = acc_ref[0]
    a1 = acc_ref[1]
    l0 = a0[:, HEAD_DIM:HEAD_DIM + 1]
    l1 = a1[:, 0:1]
    y = jnp.where(lane < HEAD_DIM, a0 / l0, a1 / l1)
    o_ref[0] = (y * za_ref[0].astype(_F32)).astype(_BF16)


def _fox_attention(base, fend, q, kt, kaugt, v, qaug, za, rb=32):
    bsz, s, _ = q.shape
    tq, tk = ATTN_TILE, ATTN_KBLOCK
    nq = s // tq
    qspec = pl.BlockSpec((1, tq, LANES), lambda b, hp, i, *_: (b, i, hp))
    grid_spec = pltpu.PrefetchScalarGridSpec(
        num_scalar_prefetch=2,
        grid=(bsz, HEAD_PAIRS, nq),
        in_specs=[qspec,
                  pl.BlockSpec((1, LANES, s), lambda b, hp, i, *_: (b, hp, 0)),
                  pl.BlockSpec((1, LANES, s), lambda b, hp, i, *_: (b, 0, 0)),
                  pl.BlockSpec((1, s, LANES), lambda b, hp, i, *_: (b, 0, hp)),
                  pl.BlockSpec((1, tq, LANES), lambda b, hp, i, *_: (b, i, 0)),
                  qspec],
        out_specs=qspec,
        scratch_shapes=[pltpu.VMEM((2, tq, 2 * LANES), _BF16),
                        pltpu.VMEM((2, tq, tk), _F32), pltpu.VMEM((2, tq, tk), _BF16),
                        pltpu.VMEM((2, tq, LANES), _F32), pltpu.VMEM((2, tq, LANES), _F32),
                        pltpu.VMEM((2, tq, LANES), _F32)])
    return pl.pallas_call(
        functools.partial(_attn_kernel, tq=tq, tk=tk, rb=rb, nq=nq),
        grid_spec=grid_spec,
        out_shape=jax.ShapeDtypeStruct((bsz, s, FOX_WIDTH), _BF16),
        compiler_params=pltpu.CompilerParams(
            dimension_semantics=("arbitrary", "arbitrary", "arbitrary"), vmem_limit_bytes=VMEM_LIMIT),
        name="fox_attn",
    )(base, fend, q, kt, kaugt, v, qaug, za)


def _gelu_tanh(x):
    return 0.5 * x * (1.0 + jnp.tanh(math.sqrt(2.0 / math.pi) * (x + 0.044715 * (x * x * x))))


def _s5_kernel(u_ref, zb_ref, ar_ref, ai_ref, bblk_ref, cblk_ref, dskip_ref, wglu_ref, bglu_ref,
               o_ref, xr_ref, xi_ref, sr_ref, si_ref, usb_ref, ysb_ref, *, tt, nbatch, slabs_per_pass):
    t = pl.program_id(0)
    nchain = nbatch * S5_HALVES
    w = u_ref.shape[-1]
    wslabs = w // LANES
    nrow = nbatch * tt

    @pl.when(t == 0)
    def _():
        sr_ref[...] = jnp.zeros_like(sr_ref)
        si_ref[...] = jnp.zeros_like(si_ref)

    for b in range(nbatch):
        ub = u_ref[b].astype(_F32)
        for jl in range(wslabs):
            usb_ref[jl, pl.ds(b, tt, stride=nbatch), :] = ub[:, jl * LANES:(jl + 1) * LANES]
    u_sb = jnp.concatenate([usb_ref[jl] for jl in range(wslabs)], axis=1)

    for hf in range(S5_HALVES):
        ub = u_sb[:, hf * S5_HALF_CH:(hf + 1) * S5_HALF_CH].astype(_BF16)
        bu = _dot(ub, bblk_ref[hf])
        for j in range(S5_SLABS):
            xr_ref[j, pl.ds(hf, nrow, stride=S5_HALVES), :] = bu[:, j * LANES:(j + 1) * LANES]
            xi_ref[j, pl.ds(hf, nrow, stride=S5_HALVES), :] = bu[:, (S5_SLABS + j) * LANES:
                                                                 (S5_SLABS + j + 1) * LANES]

    for j0 in range(0, S5_SLABS, slabs_per_pass):
        js = list(range(j0, j0 + slabs_per_pass))
        ars = [ar_ref[j] for j in js]
        ais = [ai_ref[j] for j in js]

        def body(s, carry, js=js, ars=ars, ais=ais):
            r0 = pl.multiple_of(s * nchain, nchain)
            new = []
            for n, j in enumerate(js):
                sr, si = carry[2 * n], carry[2 * n + 1]
                br = xr_ref[j, pl.ds(r0, nchain), :]
                bi = xi_ref[j, pl.ds(r0, nchain), :]
                nr = (ars[n] * sr - ais[n] * si) + br
                ni = (ars[n] * si + ais[n] * sr) + bi
                xr_ref[j, pl.ds(r0, nchain), :] = nr
                xi_ref[j, pl.ds(r0, nchain), :] = ni
                new += [nr, ni]
            return tuple(new)

        init = []
        for j in js:
            init += [sr_ref[j], si_ref[j]]
        fin = lax.fori_loop(0, tt, body, tuple(init), unroll=4)
        for n, j in enumerate(js):
            sr_ref[j] = fin[2 * n]
            si_ref[j] = fin[2 * n + 1]

    halves = []
    for hf in range(S5_HALVES):
        xs = [xr_ref[j, pl.ds(hf, nrow, stride=S5_HALVES), :] for j in range(S5_SLABS)]
        xs += [xi_ref[j, pl.ds(hf, nrow, stride=S5_HALVES), :] for j in range(S5_SLABS)]
        halves.append(_dot(jnp.concatenate(xs, axis=1).astype(_BF16), cblk_ref[hf]))
    y = jnp.concatenate(halves, axis=1)
    y = y + dskip_ref[...] * u_sb
    y = _gelu_tanh(y)
    y = y * _sigmoid(_dot(y.astype(_BF16), wglu_ref[...]) + bglu_ref[...])
    for jl in range(wslabs):
        ysb_ref[jl] = y[:, jl * LANES:(jl + 1) * LANES]
    for b in range(nbatch):
        yb = jnp.concatenate([ysb_ref[jl, pl.ds(b, tt, stride=nbatch), :] for jl in range(wslabs)], axis=1)
        o_ref[b] = (yb * zb_ref[b].astype(_F32)).astype(_BF16)


def _s5_scan(u, zb, ar_t, ai_t, bblk, cblk, d_skip, w_glu, b_glu, tt=256, slabs_per_pass=8):
    bsz, s, w = u.shape
    nchain = bsz * S5_HALVES
    assert nchain == SUBLANES, "the scan packs batch x state-half chains into the 8 sublanes"
    tok_spec = pl.BlockSpec((bsz, tt, w), lambda t: (0, t, 0))
    const = lambda shape: pl.BlockSpec(shape, lambda t: (0,) * len(shape), pipeline_mode=pl.Buffered(1))
    return pl.pallas_call(
        functools.partial(_s5_kernel, tt=tt, nbatch=bsz, slabs_per_pass=slabs_per_pass),
        grid=(s // tt,),
        in_specs=[tok_spec, tok_spec, const(ar_t.shape), const(ai_t.shape),
                  const(bblk.shape), const(cblk.shape), const((1, w)), const(w_glu.shape), const((1, w))],
        out_specs=tok_spec,
        out_shape=jax.ShapeDtypeStruct((bsz, s, w), _BF16),
        scratch_shapes=[pltpu.VMEM((S5_SLABS, tt * nchain, LANES), _F32),
                        pltpu.VMEM((S5_SLABS, tt * nchain, LANES), _F32),
                        pltpu.VMEM((S5_SLABS, nchain, LANES), _F32),
                        pltpu.VMEM((S5_SLABS, nchain, LANES), _F32),
                        pltpu.VMEM((w // LANES, bsz * tt, LANES), _F32),
                        pltpu.VMEM((w // LANES, bsz * tt, LANES), _F32)],
        compiler_params=pltpu.CompilerParams(
            dimension_semantics=("arbitrary",), vmem_limit_bytes=VMEM_LIMIT),
        name="s5_scan",
    )(u, zb, ar_t, ai_t, bblk, cblk, d_skip.reshape(1, w), w_glu, b_glu.reshape(1, w))


def _out_kernel(x_ref, ya_ref, yb_ref, ga_ref, gb_ref, gate_ref, wua_ref, wub_ref, wo_ref, gf_ref, o_ref):
    ma = ga_ref[0].astype(_F32) * _dot(ya_ref[0], wua_ref[...])
    mb = gb_ref[0].astype(_F32) * _dot(yb_ref[0], wub_ref[...])
    merged = (ma + mb).astype(_BF16)
    xn = x_ref[0] + gate_ref[0] * _dot(merged, wo_ref[...])
    ms = jnp.mean(xn * xn, axis=-1, keepdims=True)
    o_ref[0] = xn * lax.rsqrt(ms + EPS) * gf_ref[...]


def _out_proj(x, ya, yb, ga, gb, gate, w_up_a, w_up_b, w_out, g_final, tm=1024):
    bsz, s, d = x.shape
    tok_spec = lambda width: pl.BlockSpec((1, tm, width), lambda b, i: (b, i, 0))
    const = lambda shape: pl.BlockSpec(shape, lambda b, i: (0,) * len(shape), pipeline_mode=pl.Buffered(1))
    return pl.pallas_call(
        _out_kernel,
        grid=(bsz, s // tm),
        in_specs=[tok_spec(d), tok_spec(FOX_WIDTH), tok_spec(S5_WIDTH), tok_spec(d), tok_spec(d),
                  pl.BlockSpec((1, 1, d), lambda b, i: (b, 0, 0)),
                  const(w_up_a.shape), const(w_up_b.shape), const(w_out.shape), const((1, d))],
        out_specs=tok_spec(d),
        out_shape=jax.ShapeDtypeStruct((bsz, s, d), x.dtype),
        compiler_params=pltpu.CompilerParams(
            dimension_semantics=("arbitrary", "arbitrary"), vmem_limit_bytes=VMEM_LIMIT),
        name="out_proj",
    )(x, ya, yb, ga, gb, gate, w_up_a, w_up_b, w_out, g_final.reshape(1, d))


def _chain_tiles(a, nbatch):
    a = a.reshape(S5_HALVES, S5_SLABS, LANES)
    a = jnp.swapaxes(a, 0, 1)
    return jnp.tile(a, (1, nbatch, 1))


def kernel(x, c, w_ada, b_ada, g_norm, w_in, b_f, a_re, a_im, log_dt, b_re, b_im, c_re, c_im,
           d_skip, w_glu, b_glu, w_up_a, w_up_b, w_out, g_final):
    bsz, s, d = x.shape
    depth = w_ada.shape[0]
    assert depth == 1 and d == D_MODEL

    mod = _adaln_mod(c, w_ada[0], b_ada[0])
    shift = mod[:, 0:d].reshape(bsz, 1, d)
    scale = mod[:, d:2 * d].reshape(bsz, 1, d)
    gate = mod[:, 2 * d:3 * d].reshape(bsz, 1, d)

    wt = jnp.swapaxes(w_in[0], 0, 1)
    o_z = 3 * FOX_WIDTH + HEADS
    wt_a = jnp.pad(wt[0:o_z], ((0, HEADS), (0, 0))).astype(_BF16)
    wt_b = wt[o_z:].astype(_BF16)

    q, kt, v, qaug, kaugt, za, u, zb, ga, gb, stats = _in_proj(
        x, g_norm[0], scale, shift, wt_a, wt_b, b_f[0])

    base = _skip_plan(stats)
    base_tab = jnp.swapaxes(base[:, :, 0:HEADS], 1, 2).reshape(-1)
    fend = stats[:, ST_FEND:ST_FEND + KB_PER_TILE, :, 0:HEADS]
    fend_tab = jnp.transpose(fend, (0, 3, 2, 1)).reshape(-1)
    ya = _fox_attention(base_tab, fend_tab, q, kt, kaugt, v, qaug, za)

    abr, abi, bblk, cblk = _s5_params(a_re[0], a_im[0], log_dt[0],
                                      jnp.swapaxes(b_re[0], 1, 2), jnp.swapaxes(b_im[0], 1, 2),
                                      jnp.swapaxes(c_re[0], 1, 2), jnp.swapaxes(c_im[0], 1, 2))
    yb = _s5_scan(u, zb, _chain_tiles(abr, bsz), _chain_tiles(abi, bsz), bblk, cblk,
                  d_skip[0], w_glu[0].astype(_BF16), b_glu[0])

    return _out_proj(x, ya, yb, ga, gb, gate, w_up_a[0].astype(_BF16), w_up_b[0].astype(_BF16),
                     w_out[0].astype(_BF16), g_final)
```

```python
import functools
import math

import jax
import jax.numpy as jnp
from jax import lax
from jax.experimental import pallas as pl
from jax.experimental.pallas import tpu as pltpu

D_MODEL = 1024
EPS = 1e-6
FOX_WIDTH = 512
HEAD_DIM = 64
HEADS = 8
HEAD_PAIRS = HEADS // 2
S5_WIDTH = 512
S5_GROUP = 16
S5_GROUPS = 32
S5_STATE = 64

LANES = 128
SUBLANES = 8
VMEM_LIMIT = 56 * 1024 * 1024

S5_HALVES = 2
S5_HALF_CH = S5_WIDTH // S5_HALVES
S5_HALF_STATE = S5_GROUPS * S5_STATE // S5_HALVES
S5_SLABS = S5_HALF_STATE // LANES

NEG_BIG = -1e30
LOG2E = math.log2(math.e)

ATTN_TILE = 512
ATTN_KBLOCK = 512
KB_PER_TILE = ATTN_TILE // ATTN_KBLOCK
ST_QN2, ST_FSTART, ST_KMAX, ST_FEND = 0, 1, 2, 3
ST_PLANES = ST_FEND + KB_PER_TILE
SKIP_LOG2 = 40.0
NORM_SAFETY = 1.02

_F32 = jnp.float32
_BF16 = jnp.bfloat16


def _dot(a, b):
    return jnp.dot(a, b, preferred_element_type=_F32)


def _dot_nt(a, b):
    return lax.dot_general(a, b, (((1,), (1,)), ((), ())), preferred_element_type=_F32)


def _split3(x):
    hi = x.astype(_BF16).astype(_F32)
    r1 = x - hi
    mid = r1.astype(_BF16).astype(_F32)
    lo = (r1 - mid).astype(_BF16).astype(_F32)
    return hi, mid, lo


def _sigmoid(x):
    return 1.0 / (1.0 + jnp.exp(-x))


def _silu(x):
    return x * _sigmoid(x)


def _adaln_kernel(ct_ref, w_ref, b_ref, o_ref):
    d, bn = w_ref.shape
    w3 = w_ref[...].reshape(d // SUBLANES, SUBLANES, bn)
    rows = []
    for b in range(ct_ref.shape[1]):
        cb = ct_ref[:, b:b + 1].reshape(d // SUBLANES, SUBLANES, 1)
        part = jnp.sum(w3 * cb, axis=0)
        rows.append(jnp.sum(part, axis=0, keepdims=True))
    o_ref[...] = jnp.concatenate(rows, axis=0) + b_ref[...]


def _adaln_mod(c, w, b):
    bsz, d = c.shape
    n = w.shape[1]
    bn = 512
    return pl.pallas_call(
        _adaln_kernel,
        grid=(n // bn,),
        in_specs=[pl.BlockSpec((d, bsz), lambda j: (0, 0)),
                  pl.BlockSpec((d, bn), lambda j: (0, j)),
                  pl.BlockSpec((1, bn), lambda j: (0, j))],
        out_specs=pl.BlockSpec((bsz, bn), lambda j: (0, j)),
        out_shape=jax.ShapeDtypeStruct((bsz, n), _F32),
        name="adaln_mod",
    )(c.T, w, b.reshape(1, n))


def _block_diag(x, nblk, reps_per_tile):
    rows, c = x.shape
    r = rows // nblk
    tile = jnp.concatenate([x] * reps_per_tile, axis=1)
    wide = jnp.concatenate([tile] * (nblk // reps_per_tile), axis=1)
    rb = lax.broadcasted_iota(jnp.int32, wide.shape, 0) // r
    cb = lax.broadcasted_iota(jnp.int32, wide.shape, 1) // c
    return jnp.where(rb == cb, wide, 0.0)


def _s5_param_kernel(are_ref, aim_ref, ldt_ref, bre_ref, bim_ref, cre_ref, cim_ref,
                     abr_ref, abi_ref, bblk_ref, cblk_ref):
    lr = are_ref[...]
    li = aim_ref[...]
    dt = jnp.exp(ldt_ref[...])
    mag = jnp.exp(lr * dt)
    ang = li * dt
    abr = mag * jnp.cos(ang)
    abi = mag * jnp.sin(ang)
    abr_ref[...] = abr
    abi_ref[...] = abi
    nr = abr - 1.0
    ni = abi
    den = lr * lr + li * li
    cr = (nr * lr + ni * li) / den
    ci = (ni * lr - nr * li) / den
    cr3 = cr[:, None, :]
    ci3 = ci[:, None, :]
    br = bre_ref[...]
    bi = bim_ref[...]
    bbr = cr3 * br - ci3 * bi
    bbi = cr3 * bi + ci3 * br

    gh = S5_GROUPS // S5_HALVES
    for hf in range(S5_HALVES):
        gsl = slice(hf * gh, (hf + 1) * gh)
        b_in = [_block_diag(t[gsl].reshape(gh * S5_GROUP, S5_STATE), gh, LANES // S5_STATE) for t in (bbr, bbi)]
        bblk_ref[hf] = jnp.concatenate(b_in, axis=1).astype(_BF16)
        c_out = [_block_diag(t[gsl].reshape(gh * S5_STATE, S5_GROUP), gh, LANES // S5_GROUP)
                 for t in (cre_ref[...], -cim_ref[...])]
        cblk_ref[hf] = jnp.concatenate(c_out, axis=0).astype(_BF16)


def _s5_params(a_re, a_im, log_dt, b_re_t, b_im_t, c_re_t, c_im_t):
    g, p = a_re.shape
    cg = b_re_t.shape[1]
    gh = g // S5_HALVES
    return pl.pallas_call(
        _s5_param_kernel,
        out_shape=(jax.ShapeDtypeStruct((g, p), _F32), jax.ShapeDtypeStruct((g, p), _F32),
                   jax.ShapeDtypeStruct((S5_HALVES, gh * cg, 2 * gh * p), _BF16),
                   jax.ShapeDtypeStruct((S5_HALVES, 2 * gh * p, gh * cg), _BF16)),
        name="s5_params",
    )(a_re, a_im, log_dt.reshape(g, 1), b_re_t, b_im_t, c_re_t, c_im_t)


def _in_proj_kernel(x_ref, gn_ref, sc_ref, sh_ref, wa_ref, wt_ref, bf_ref,
                    q_ref, kt_ref, v_ref, qaug_ref, kaugt_ref, za_ref, u_ref, zb_ref, ga_ref, gb_ref,
                    stats_ref, carry_ref, kmax_ref, *, tm):
    i = pl.program_id(1)
    fw = FOX_WIDTH
    c_v, c_f = 2 * fw, 3 * fw
    c_za = 0
    c_u, c_zb = c_za + fw, c_za + fw + S5_WIDTH
    c_ga = c_zb + S5_WIDTH
    c_gb = c_ga + D_MODEL

    @pl.when(i == 0)
    def _():
        carry_ref[...] = jnp.zeros_like(carry_ref)
        kmax_ref[...] = jnp.zeros_like(kmax_ref)

    x = x_ref[0]
    ms = jnp.mean(x * x, axis=-1, keepdims=True)
    h = x * lax.rsqrt(ms + EPS) * gn_ref[...]
    h = (h * (1.0 + sc_ref[0]) + sh_ref[0]).astype(_BF16)

    w = FOX_WIDTH
    qb = (_dot_nt(h, wa_ref[0:fw, :]) * (HEAD_DIM ** -0.5 * LOG2E)).astype(_BF16)
    q_ref[0] = qb
    ktb = _dot_nt(wa_ref[fw:c_v, :], h).astype(_BF16)
    kt_ref[0] = ktb
    v_ref[0] = _dot_nt(h, wa_ref[c_v:c_f, :]).astype(_BF16)

    seg = (lax.broadcasted_iota(jnp.int32, (w, LANES), 0) // HEAD_DIM
           == lax.broadcasted_iota(jnp.int32, (w, LANES), 1)).astype(_BF16)
    seg_t = (lax.broadcasted_iota(jnp.int32, (LANES, w), 1) // HEAD_DIM
             == lax.broadcasted_iota(jnp.int32, (LANES, w), 0)).astype(_BF16)
    qf = qb.astype(_F32)
    kf = ktb.astype(_F32)
    qn2 = _dot((qf * qf).astype(_BF16), seg)
    kn2 = _dot(seg_t, (kf * kf).astype(_BF16)).T
    kmax = jnp.maximum(kmax_ref[...], jnp.max(kn2, axis=0, keepdims=True))
    kmax_ref[...] = kmax
    za_ref[0] = _silu(_dot_nt(h, wt_ref[c_za:c_u, :])).astype(_BF16)
    u_ref[0] = _dot_nt(h, wt_ref[c_u:c_zb, :]).astype(_BF16)
    zb_ref[0] = _silu(_dot_nt(h, wt_ref[c_zb:c_ga, :])).astype(_BF16)
    ga_ref[0] = _sigmoid(_dot_nt(h, wt_ref[c_ga:c_gb, :])).astype(_BF16)
    gb_ref[0] = _sigmoid(_dot_nt(h, wt_ref[c_gb:c_gb + D_MODEL, :])).astype(_BF16)

    flt = _dot_nt(wa_ref[c_f:c_f + 2 * HEADS, :], h)[0:HEADS] + bf_ref[...]
    logft = -(jnp.maximum(-flt, 0.0) + jnp.log1p(jnp.exp(-jnp.abs(flt)))) * LOG2E
    lmh_t = jnp.concatenate(_split3(logft), axis=0).astype(_BF16)

    ta = ATTN_TILE
    row = lax.broadcasted_iota(jnp.int32, (ta, ta), 0)
    col = lax.broadcasted_iota(jnp.int32, (ta, ta), 1)
    triu = (row <= col).astype(_BF16)
    ones = jnp.ones((HEADS, ta), _F32)
    zeros = jnp.zeros((HEADS, ta), _F32)
    nt = tm // ta
    carry = carry_ref[...]
    for t in range(nt):
        r0 = t * ta
        parts = _dot(lmh_t[:, r0:r0 + ta], triu)
        cum_t = (parts[0:HEADS] + parts[HEADS:2 * HEADS]) + parts[2 * HEADS:3 * HEADS] + carry[:, 0:1]
        carry = jnp.broadcast_to(cum_t[:, ta - 1:ta], (HEADS, LANES))

        kh, km, kl = _split3(cum_t)
        kaugt_ref[0, :, r0:r0 + ta] = jnp.concatenate(
            [ones, ones, ones, zeros, kh, km, kl] + [zeros] * 9, axis=0).astype(_BF16)
        cum = jnp.concatenate([cum_t] + [zeros] * (LANES // HEADS - 1), axis=0).T
        qh, qm, ql = _split3(cum)
        qaug_ref[0, r0:r0 + ta, :] = (qh + pltpu.roll(qm, HEADS, 1)
                                      + pltpu.roll(ql, 2 * HEADS, 1)).astype(_BF16)

        tile = i * nt + t
        stats_ref[0, ST_QN2, pl.ds(tile, 1), :] = jnp.max(qn2[r0:r0 + ta], axis=0, keepdims=True)
        stats_ref[0, ST_FSTART, pl.ds(tile, 1), :] = cum[0:1, :]
        for c in range(KB_PER_TILE):
            last = (c + 1) * ATTN_KBLOCK - 1
            stats_ref[0, ST_FEND + c, pl.ds(tile, 1), :] = cum[last:last + 1, :]
    carry_ref[...] = carry
    stats_ref[0, ST_KMAX] = jnp.broadcast_to(kmax, stats_ref.shape[2:])


def _in_proj(x, g_norm, scale, shift, wt_a, wt_b, b_f, tm=1024):
    bsz, s, d = x.shape
    ni = s // tm
    tok = lambda width, dt: jax.ShapeDtypeStruct((bsz, s, width), dt)
    tok_spec = lambda width: pl.BlockSpec((1, tm, width), lambda b, i: (b, i, 0))
    tr = lambda rows: jax.ShapeDtypeStruct((bsz, rows, s), _BF16)
    tr_spec = lambda rows: pl.BlockSpec((1, rows, tm), lambda b, i: (b, 0, i))
    const = lambda shape: pl.BlockSpec(shape, lambda b, i: (0,) * len(shape), pipeline_mode=pl.Buffered(1))
    per_b = pl.BlockSpec((1, 1, d), lambda b, i: (b, 0, 0))
    return pl.pallas_call(
        functools.partial(_in_proj_kernel, tm=tm),
        grid=(bsz, ni),
        in_specs=[tok_spec(d), const((1, d)), per_b, per_b,
                  const(wt_a.shape), const(wt_b.shape), const((HEADS, 1))],
        out_specs=[tok_spec(FOX_WIDTH), tr_spec(FOX_WIDTH), tok_spec(FOX_WIDTH),
                   tok_spec(LANES), tr_spec(LANES),
                   tok_spec(FOX_WIDTH), tok_spec(S5_WIDTH), tok_spec(S5_WIDTH),
                   tok_spec(d), tok_spec(d),
                   pl.BlockSpec((1, ST_PLANES, s // ATTN_TILE, LANES), lambda b, i: (b, 0, 0, 0))],
        out_shape=[tok(FOX_WIDTH, _BF16), tr(FOX_WIDTH), tok(FOX_WIDTH, _BF16),
                   tok(LANES, _BF16), tr(LANES),
                   tok(FOX_WIDTH, _BF16), tok(S5_WIDTH, _BF16), tok(S5_WIDTH, _BF16),
                   tok(d, _BF16), tok(d, _BF16),
                   jax.ShapeDtypeStruct((bsz, ST_PLANES, s // ATTN_TILE, LANES), _F32)],
        scratch_shapes=[pltpu.VMEM((HEADS, LANES), _F32), pltpu.VMEM((1, LANES), _F32)],
        compiler_params=pltpu.CompilerParams(
            dimension_semantics=("arbitrary", "arbitrary"), vmem_limit_bytes=VMEM_LIMIT),
        name="in_proj",
    )(x, g_norm.reshape(1, d), scale, shift, wt_a, wt_b, b_f.reshape(HEADS, 1))


def _plan_kernel(st_ref, o_ref):
    qn2 = st_ref[0, ST_QN2]
    fstart = st_ref[0, ST_FSTART]
    kmax2 = st_ref[0, ST_KMAX]
    o_ref[0] = fstart + (NORM_SAFETY * jnp.sqrt(qn2 * kmax2) + SKIP_LOG2)


def _skip_plan(stats):
    bsz, _, nq, _ = stats.shape
    return pl.pallas_call(
        _plan_kernel,
        grid=(bsz,),
        in_specs=[pl.BlockSpec((1, ST_PLANES, nq, LANES), lambda b: (b, 0, 0, 0))],
        out_specs=pl.BlockSpec((1, nq, LANES), lambda b: (b, 0, 0)),
        out_shape=jax.ShapeDtypeStruct((bsz, nq, LANES), _F32),
        name="skip_plan",
    )(stats)


def _attn_kernel(base_ref, fend_ref, q_ref, kt_ref, kaugt_ref, v_ref, qaug_ref, za_ref, o_ref,
                 lhs_ref, s_ref, p_ref, m_ref, alpha_ref, acc_ref, *, tq, tk, rb, nq):
    b = pl.program_id(0)
    hp = pl.program_id(1)
    i = pl.program_id(2)
    lane = lax.broadcasted_iota(jnp.int32, (1, LANES), 1)
    q = q_ref[0]
    qa = qaug_ref[0]

    head_lanes = [(lane >= hh * HEAD_DIM) & (lane < (hh + 1) * HEAD_DIM) for hh in range(2)]
    for hh in range(2):
        h = 2 * hp + hh
        f_lanes = (lane == h) | (lane == HEADS + h) | (lane == 2 * HEADS + h)
        neg_lanes = (lane == 4 * HEADS + h) | (lane == 5 * HEADS + h) | (lane == 6 * HEADS + h)
        qm = jnp.where(head_lanes[hh], q, jnp.zeros_like(q))
        qaug = jnp.where(f_lanes, qa, jnp.where(neg_lanes, -1.0, 0.0).astype(_BF16))
        lhs_ref[hh] = jnp.concatenate([qm, qaug], axis=1)
        m_ref[hh] = jnp.full((tq, LANES), NEG_BIG, _F32)
        acc_ref[hh] = jnp.zeros((tq, LANES), _F32)

    def step(j, masked):
        k0 = pl.multiple_of(j * tk, tk)
        w = jnp.concatenate([kt_ref[0, :, pl.ds(k0, tk)], kaugt_ref[0, :, pl.ds(k0, tk)]], axis=0)
        vb = v_ref[0, pl.ds(k0, tk), :]
        for hh in range(2):
            s = _dot(lhs_ref[hh], w)
            if masked:
                r = lax.broadcasted_iota(jnp.int32, (tq, tk), 0)
                c = lax.broadcasted_iota(jnp.int32, (tq, tk), 1)
                s = jnp.where(c - r <= i * tq - j * tk, s, NEG_BIG)
            s_ref[hh] = s
        for hh in range(2):
            for r0 in range(0, tq, rb):
                chunks = [s_ref[hh, r0:r0 + rb, c0:c0 + LANES] for c0 in range(0, tk, LANES)]
                m_old = m_ref[hh, r0:r0 + rb, :]
                m_new = jnp.maximum(m_old, jnp.max(functools.reduce(jnp.maximum, chunks), axis=1, keepdims=True))
                alpha_ref[hh, r0:r0 + rb, :] = jnp.exp2(m_old - m_new)
                m_ref[hh, r0:r0 + rb, :] = m_new
                for c0, ch in zip(range(0, tk, LANES), chunks):
                    p_ref[hh, r0:r0 + rb, c0:c0 + LANES] = jnp.exp2(ch - m_new).astype(_BF16)
        for hh in range(2):
            sum_lane = (1 - hh) * HEAD_DIM
            vaug = jnp.where(head_lanes[hh], vb, jnp.where(lane == sum_lane, 1.0, 0.0).astype(_BF16))
            acc_ref[hh] = alpha_ref[hh] * acc_ref[hh] + _dot(p_ref[hh], vaug)

    kpt = tq // tk
    j_diag = i * kpt

    @pl.when(i == 0)
    def _():
        for c in range(kpt):
            step(j_diag + c, True)

    @pl.when(i > 0)
    def _():
        for c in range(kpt):
            step(j_diag + c, True)
        step(j_diag - 1, False)

    j_end = jnp.maximum(j_diag - 1, 0)

    def first_needed(hh):
        thr = base_ref[(b * HEADS + 2 * hp + hh) * nq + i] - jnp.min(m_ref[hh])
        row = (b * HEADS + 2 * hp + hh) * (nq * kpt)
        count = jnp.int32(0)
        for jj in range(nq * kpt):
            count += ((fend_ref[row + jj] >= thr) & (jj < j_end)).astype(jnp.int32)
        return count

    def step_one(hd, js):
        lanes_h = (lane >= hd * HEAD_DIM) & (lane < (hd + 1) * HEAD_DIM)
        sum_lane = (1 - hd) * HEAD_DIM
        lhs = lhs_ref[hd]
        vbs = []
        for n, j in enumerate(js):
            k0 = pl.multiple_of(j * tk, tk)
            w = jnp.concatenate([kt_ref[0, :, pl.ds(k0, tk)], kaugt_ref[0, :, pl.ds(k0, tk)]], axis=0)
            vbs.append(v_ref[0, pl.ds(k0, tk), :])
            s_ref[n] = _dot(lhs, w)
        for n in range(len(js)):
            for r0 in range(0, tq, rb):
                chunks = [s_ref[n, r0:r0 + rb, c0:c0 + LANES] for c0 in range(0, tk, LANES)]
                m_old = m_ref[hd, r0:r0 + rb, :]
                m_new = jnp.maximum(m_old, jnp.max(functools.reduce(jnp.maximum, chunks), axis=1, keepdims=True))
                alpha_ref[hd, r0:r0 + rb, :] = jnp.exp2(m_old - m_new)
                m_ref[hd, r0:r0 + rb, :] = m_new
                for c0, ch in zip(range(0, tk, LANES), chunks):
                    p_ref[n, r0:r0 + rb, c0:c0 + LANES] = jnp.exp2(ch - m_new).astype(_BF16)
            vaug = jnp.where(lanes_h, vbs[n], jnp.where(lane == sum_lane, 1.0, 0.0).astype(_BF16))
            acc_ref[hd] = alpha_ref[hd] * acc_ref[hd] + _dot(p_ref[n], vaug)

    jf0 = first_needed(0)
    jf1 = first_needed(1)
    j_lo = jnp.minimum(jf0, jf1)
    j_first = jnp.maximum(jf0, jf1)
    slow = (jf1 < jf0).astype(jnp.int32)
    n_one = j_first - j_lo

    @pl.when((n_one & 1) == 1)
    def _():
        step_one(slow, [j_lo])

    j_lo2 = j_lo + (n_one & 1)

    def pair_one(t, carry):
        step_one(slow, [j_lo2 + 2 * t, j_lo2 + 2 * t + 1])
        return carry

    lax.fori_loop(0, n_one >> 1, pair_one, 0)

    n_old = j_end - j_first

    @pl.when((n_old & 1) == 1)
    def _():
        step(j_first, False)

    j_2 = j_first + (n_old & 1)

    def pair_step(t, carry):
        step(j_2 + 2 * t, False)
        step(j_2 + 2 * t + 1, False)
        return carry

    lax.fori_loop(0, n_old >> 1, pair_step, 0)

    a0 = acc_ref[0]
    a1 = acc_ref[1]
    l0 = a0[:, HEAD_DIM:HEAD_DIM + 1]
    l1 = a1[:, 0:1]
    y = jnp.where(lane < HEAD_DIM, a0 / l0, a1 / l1)
    o_ref[0] = (y * za_ref[0].astype(_F32)).astype(_BF16)


def _fox_attention(base, fend, q, kt, kaugt, v, qaug, za, rb=32):
    bsz, s, _ = q.shape
    tq, tk = ATTN_TILE, ATTN_KBLOCK
    nq = s // tq
    qspec = pl.BlockSpec((1, tq, LANES), lambda b, hp, i, *_: (b, i, hp))
    grid_spec = pltpu.PrefetchScalarGridSpec(
        num_scalar_prefetch=2,
        grid=(bsz, HEAD_PAIRS, nq),
        in_specs=[qspec,
                  pl.BlockSpec((1, LANES, s), lambda b, hp, i, *_: (b, hp, 0)),
                  pl.BlockSpec((1, LANES, s), lambda b, hp, i, *_: (b, 0, 0)),
                  pl.BlockSpec((1, s, LANES), lambda b, hp, i, *_: (b, 0, hp)),
                  pl.BlockSpec((1, tq, LANES), lambda b, hp, i, *_: (b, i, 0)),
                  qspec],
        out_specs=qspec,
        scratch_shapes=[pltpu.VMEM((2, tq, 2 * LANES), _BF16),
                        pltpu.VMEM((2, tq, tk), _F32), pltpu.VMEM((2, tq, tk), _BF16),
                        pltpu.VMEM((2, tq, LANES), _F32), pltpu.VMEM((2, tq, LANES), _F32),
                        pltpu.VMEM((2, tq, LANES), _F32)])
    return pl.pallas_call(
        functools.partial(_attn_kernel, tq=tq, tk=tk, rb=rb, nq=nq),
        grid_spec=grid_spec,
        out_shape=jax.ShapeDtypeStruct((bsz, s, FOX_WIDTH), _BF16),
        compiler_params=pltpu.CompilerParams(
            dimension_semantics=("arbitrary", "arbitrary", "arbitrary"), vmem_limit_bytes=VMEM_LIMIT),
        name="fox_attn",
    )(base, fend, q, kt, kaugt, v, qaug, za)


def _gelu_tanh(x):
    return 0.5 * x * (1.0 + jnp.tanh(math.sqrt(2.0 / math.pi) * (x + 0.044715 * (x * x * x))))


def _s5_kernel(u_ref, zb_ref, ar_ref, ai_ref, bblk_ref, cblk_ref, dskip_ref, wglu_ref, bglu_ref,
               o_ref, xr_ref, xi_ref, sr_ref, si_ref, usb_ref, ysb_ref, *, tt, nbatch, slabs_per_pass):
    t = pl.program_id(0)
    nchain = nbatch * S5_HALVES
    w = u_ref.shape[-1]
    wslabs = w // LANES
    nrow = nbatch * tt

    @pl.when(t == 0)
    def _():
        sr_ref[...] = jnp.zeros_like(sr_ref)
        si_ref[...] = jnp.zeros_like(si_ref)

    for b in range(nbatch):
        ub = u_ref[b].astype(_F32)
        for jl in range(wslabs):
            usb_ref[jl, pl.ds(b, tt, stride=nbatch), :] = ub[:, jl * LANES:(jl + 1) * LANES]
    u_sb = jnp.concatenate([usb_ref[jl] for jl in range(wslabs)], axis=1)

    for hf in range(S5_HALVES):
        ub = u_sb[:, hf * S5_HALF_CH:(hf + 1) * S5_HALF_CH].astype(_BF16)
        bu = _dot(ub, bblk_ref[hf])
        for j in range(S5_SLABS):
            xr_ref[j, pl.ds(hf, nrow, stride=S5_HALVES), :] = bu[:, j * LANES:(j + 1) * LANES]
            xi_ref[j, pl.ds(hf, nrow, stride=S5_HALVES), :] = bu[:, (S5_SLABS + j) * LANES:
                                                                 (S5_SLABS + j + 1) * LANES]

    for j0 in range(0, S5_SLABS, slabs_per_pass):
        js = list(range(j0, j0 + slabs_per_pass))
        ars = [ar_ref[j] for j in js]
        ais = [ai_ref[j] for j in js]

        def body(s, carry, js=js, ars=ars, ais=ais):
            r0 = pl.multiple_of(s * nchain, nchain)
            new = []
            for n, j in enumerate(js):
                sr, si = carry[2 * n], carry[2 * n + 1]
                br = xr_ref[j, pl.ds(r0, nchain), :]
                bi = xi_ref[j, pl.ds(r0, nchain), :]
                nr = (ars[n] * sr - ais[n] * si) + br
                ni = (ars[n] * si + ais[n] * sr) + bi
                xr_ref[j, pl.ds(r0, nchain), :] = nr
                xi_ref[j, pl.ds(r0, nchain), :] = ni
                new += [nr, ni]
            return tuple(new)

        init = []
        for j in js:
            init += [sr_ref[j], si_ref[j]]
        fin = lax.fori_loop(0, tt, body, tuple(init), unroll=4)
        for n, j in enumerate(js):
            sr_ref[j] = fin[2 * n]
            si_ref[j] = fin[2 * n + 1]

    halves = []
    for hf in range(S5_HALVES):
        xs = [xr_ref[j, pl.ds(hf, nrow, stride=S5_HALVES), :] for j in range(S5_SLABS)]
        xs += [xi_ref[j, pl.ds(hf, nrow, stride=S5_HALVES), :] for j in range(S5_SLABS)]
        halves.append(_dot(jnp.concatenate(xs, axis=1).astype(_BF16), cblk_ref[hf]))
    y = jnp.concatenate(halves, axis=1)
    y = y + dskip_ref[...] * u_sb
    y = _gelu_tanh(y)
    y = y * _sigmoid(_dot(y.astype(_BF16), wglu_ref[...]) + bglu_ref[...])
    for jl in range(wslabs):
        ysb_ref[jl] = y[:, jl * LANES:(jl + 1) * LANES]
    for b in range(nbatch):
        yb = jnp.concatenate([ysb_ref[jl, pl.ds(b, tt, stride=nbatch), :] for jl in range(wslabs)], axis=1)
        o_ref[b] = (yb * zb_ref[b].astype(_F32)).astype(_BF16)


def _s5_scan(u, zb, ar_t, ai_t, bblk, cblk, d_skip, w_glu, b_glu, tt=256, slabs_per_pass=8):
    bsz, s, w = u.shape
    nchain = bsz * S5_HALVES
    assert nchain == SUBLANES, "the scan packs batch x state-half chains into the 8 sublanes"
    tok_spec = pl.BlockSpec((bsz, tt, w), lambda t: (0, t, 0))
    const = lambda shape: pl.BlockSpec(shape, lambda t: (0,) * len(shape), pipeline_mode=pl.Buffered(1))
    return pl.pallas_call(
        functools.partial(_s5_kernel, tt=tt, nbatch=bsz, slabs_per_pass=slabs_per_pass),
        grid=(s // tt,),
        in_specs=[tok_spec, tok_spec, const(ar_t.shape), const(ai_t.shape),
                  const(bblk.shape), const(cblk.shape), const((1, w)), const(w_glu.shape), const((1, w))],
        out_specs=tok_spec,
        out_shape=jax.ShapeDtypeStruct((bsz, s, w), _BF16),
        scratch_shapes=[pltpu.VMEM((S5_SLABS, tt * nchain, LANES), _F32),
                        pltpu.VMEM((S5_SLABS, tt * nchain, LANES), _F32),
                        pltpu.VMEM((S5_SLABS, nchain, LANES), _F32),
                        pltpu.VMEM((S5_SLABS, nchain, LANES), _F32),
                        pltpu.VMEM((w // LANES, bsz * tt, LANES), _F32),
                        pltpu.VMEM((w // LANES, bsz * tt, LANES), _F32)],
        compiler_params=pltpu.CompilerParams(
            dimension_semantics=("arbitrary",), vmem_limit_bytes=VMEM_LIMIT),
        name="s5_scan",
    )(u, zb, ar_t, ai_t, bblk, cblk, d_skip.reshape(1, w), w_glu, b_glu.reshape(1, w))


def _out_kernel(x_ref, ya_ref, yb_ref, ga_ref, gb_ref, gate_ref, wua_ref, wub_ref, wo_ref, gf_ref, o_ref):
    ma = ga_ref[0].astype(_F32) * _dot(ya_ref[0], wua_ref[...])
    mb = gb_ref[0].astype(_F32) * _dot(yb_ref[0], wub_ref[...])
    merged = (ma + mb).astype(_BF16)
    xn = x_ref[0] + gate_ref[0] * _dot(merged, wo_ref[...])
    ms = jnp.mean(xn * xn, axis=-1, keepdims=True)
    o_ref[0] = xn * lax.rsqrt(ms + EPS) * gf_ref[...]


def _out_proj(x, ya, yb, ga, gb, gate, w_up_a, w_up_b, w_out, g_final, tm=1024):
    bsz, s, d = x.shape
    tok_spec = lambda width: pl.BlockSpec((1, tm, width), lambda b, i: (b, i, 0))
    const = lambda shape: pl.BlockSpec(shape, lambda b, i: (0,) * len(shape), pipeline_mode=pl.Buffered(1))
    return pl.pallas_call(
        _out_kernel,
        grid=(bsz, s // tm),
        in_specs=[tok_spec(d), tok_spec(FOX_WIDTH), tok_spec(S5_WIDTH), tok_spec(d), tok_spec(d),
                  pl.BlockSpec((1, 1, d), lambda b, i: (b, 0, 0)),
                  const(w_up_a.shape), const(w_up_b.shape), const(w_out.shape), const((1, d))],
        out_specs=tok_spec(d),
        out_shape=jax.ShapeDtypeStruct((bsz, s, d), x.dtype),
        compiler_params=pltpu.CompilerParams(
            dimension_semantics=("arbitrary", "arbitrary"), vmem_limit_bytes=VMEM_LIMIT),
        name="out_proj",
    )(x, ya, yb, ga, gb, gate, w_up_a, w_up_b, w_out, g_final.reshape(1, d))


def _chain_tiles(a, nbatch):
    a = a.reshape(S5_HALVES, S5_SLABS, LANES)
    a = jnp.swapaxes(a, 0, 1)
    return jnp.tile(a, (1, nbatch, 1))


def kernel(x, c, w_ada, b_ada, g_norm, w_in, b_f, a_re, a_im, log_dt, b_re, b_im, c_re, c_im,
           d_skip, w_glu, b_glu, w_up_a, w_up_b, w_out, g_final):
    bsz, s, d = x.shape
    depth = w_ada.shape[0]
    assert depth == 1 and d == D_MODEL

    mod = _adaln_mod(c, w_ada[0], b_ada[0])
    shift = mod[:, 0:d].reshape(bsz, 1, d)
    scale = mod[:, d:2 * d].reshape(bsz, 1, d)
    gate = mod[:, 2 * d:3 * d].reshape(bsz, 1, d)

    wt = jnp.swapaxes(w_in[0], 0, 1)
    o_z = 3 * FOX_WIDTH + HEADS
    wt_a = jnp.pad(wt[0:o_z], ((0, HEADS), (0, 0))).astype(_BF16)
    wt_b = wt[o_z:].astype(_BF16)

    q, kt, v, qaug, kaugt, za, u, zb, ga, gb, stats = _in_proj(
        x, g_norm[0], scale, shift, wt_a, wt_b, b_f[0])

    base = _skip_plan(stats)
    base_tab = jnp.swapaxes(base[:, :, 0:HEADS], 1, 2).reshape(-1)
    fend = stats[:, ST_FEND:ST_FEND + KB_PER_TILE, :, 0:HEADS]
    fend_tab = jnp.transpose(fend, (0, 3, 2, 1)).reshape(-1)
    ya = _fox_attention(base_tab, fend_tab, q, kt, kaugt, v, qaug, za)

    abr, abi, bblk, cblk = _s5_params(a_re[0], a_im[0], log_dt[0],
                                      jnp.swapaxes(b_re[0], 1, 2), jnp.swapaxes(b_im[0], 1, 2),
                                      jnp.swapaxes(c_re[0], 1, 2), jnp.swapaxes(c_im[0], 1, 2))
    yb = _s5_scan(u, zb, _chain_tiles(abr, bsz), _chain_tiles(abi, bsz), bblk, cblk,
                  d_skip[0], w_glu[0].astype(_BF16), b_glu[0])

    return _out_proj(x, ya, yb, ga, gb, gate, w_up_a[0].astype(_BF16), w_up_b[0].astype(_BF16),
                     w_out[0].astype(_BF16), g_final)
```
